```python
import math
import jax
import jax.numpy as jnp
from jax import lax
import numpy as np

D_MODEL = 2048
BATCH = 2
SEQ = 4096
DEPTH = 2
DEC_BATCH = 32
DEC_SEQ = 1
PAST_LEN = 8192
PAGE_SIZE = 128

XA_W = D_MODEL // 4
XA_H = 4
XA_HD = XA_W // XA_H
FOX_W = D_MODEL - XA_W
FOX_HD = 128
FOX_H = FOX_W // FOX_HD
S5_W = FOX_W
S5_P = 16
S5_G = S5_W // S5_P
S5_N = 64
MIX_W = FOX_W + XA_W
D_FF = -(-8 * D_MODEL // (3 * 256)) * 256
N_MEM = 256
Q_BLOCK = 128
EPS = 1e-6
NEG_INF = -1e30
FORGET_BIAS = 3.0
N_FOX = (DEPTH + 1) // 2
N_S5 = DEPTH // 2

kernel_name = 'fox_s5_memxattn_hybrid_step'


def rmsnorm(x, g):
    xf = x.astype(jnp.float32)
    y = xf * lax.rsqrt(jnp.mean(xf * xf, axis=-1, keepdims=True) + EPS)
    return (y * g.astype(jnp.float32)).astype(x.dtype)


def swiglu(x, g, w_gate, w_up, w_down):
    h = rmsnorm(x, g)
    return (jax.nn.silu(h @ w_gate) * (h @ w_up)) @ w_down


def fox_project(z, b_f, g_q, g_k):
    Bt, T, _ = z.shape
    q = rmsnorm(z[..., :FOX_W].reshape(Bt, T, FOX_H, FOX_HD), g_q)
    k = rmsnorm(z[..., FOX_W:2 * FOX_W].reshape(Bt, T, FOX_H, FOX_HD), g_k)
    v = z[..., 2 * FOX_W:3 * FOX_W].reshape(Bt, T, FOX_H, FOX_HD)
    f = z[..., 3 * FOX_W:3 * FOX_W + FOX_H]
    logf = jax.nn.log_sigmoid(f.astype(jnp.float32) + b_f.astype(jnp.float32))
    return q, k, v, logf


def fox_attention(q, k, v, c_q, c_k, q_pos, k_pos):
    Bt, T, H, Dh = q.shape
    blk = min(Q_BLOCK, T)
    nb = -(-T // blk)
    pad = nb * blk - T
    if pad:
        q = jnp.pad(q, ((0, 0), (0, pad), (0, 0), (0, 0)))
        c_q = jnp.pad(c_q, ((0, 0), (0, pad), (0, 0)), mode='edge')
        q_pos = jnp.pad(q_pos, (0, pad), mode='edge')
    qb = q.reshape(Bt, nb, blk, H, Dh).transpose(1, 0, 2, 3, 4)
    cb = c_q.reshape(Bt, nb, blk, H).transpose(1, 0, 2, 3)
    pb = q_pos.reshape(nb, blk)
    ck = c_k.transpose(0, 2, 1)[:, :, None, :]
    scale = Dh ** -0.5

    def one_block(args):
        qi, ci, pi = args
        s = jnp.einsum('bqhd,bkhd->bhqk', qi, k, preferred_element_type=jnp.float32) * scale
        s = s + ci.transpose(0, 2, 1)[..., None] - ck
        mask = k_pos[None, :] <= pi[:, None]
        s = jnp.where(mask[None, None], s, NEG_INF)
        p = jax.nn.softmax(s, axis=-1)
        return jnp.einsum('bhqk,bkhd->bqhd', p.astype(v.dtype), v)

    o = lax.map(one_block, (qb, cb, pb))
    o = o.transpose(1, 0, 2, 3, 4).reshape(Bt, nb * blk, H, Dh)[:, :T]
    return o.reshape(Bt, T, H * Dh)


def _complex_affine_combine(e1, e2):
    a1r, a1i, b1r, b1i = e1
    a2r, a2i, b2r, b2i = e2
    return (a2r * a1r - a2i * a1i,
            a2r * a1i + a2i * a1r,
            a2r * b1r - a2i * b1i + b2r,
            a2r * b1i + a2i * b1r + b2i)


def s5_mixer(u, h0_re, h0_im, a_re, a_im, log_dt, b_re, b_im, c_re, c_im, d, w_glu, b_glu):
    f32 = jnp.float32
    Bt, T, _ = u.shape
    ug = u.reshape(Bt, T, S5_G, S5_P).astype(f32)
    lam_re = a_re.astype(f32)
    lam_im = a_im.astype(f32)
    dt = jnp.exp(log_dt.astype(f32))[:, None]
    mag = jnp.exp(lam_re * dt)
    ab_re = mag * jnp.cos(lam_im * dt)
    ab_im = mag * jnp.sin(lam_im * dt)
    num_re = ab_re - 1.0
    num_im = ab_im
    den = lam_re * lam_re + lam_im * lam_im
    z_re = (num_re * lam_re + num_im * lam_im) / den
    z_im = (num_im * lam_re - num_re * lam_im) / den
    br = b_re.astype(f32)
    bi = b_im.astype(f32)
    bb_re = z_re[..., None] * br - z_im[..., None] * bi
    bb_im = z_re[..., None] * bi + z_im[..., None] * br
    bu_re = jnp.einsum('btgp,gnp->btgn', ug, bb_re)
    bu_im = jnp.einsum('btgp,gnp->btgn', ug, bb_im)
    h0r = h0_re.astype(f32)
    h0i = h0_im.astype(f32)
    bu_re = bu_re.at[:, 0].add(ab_re * h0r - ab_im * h0i)
    bu_im = bu_im.at[:, 0].add(ab_re * h0i + ab_im * h0r)
    a_seq_re = jnp.broadcast_to(ab_re, (1, T, S5_G, S5_N))
    a_seq_im = jnp.broadcast_to(ab_im, (1, T, S5_G, S5_N))
    _, _, h_re, h_im = lax.associative_scan(_complex_affine_combine, (a_seq_re, a_seq_im, bu_re, bu_im), axis=1)
    y = (jnp.einsum('btgn,gpn->btgp', h_re, c_re.astype(f32))
         - jnp.einsum('btgn,gpn->btgp', h_im, c_im.astype(f32))
         + d.astype(f32) * ug).reshape(Bt, T, S5_W)
    y = jax.nn.gelu(y)
    y = y * jax.nn.sigmoid(y @ w_glu.astype(f32) + b_glu.astype(f32))
    return y.astype(u.dtype), h_re[:, -1], h_im[:, -1]


def memory_kv(mem, g_mem, w_kv, g_k):
    Bt, M, _ = mem.shape
    kv = rmsnorm(mem, g_mem) @ w_kv
    k = rmsnorm(kv[..., :XA_W].reshape(Bt, M, XA_H, XA_HD), g_k)
    v = kv[..., XA_W:].reshape(Bt, M, XA_H, XA_HD)
    return k, v


def cross_attend(zq, g_q, k, v):
    Bt, T, _ = zq.shape
    q = rmsnorm(zq.reshape(Bt, T, XA_H, XA_HD), g_q)
    s = jnp.einsum('bthd,bmhd->bhtm', q, k, preferred_element_type=jnp.float32) * (XA_HD ** -0.5)
    p = jax.nn.softmax(s, axis=-1)
    o = jnp.einsum('bhtm,bmhd->bthd', p.astype(v.dtype), v)
    return o.reshape(Bt, T, XA_W)


def setup_inputs(seed: int = 0) -> dict:
    key = jax.random.key(seed)
    ks = iter(jax.random.split(key, 64))
    f32 = jnp.float32

    def nrm(shape, scale=1.0):
        return jax.random.normal(next(ks), shape, f32) * scale

    def gain(shape):
        return 1.0 + nrm(shape, 0.02)

    n_pages = PAST_LEN // PAGE_SIZE
    n_used = DEC_BATCH * n_pages
    n_pool = n_used + -(-n_used // 4)
    page_table = jax.random.permutation(next(ks), n_pool)[:n_used].reshape(DEC_BATCH, n_pages).astype(jnp.int32)
    w_in_fox_cols = 3 * FOX_W + FOX_H + XA_W
    w_in_s5_cols = S5_W + XA_W
    log_lo, log_hi = math.log(0.001), math.log(0.1)
    return {
        'x_prompt': nrm((BATCH, SEQ, D_MODEL)),
        'x_sample': nrm((DEC_BATCH, DEC_SEQ, D_MODEL)),
        'cache_fox_k': nrm((N_FOX, n_pool, PAGE_SIZE, FOX_H, FOX_HD)),
        'cache_fox_v': nrm((N_FOX, n_pool, PAGE_SIZE, FOX_H, FOX_HD)),
        'cache_fox_logf': jax.nn.log_sigmoid(FORGET_BIAS + nrm((N_FOX, n_pool, PAGE_SIZE, FOX_H))),
        'state_s5_re': nrm((N_S5, DEC_BATCH, S5_G, S5_N), 0.5),
        'state_s5_im': nrm((N_S5, DEC_BATCH, S5_G, S5_N), 0.5),
        'cache_mem_k': nrm((DEPTH, DEC_BATCH, N_MEM, XA_H, XA_HD)),
        'cache_mem_v': nrm((DEPTH, DEC_BATCH, N_MEM, XA_H, XA_HD)),
        'page_table': page_table,
        'mem_prompt': nrm((BATCH, N_MEM, D_MODEL)),
        'norm1_g': gain((DEPTH, D_MODEL)),
        'w_out': nrm((DEPTH, MIX_W, D_MODEL), MIX_W ** -0.5),
        'mem_norm_g': gain((DEPTH, D_MODEL)),
        'w_mem_kv': nrm((DEPTH, D_MODEL, 2 * XA_W), D_MODEL ** -0.5),
        'xq_norm_g': gain((DEPTH, XA_HD)),
        'xk_norm_g': gain((DEPTH, XA_HD)),
        'norm2_g': gain((DEPTH, D_MODEL)),
        'w_ffn_gate': nrm((DEPTH, D_MODEL, D_FF), D_MODEL ** -0.5),
        'w_ffn_up': nrm((DEPTH, D_MODEL, D_FF), D_MODEL ** -0.5),
        'w_ffn_down': nrm((DEPTH, D_FF, D_MODEL), D_FF ** -0.5),
        'fox_w_in': nrm((N_FOX, D_MODEL, w_in_fox_cols), D_MODEL ** -0.5),
        'fox_b_f': FORGET_BIAS + nrm((N_FOX, FOX_H), 0.1),
        'fox_q_norm_g': gain((N_FOX, FOX_HD)),
        'fox_k_norm_g': gain((N_FOX, FOX_HD)),
        's5_w_in': nrm((N_S5, D_MODEL, w_in_s5_cols), D_MODEL ** -0.5),
        's5_a_re': -0.5 * jnp.exp(nrm((N_S5, S5_G, S5_N), 0.01)),
        's5_a_im': jnp.broadcast_to(jnp.pi * jnp.arange(S5_N, dtype=f32), (N_S5, S5_G, S5_N)) + 0.0,
        's5_log_dt': log_lo + jax.random.uniform(next(ks), (N_S5, S5_G), f32) * (log_hi - log_lo),
        's5_b_re': nrm((N_S5, S5_G, S5_N, S5_P), (2 * S5_P) ** -0.5),
        's5_b_im': nrm((N_S5, S5_G, S5_N, S5_P), (2 * S5_P) ** -0.5),
        's5_c_re': nrm((N_S5, S5_G, S5_P, S5_N), (2 * S5_N) ** -0.5),
        's5_c_im': nrm((N_S5, S5_G, S5_P, S5_N), (2 * S5_N) ** -0.5),
        's5_d': nrm((N_S5, S5_G, S5_P)),
        's5_w_glu': nrm((N_S5, S5_W, S5_W), S5_W ** -0.5),
        's5_b_glu': nrm((N_S5, S5_W), 0.01),
    }


def reference(x_prompt, x_sample, cache_fox_k, cache_fox_v, cache_fox_logf, state_s5_re, state_s5_im,
              cache_mem_k, cache_mem_v, page_table, mem_prompt,
              norm1_g, w_out, mem_norm_g, w_mem_kv, xq_norm_g, xk_norm_g,
              norm2_g, w_ffn_gate, w_ffn_up, w_ffn_down,
              fox_w_in, fox_b_f, fox_q_norm_g, fox_k_norm_g,
              s5_w_in, s5_a_re, s5_a_im, s5_log_dt, s5_b_re, s5_b_im, s5_c_re, s5_c_im, s5_d,
              s5_w_glu, s5_b_glu):
    xp, xs = x_prompt, x_sample
    Bp, Tp, _ = xp.shape
    Bs, Ts, _ = xs.shape
    past = page_table.shape[1] * PAGE_SIZE
    fk_p, fv_p, fl_p, fk_s, fv_s, fl_s = [], [], [], [], [], []
    sr_p, si_p, sr_s, si_s = [], [], [], []
    mk_p, mv_p = [], []
    for i in range(DEPTH):
        j = i // 2
        hp = rmsnorm(xp, norm1_g[i])
        hs = rmsnorm(xs, norm1_g[i])
        if i % 2 == 0:
            zp = hp @ fox_w_in[j]
            zs = hs @ fox_w_in[j]
            q, k, v, lf = fox_project(zp, fox_b_f[j], fox_q_norm_g[j], fox_k_norm_g[j])
            c = jnp.cumsum(lf, axis=1)
            pos = jnp.arange(Tp, dtype=jnp.int32)
            mix_p = fox_attention(q, k, v, c, c, pos, pos)
            fk_p.append(k.reshape(Bp * Tp // PAGE_SIZE, PAGE_SIZE, FOX_H, FOX_HD))
            fv_p.append(v.reshape(Bp * Tp // PAGE_SIZE, PAGE_SIZE, FOX_H, FOX_HD))
            fl_p.append(lf.reshape(Bp * Tp // PAGE_SIZE, PAGE_SIZE, FOX_H))
            q, k, v, lf = fox_project(zs, fox_b_f[j], fox_q_norm_g[j], fox_k_norm_g[j])
            k_past = cache_fox_k[j, page_table].reshape(Bs, past, FOX_H, FOX_HD)
            v_past = cache_fox_v[j, page_table].reshape(Bs, past, FOX_H, FOX_HD)
            lf_past = cache_fox_logf[j, page_table].reshape(Bs, past, FOX_H).astype(jnp.float32)
            k_all = jnp.concatenate([k_past, k.astype(k_past.dtype)], axis=1)
            v_all = jnp.concatenate([v_past, v.astype(v_past.dtype)], axis=1)
            c = jnp.cumsum(jnp.concatenate([lf_past, lf], axis=1), axis=1)
            mix_s = fox_attention(q, k_all, v_all, c[:, past:], c,
                                  past + jnp.arange(Ts, dtype=jnp.int32),
                                  jnp.arange(past + Ts, dtype=jnp.int32))
            fk_s.append(k)
            fv_s.append(v)
            fl_s.append(lf)
        else:
            zp = hp @ s5_w_in[j]
            zs = hs @ s5_w_in[j]
            prm = (s5_a_re[j], s5_a_im[j], s5_log_dt[j], s5_b_re[j], s5_b_im[j],
                   s5_c_re[j], s5_c_im[j], s5_d[j], s5_w_glu[j], s5_b_glu[j])
            h0 = jnp.zeros((Bp, S5_G, S5_N), jnp.float32)
            mix_p, hr, hi = s5_mixer(zp[..., :S5_W], h0, h0, *prm)
            sr_p.append(hr)
            si_p.append(hi)
            mix_s, hr, hi = s5_mixer(zs[..., :S5_W], state_s5_re[j], state_s5_im[j], *prm)
            sr_s.append(hr)
            si_s.append(hi)
        mk, mv = memory_kv(mem_prompt, mem_norm_g[i], w_mem_kv[i], xk_norm_g[i])
        mk_p.append(mk)
        mv_p.append(mv)
        xa_p = cross_attend(zp[..., -XA_W:], xq_norm_g[i], mk, mv)
        xa_s = cross_attend(zs[..., -XA_W:], xq_norm_g[i], cache_mem_k[i], cache_mem_v[i])
        xp = xp + jnp.concatenate([mix_p, xa_p.astype(mix_p.dtype)], axis=-1) @ w_out[i]
        xs = xs + jnp.concatenate([mix_s, xa_s.astype(mix_s.dtype)], axis=-1) @ w_out[i]
        xp = xp + swiglu(xp, norm2_g[i], w_ffn_gate[i], w_ffn_up[i], w_ffn_down[i])
        xs = xs + swiglu(xs, norm2_g[i], w_ffn_gate[i], w_ffn_up[i], w_ffn_down[i])
    return (xp, xs,
            jnp.stack(fk_p), jnp.stack(fv_p), jnp.stack(fl_p),
            jnp.stack(fk_s), jnp.stack(fv_s), jnp.stack(fl_s),
            jnp.stack(sr_p), jnp.stack(si_p), jnp.stack(sr_s), jnp.stack(si_s),
            jnp.stack(mk_p), jnp.stack(mv_p))
```

```python
import functools
import math

import jax
import jax.numpy as jnp
from jax import lax
from jax.experimental import pallas as pl
from jax.experimental.pallas import tpu as pltpu

F32 = jnp.float32
BF16 = jnp.bfloat16

EPS = 1e-6
NEG_INF = -1e30
LANE = 128
SUBLANE = 8
HEAD_ROWS = 16
VMEM_LIMIT = 52 * 1024 * 1024
NT_DIMS = (((1,), (1,)), ((), ()))


def _params(sem):
    return pltpu.CompilerParams(dimension_semantics=sem, vmem_limit_bytes=VMEM_LIMIT)


def _split3(x):
    hi = x.astype(BF16)
    r1 = x - hi.astype(F32)
    mid = r1.astype(BF16)
    lo = (r1 - mid.astype(F32)).astype(BF16)
    return hi, mid, lo


def _dot_exact01(x, w01):
    r = jnp.dot(jnp.concatenate(_split3(x), axis=0), w01, preferred_element_type=F32)
    return r[0:HEAD_ROWS] + r[HEAD_ROWS:2 * HEAD_ROWS] + r[2 * HEAD_ROWS:3 * HEAD_ROWS]


def _log_sigmoid(x):
    return jnp.minimum(x, 0.0) - jnp.log1p(jnp.exp(-jnp.abs(x)))


def _head_rmsnorm(y, gain_row):
    parts = []
    for c in range(y.shape[-1] // LANE):
        p = y[:, c * LANE:(c + 1) * LANE]
        parts.append(p * lax.rsqrt(jnp.mean(p * p, axis=-1, keepdims=True) + EPS))
    return jnp.concatenate(parts, axis=-1) * gain_row


def _proj_kernel(*refs, segs, n_out, with_f, cumsum, seq_tiles):
    x_ref, g_ref, w_ref, gain_ref = refs[:4]
    pos = 4
    if with_f:
        wft_ref, bf_ref = refs[4:6]
        pos = 6
    out_refs = refs[pos:pos + n_out]
    pos += n_out
    if with_f:
        lft_ref = refs[pos]
        pos += 1
        if cumsum:
            ct_ref = refs[pos]
            pos += 1
    h_scr = refs[pos]
    pos += 1
    if with_f and cumsum:
        tri_scr, carry_scr = refs[pos:pos + 2]

    i = pl.program_id(0)
    j = pl.program_id(1)
    tm = x_ref.shape[0]

    if with_f and cumsum:
        @pl.when((i == 0) & (j == 0))
        def _():
            u = lax.broadcasted_iota(jnp.int32, (tm, tm), 0)
            t = lax.broadcasted_iota(jnp.int32, (tm, tm), 1)
            tri_scr[...] = jnp.where(u <= t, 1.0, 0.0).astype(BF16)

    @pl.when(j == 0)
    def _():
        x = x_ref[...]
        h = x * lax.rsqrt(jnp.mean(x * x, axis=-1, keepdims=True) + EPS) * g_ref[...]
        hb = h.astype(BF16)
        h_scr[...] = hb
        if with_f:
            f = lax.dot_general(wft_ref[...], hb, NT_DIMS, preferred_element_type=F32)
            lf = _log_sigmoid(f + bf_ref[...])
            lft_ref[...] = lf
            if cumsum:
                @pl.when(i % seq_tiles == 0)
                def _():
                    carry_scr[...] = jnp.zeros_like(carry_scr)
                c = _dot_exact01(lf, tri_scr[...]) + carry_scr[...]
                ct_ref[...] = c
                carry_scr[...] = c[:, tm - 1:tm]

    y = jnp.dot(h_scr[...], w_ref[...], preferred_element_type=F32)
    for start, n_tiles, norm, outs in segs:
        @pl.when((j >= start) & (j < start + n_tiles))
        def _(norm=norm, outs=outs):
            yy = _head_rmsnorm(y, gain_ref[...]) if norm else y
            for o in outs:
                out_refs[o][...] = yy.astype(out_refs[o].dtype)


def _proj(x, g, w, gain_row, segs, out_dtypes, *, tm, tn, wft=None, bf=None, seq_len=None):
    m, k = x.shape
    n = w.shape[1]
    with_f = wft is not None
    cumsum = seq_len is not None
    grid = (m // tm, n // tn)
    seg_of_out = {}
    for start, n_tiles, _, outs in segs:
        for o in outs:
            seg_of_out[o] = (start, n_tiles)
    in_specs = [
        pl.BlockSpec((tm, k), lambda i, j: (i, 0)),
        pl.BlockSpec((1, k), lambda i, j: (0, 0)),
        pl.BlockSpec((k, tn), lambda i, j: (0, j)),
        pl.BlockSpec((1, tn), lambda i, j: (0, j)),
    ]
    args = [x, g.reshape(1, k), w, gain_row]
    if with_f:
        in_specs += [pl.BlockSpec((HEAD_ROWS, k), lambda i, j: (0, 0)),
                     pl.BlockSpec((HEAD_ROWS, 1), lambda i, j: (0, 0))]
        args += [wft, bf]
    out_shapes, out_specs = [], []
    for o, dt in enumerate(out_dtypes):
        start, n_tiles = seg_of_out[o]
        out_shapes.append(jax.ShapeDtypeStruct((m, n_tiles * tn), dt))
        out_specs.append(pl.BlockSpec(
            (tm, tn), lambda i, j, s=start, nt=n_tiles: (i, jnp.clip(j - s, 0, nt - 1))))
    scratch = [pltpu.VMEM((tm, k), BF16)]
    seq_tiles = 1
    if with_f:
        t_len = seq_len if cumsum else m
        seq_tiles = t_len // tm
        nb = m // t_len
        n_f = 2 if cumsum else 1
        for _ in range(n_f):
            out_shapes.append(jax.ShapeDtypeStruct((nb, HEAD_ROWS, t_len), F32))
            out_specs.append(pl.BlockSpec(
                (None, HEAD_ROWS, tm), lambda i, j, st=seq_tiles: (i // st, 0, i % st)))
        if cumsum:
            scratch += [pltpu.VMEM((tm, tm), BF16), pltpu.VMEM((HEAD_ROWS, 1), F32)]
    kern = functools.partial(_proj_kernel, segs=segs, n_out=len(out_dtypes), with_f=with_f,
                             cumsum=cumsum, seq_tiles=seq_tiles)
    return pl.pallas_call(
        kern, grid=grid, in_specs=in_specs, out_specs=out_specs, out_shape=out_shapes,
        scratch_shapes=scratch, compiler_params=_params(("arbitrary", "arbitrary")),
        name="norm_proj")(*args)


def _flash_kernel(q_ref, k_ref, v_ref, cq_ref, ck_ref, o_ref, m_scr, l_scr, acc_scr, *, scale):
    i = pl.program_id(2)
    j = pl.program_id(3)

    @pl.when(j == 0)
    def _():
        m_scr[...] = jnp.full_like(m_scr, NEG_INF)
        l_scr[...] = jnp.zeros_like(l_scr)
        acc_scr[...] = jnp.zeros_like(acc_scr)

    def step(masked):
        s = lax.dot_general(q_ref[...], k_ref[...], NT_DIMS, preferred_element_type=F32) * scale
        s = s + cq_ref[...] - ck_ref[...]
        if masked:
            row = lax.broadcasted_iota(jnp.int32, s.shape, 0)
            col = lax.broadcasted_iota(jnp.int32, s.shape, 1)
            s = jnp.where(col <= row, s, NEG_INF)
        m_old = m_scr[...]
        m_new = jnp.maximum(m_old, jnp.max(s, axis=-1, keepdims=True))
        alpha = jnp.exp(m_old - m_new)
        p = jnp.exp(s - m_new)
        l_scr[...] = alpha * l_scr[...] + jnp.sum(p, axis=-1, keepdims=True)
        acc_scr[...] = alpha * acc_scr[...] + jnp.dot(p.astype(BF16), v_ref[...],
                                                      preferred_element_type=F32)
        m_scr[...] = m_new

    @pl.when(j < i)
    def _():
        step(False)

    @pl.when(j == i)
    def _():
        step(True)
        o_ref[...] = (acc_scr[...] / l_scr[...]).astype(o_ref.dtype)


def _flash(q, k, v, c_col, c_row, *, tq):
    b, t, w = q.shape
    h = w // LANE
    nq = t // tq
    kern = functools.partial(_flash_kernel, scale=LANE ** -0.5)
    kv_map = lambda bb, hh, i, j: (bb, jnp.minimum(j, i), hh)
    return pl.pallas_call(
        kern, grid=(b, h, nq, nq),
        in_specs=[
            pl.BlockSpec((None, tq, LANE), lambda bb, hh, i, j: (bb, i, hh)),
            pl.BlockSpec((None, tq, LANE), kv_map),
            pl.BlockSpec((None, tq, LANE), kv_map),
            pl.BlockSpec((None, None, tq, 1), lambda bb, hh, i, j: (bb, hh, i, 0)),
            pl.BlockSpec((None, None, 1, tq), lambda bb, hh, i, j: (bb, hh, 0, jnp.minimum(j, i))),
        ],
        out_specs=pl.BlockSpec((None, tq, LANE), lambda bb, hh, i, j: (bb, i, hh)),
        out_shape=jax.ShapeDtypeStruct((b, t, w), BF16),
        scratch_shapes=[pltpu.VMEM((tq, 1), F32), pltpu.VMEM((tq, 1), F32), pltpu.VMEM((tq, LANE), F32)],
        compiler_params=_params(("arbitrary",) * 4), name="fox_prompt_attn")(q, k, v, c_col, c_row)


def _xattn_kernel(q_ref, k_ref, v_ref, o_ref, *, scale):
    for h in range(q_ref.shape[-1] // LANE):
        sl = slice(h * LANE, (h + 1) * LANE)
        s = lax.dot_general(q_ref[:, sl], k_ref[:, sl], NT_DIMS, preferred_element_type=F32) * scale
        p = jnp.exp(s - jnp.max(s, axis=-1, keepdims=True))
        p = p / jnp.sum(p, axis=-1, keepdims=True)
        o_ref[:, sl] = jnp.dot(p.astype(BF16), v_ref[:, sl],
                               preferred_element_type=F32).astype(o_ref.dtype)


def _xattn(q, k, v, *, tq):
    b, t, w = q.shape
    n_mem = k.shape[1]
    kern = functools.partial(_xattn_kernel, scale=LANE ** -0.5)
    return pl.pallas_call(
        kern, grid=(b, t // tq),
        in_specs=[pl.BlockSpec((None, tq, w), lambda bb, i: (bb, i, 0)),
                  pl.BlockSpec((None, n_mem, w), lambda bb, i: (bb, 0, 0)),
                  pl.BlockSpec((None, n_mem, w), lambda bb, i: (bb, 0, 0))],
        out_specs=pl.BlockSpec((None, tq, w), lambda bb, i: (bb, i, 0)),
        out_shape=jax.ShapeDtypeStruct((b, t, w), BF16),
        compiler_params=_params(("arbitrary", "arbitrary")), name="mem_xattn_prompt")(q, k, v)


GROUP = SUBLANE


def _key_major(ref):
    keys, n_heads, _ = ref.shape
    mats = []
    for h0 in range(0, n_heads, GROUP):
        nh = min(GROUP, n_heads - h0)
        x = ref[:, h0:h0 + nh, :]
        if nh < GROUP:
            x = jnp.concatenate([x, jnp.zeros((keys, GROUP - nh, LANE), x.dtype)], axis=1)
        mats.append(x.reshape(keys * GROUP, LANE).astype(BF16))
    return mats[0] if len(mats) == 1 else jnp.concatenate(mats, axis=0)


def _own_head_mask(keys, n_groups):
    cols = n_groups * keys * GROUP
    row = lax.broadcasted_iota(jnp.int32, (HEAD_ROWS, cols), 0)
    col = lax.broadcasted_iota(jnp.int32, (HEAD_ROWS, cols), 1)
    return (row // GROUP == col // (keys * GROUP)) & (row % GROUP == col % GROUP)


def _dec_fox_kernel(pt_ref, q_ref, kn_ref, vn_ref, lfn_ref, k_ref, v_ref, lf_ref, o_ref,
                    m_scr, l_scr, acc_scr, carry_scr, lfpad_scr, *, scale, n_pages):
    del pt_ref
    p = pl.program_id(1)
    keys, n_heads, _ = k_ref.shape
    n_groups = -(-n_heads // GROUP)

    @pl.when(p == 0)
    def _():
        qb = q_ref[...].astype(BF16).astype(F32)
        kn = kn_ref[...].astype(BF16).astype(F32)
        m_scr[...] = jnp.sum(qb * kn, axis=-1, keepdims=True) * scale
        l_scr[...] = jnp.ones_like(l_scr)
        acc_scr[...] = vn_ref[...].astype(BF16).astype(F32)
        carry_scr[...] = lfn_ref[...]
        lfpad_scr[...] = jnp.zeros_like(lfpad_scr)

    s = lax.dot_general(q_ref[...].astype(BF16), _key_major(k_ref), NT_DIMS,
                        preferred_element_type=F32) * scale

    lfpad_scr[:, 0:n_heads] = lf_ref[...]
    lft = lfpad_scr[...].T[0:HEAD_ROWS, :]
    u = lax.broadcasted_iota(jnp.int32, (keys, keys), 0)
    kk = lax.broadcasted_iota(jnp.int32, (keys, keys), 1)
    incl = _dot_exact01(lft, jnp.where(u >= kk, 1.0, 0.0).astype(BF16))
    carry = carry_scr[...]
    bias = carry + incl - lft
    carry_scr[...] = carry + incl[:, 0:1]
    ek = lax.broadcasted_iota(jnp.int32, (keys, keys * GROUP), 0)
    ec = lax.broadcasted_iota(jnp.int32, (keys, keys * GROUP), 1)
    bias_cols = _dot_exact01(bias, jnp.where(ec // GROUP == ek, 1.0, 0.0).astype(BF16))
    bias_cols = jnp.concatenate([bias_cols] * n_groups, axis=-1)
    logits = jnp.where(_own_head_mask(keys, n_groups), s + bias_cols, NEG_INF)

    m_old = m_scr[...]
    m_new = jnp.maximum(m_old, jnp.max(logits, axis=-1, keepdims=True))
    alpha = jnp.exp(m_old - m_new)
    pw = jnp.exp(logits - m_new)
    l_new = alpha * l_scr[...] + jnp.sum(pw, axis=-1, keepdims=True)
    acc = alpha * acc_scr[...] + jnp.dot(pw.astype(BF16), _key_major(v_ref),
                                         preferred_element_type=F32)
    m_scr[...] = m_new
    l_scr[...] = l_new
    acc_scr[...] = acc

    @pl.when(p == n_pages - 1)
    def _():
        o_ref[...] = acc / l_new


def _dec_fox(q, k_new, v_new, lf_new_col, cache_k, cache_v, cache_lf, page_table, layer_off):
    b = q.shape[0]
    n_pages = page_table.shape[1]
    _, page, n_heads, _ = cache_k.shape
    page_map = lambda bb, p, pt: (pt[bb, n_pages - 1 - p] + layer_off, 0, 0, 0)
    lf_map = lambda bb, p, pt: (pt[bb, n_pages - 1 - p] + layer_off, 0, 0)
    row_map = lambda bb, p, pt: (bb, 0, 0)
    kern = functools.partial(_dec_fox_kernel, scale=LANE ** -0.5, n_pages=n_pages)
    head_spec = pl.BlockSpec((None, HEAD_ROWS, LANE), row_map)
    grid_spec = pltpu.PrefetchScalarGridSpec(
        num_scalar_prefetch=1, grid=(b, n_pages),
        in_specs=[head_spec, head_spec, head_spec,
                  pl.BlockSpec((None, HEAD_ROWS, 1), row_map),
                  pl.BlockSpec((None, page, n_heads, LANE), page_map),
                  pl.BlockSpec((None, page, n_heads, LANE), page_map),
                  pl.BlockSpec((None, page, n_heads), lf_map)],
        out_specs=head_spec,
        scratch_shapes=[pltpu.VMEM((HEAD_ROWS, 1), F32), pltpu.VMEM((HEAD_ROWS, 1), F32),
                        pltpu.VMEM((HEAD_ROWS, LANE), F32), pltpu.VMEM((HEAD_ROWS, 1), F32),
                        pltpu.VMEM((page, LANE), F32)])
    return pl.pallas_call(
        kern, grid_spec=grid_spec, out_shape=jax.ShapeDtypeStruct((b, HEAD_ROWS, LANE), F32),
        compiler_params=_params(("arbitrary", "arbitrary")), name="fox_decode_attn")(
            page_table, q, k_new, v_new, lf_new_col, cache_k, cache_v, cache_lf)


def _dec_xattn_kernel(q_ref, k_ref, v_ref, o_ref, *, scale):
    keys, n_heads, _ = k_ref.shape
    n_groups = -(-n_heads // GROUP)
    s = lax.dot_general(q_ref[...].astype(BF16), _key_major(k_ref), NT_DIMS,
                        preferred_element_type=F32) * scale
    logits = jnp.where(_own_head_mask(keys, n_groups), s, NEG_INF)
    p = jnp.exp(logits - jnp.max(logits, axis=-1, keepdims=True))
    p = p / jnp.sum(p, axis=-1, keepdims=True)
    o_ref[...] = jnp.dot(p.astype(BF16), _key_major(v_ref), preferred_element_type=F32)


def _dec_xattn(q, mem_k, mem_v, layer):
    b = q.shape[0]
    _, _, n_mem, n_heads, _ = mem_k.shape
    kern = functools.partial(_dec_xattn_kernel, scale=LANE ** -0.5)
    head_spec = pl.BlockSpec((None, HEAD_ROWS, LANE), lambda bb: (bb, 0, 0))
    mem_spec = pl.BlockSpec((None, None, n_mem, n_heads, LANE), lambda bb: (layer, bb, 0, 0, 0))
    return pl.pallas_call(
        kern, grid=(b,), in_specs=[head_spec, mem_spec, mem_spec], out_specs=head_spec,
        out_shape=jax.ShapeDtypeStruct((b, HEAD_ROWS, LANE), F32),
        compiler_params=_params(("arbitrary",)), name="mem_xattn_decode")(q, mem_k, mem_v)


def _outproj_kernel(a_ref, xa_ref, wa_ref, wx_ref, x_ref, o_ref):
    o_ref[...] = (x_ref[...]
                  + jnp.dot(a_ref[...].astype(BF16), wa_ref[...], preferred_element_type=F32)
                  + jnp.dot(xa_ref[...].astype(BF16), wx_ref[...], preferred_element_type=F32))


def _outproj(a, xa, w_a, w_x, x, *, tm, tn):
    m, d = x.shape
    ka, kx = a.shape[1], xa.shape[1]
    return pl.pallas_call(
        _outproj_kernel, grid=(m // tm, d // tn),
        in_specs=[pl.BlockSpec((tm, ka), lambda i, j: (i, 0)),
                  pl.BlockSpec((tm, kx), lambda i, j: (i, 0)),
                  pl.BlockSpec((ka, tn), lambda i, j: (0, j)),
                  pl.BlockSpec((kx, tn), lambda i, j: (0, j)),
                  pl.BlockSpec((tm, tn), lambda i, j: (i, j))],
        out_specs=pl.BlockSpec((tm, tn), lambda i, j: (i, j)),
        out_shape=jax.ShapeDtypeStruct((m, d), F32),
        compiler_params=_params(("arbitrary", "arbitrary")), name="out_proj")(a, xa, w_a, w_x, x)


def _ffn_kernel(x_ref, g_ref, wg_ref, wu_ref, wd_ref, o_ref, h_scr, acc_scr):
    f = pl.program_id(1)

    @pl.when(f == 0)
    def _():
        x = x_ref[...]
        h = x * lax.rsqrt(jnp.mean(x * x, axis=-1, keepdims=True) + EPS) * g_ref[...]
        h_scr[...] = h.astype(BF16)
        acc_scr[...] = jnp.zeros_like(acc_scr)

    h = h_scr[...]
    gate = jnp.dot(h, wg_ref[...], preferred_element_type=F32)
    up = jnp.dot(h, wu_ref[...], preferred_element_type=F32)
    act = (gate * jax.nn.sigmoid(gate)) * up
    acc_scr[...] += jnp.dot(act.astype(BF16), wd_ref[...], preferred_element_type=F32)

    @pl.when(f == pl.num_programs(1) - 1)
    def _():
        o_ref[...] = x_ref[...] + acc_scr[...]


def _ffn(x, g, w_gate, w_up, w_down, *, tm, tf):
    m, d = x.shape
    d_ff = w_gate.shape[1]
    return pl.pallas_call(
        _ffn_kernel, grid=(m // tm, d_ff // tf),
        in_specs=[pl.BlockSpec((tm, d), lambda i, f: (i, 0)),
                  pl.BlockSpec((1, d), lambda i, f: (0, 0)),
                  pl.BlockSpec((d, tf), lambda i, f: (0, f)),
                  pl.BlockSpec((d, tf), lambda i, f: (0, f)),
                  pl.BlockSpec((tf, d), lambda i, f: (f, 0))],
        out_specs=pl.BlockSpec((tm, d), lambda i, f: (i, 0)),
        out_shape=jax.ShapeDtypeStruct((m, d), F32),
        scratch_shapes=[pltpu.VMEM((tm, d), BF16), pltpu.VMEM((tm, d), F32)],
        compiler_params=_params(("arbitrary", "arbitrary")), name="swiglu_ffn")(
            x, g.reshape(1, d), w_gate, w_up, w_down)


S5_CHUNK = 2 * LANE


def _gelu_tanh(y):
    return 0.5 * y * (1.0 + jnp.tanh(math.sqrt(2.0 / math.pi) * (y + 0.044715 * (y * y * y))))


def _s5_prompt_kernel(u_ref, bbre_ref, bbim_ref, cre_ref, cimn_ref, are_ref, aim_ref, d_ref,
                      wglu_ref, bglu_ref, mix_ref, hre_ref, him_ref,
                      sre, sim, hre_s, him_s, y_scr, *, lt, n_chunks):
    t = pl.program_id(1)
    tiles_per_chunk = S5_CHUNK // LANE
    chunks_per_row = are_ref.shape[0] // tiles_per_chunk
    chunks_per_tile = S5_CHUNK // (u_ref.shape[1] // n_chunks)

    @pl.when(t == 0)
    def _():
        hre_s[...] = jnp.zeros_like(hre_s)
        him_s[...] = jnp.zeros_like(him_s)

    u = u_ref[...]
    ub = u.astype(BF16)

    def put(dst, q, val):
        j, lc = divmod(q, chunks_per_row)
        for k in range(tiles_per_chunk):
            dst[lc * tiles_per_chunk + k, pl.ds(j, lt, stride=SUBLANE), :] = val[:, k * LANE:(k + 1) * LANE]

    def get(src, q):
        j, lc = divmod(q, chunks_per_row)
        return jnp.concatenate([src[lc * tiles_per_chunk + k, pl.ds(j, lt, stride=SUBLANE), :]
                                for k in range(tiles_per_chunk)], axis=-1)

    for q in range(n_chunks):
        ut = ub[:, (q // 2) * LANE:(q // 2 + 1) * LANE]
        put(sre, q, jnp.dot(ut, bbre_ref[q], preferred_element_type=F32))
        put(sim, q, jnp.dot(ut, bbim_ref[q], preferred_element_type=F32))

    a_re = are_ref[...]
    a_im = aim_ref[...]

    def body(tt, carry):
        h_re, h_im = carry
        r0 = pl.multiple_of(tt * SUBLANE, SUBLANE)
        n_re = a_re * h_re - a_im * h_im + sre[:, pl.ds(r0, SUBLANE), :]
        n_im = a_re * h_im + a_im * h_re + sim[:, pl.ds(r0, SUBLANE), :]
        sre[:, pl.ds(r0, SUBLANE), :] = n_re
        sim[:, pl.ds(r0, SUBLANE), :] = n_im
        return n_re, n_im

    h_re, h_im = lax.fori_loop(0, lt, body, (hre_s[...], him_s[...]), unroll=8)
    hre_s[...] = h_re
    him_s[...] = h_im
    hre_ref[...] = h_re
    him_ref[...] = h_im

    for c in range(n_chunks // chunks_per_tile):
        acc = jnp.zeros((lt, S5_CHUNK), F32)
        for q in range(c * chunks_per_tile, (c + 1) * chunks_per_tile):
            acc += jnp.dot(get(sre, q).astype(BF16), cre_ref[q], preferred_element_type=F32)
            acc += jnp.dot(get(sim, q).astype(BF16), cimn_ref[q], preferred_element_type=F32)
        sl = slice(c * S5_CHUNK, (c + 1) * S5_CHUNK)
        y_scr[:, sl] = _gelu_tanh(acc + d_ref[:, sl] * u[:, sl])

    y = y_scr[...]
    gate = jax.nn.sigmoid(jnp.dot(y.astype(BF16), wglu_ref[...], preferred_element_type=F32)
                          + bglu_ref[...])
    mix_ref[...] = (y * gate).astype(mix_ref.dtype)


def _s5_prompt(u, pk, w_glu, b_glu, *, lt):
    b, t, w = u.shape
    n_chunks = pk["bb_re"].shape[0]
    st_shape = pk["a_re8"].shape
    full = lambda a: pl.BlockSpec(a.shape, lambda bb, tt, nd=a.ndim: (0,) * nd)
    consts = [pk["bb_re"], pk["bb_im"], pk["c_re"], pk["c_imn"], pk["a_re8"], pk["a_im8"], pk["d_row"],
              w_glu, b_glu]
    kern = functools.partial(_s5_prompt_kernel, lt=lt, n_chunks=n_chunks)
    st_spec = pl.BlockSpec((None,) + st_shape, lambda bb, tt: (bb, 0, 0, 0))
    return pl.pallas_call(
        kern, grid=(b, t // lt),
        in_specs=[pl.BlockSpec((None, lt, w), lambda bb, tt: (bb, tt, 0))] + [full(a) for a in consts],
        out_specs=[pl.BlockSpec((None, lt, w), lambda bb, tt: (bb, tt, 0)), st_spec, st_spec],
        out_shape=[jax.ShapeDtypeStruct((b, t, w), BF16),
                   jax.ShapeDtypeStruct((b,) + st_shape, F32),
                   jax.ShapeDtypeStruct((b,) + st_shape, F32)],
        scratch_shapes=[pltpu.VMEM((st_shape[0], lt * SUBLANE, LANE), F32),
                        pltpu.VMEM((st_shape[0], lt * SUBLANE, LANE), F32),
                        pltpu.VMEM(st_shape, F32), pltpu.VMEM(st_shape, F32),
                        pltpu.VMEM((lt, w), F32)],
        compiler_params=_params(("arbitrary", "arbitrary")), name="s5_prompt")(u, *consts)


def _s5_sample_kernel(u_ref, h0re_ref, h0im_ref, bbre_ref, bbim_ref, cre_ref, cimn_ref, are_ref, aim_ref,
                      d_ref, wglu_ref, bglu_ref, mix_ref, hre_ref, him_ref, *, n_chunks):
    u = u_ref[...]
    ub = u.astype(BF16)
    chunks_per_tile = S5_CHUNK // (u.shape[1] // n_chunks)
    ys = []
    for c in range(n_chunks // chunks_per_tile):
        acc = jnp.zeros((u.shape[0], S5_CHUNK), F32)
        for q in range(c * chunks_per_tile, (c + 1) * chunks_per_tile):
            ut = ub[:, (q // 2) * LANE:(q // 2 + 1) * LANE]
            sl = slice(q * S5_CHUNK, (q + 1) * S5_CHUNK)
            a_re, a_im = are_ref[:, sl], aim_ref[:, sl]
            h_re, h_im = h0re_ref[:, sl], h0im_ref[:, sl]
            n_re = a_re * h_re - a_im * h_im + jnp.dot(ut, bbre_ref[q], preferred_element_type=F32)
            n_im = a_re * h_im + a_im * h_re + jnp.dot(ut, bbim_ref[q], preferred_element_type=F32)
            hre_ref[:, sl] = n_re
            him_ref[:, sl] = n_im
            acc += jnp.dot(n_re.astype(BF16), cre_ref[q], preferred_element_type=F32)
            acc += jnp.dot(n_im.astype(BF16), cimn_ref[q], preferred_element_type=F32)
        cs = slice(c * S5_CHUNK, (c + 1) * S5_CHUNK)
        ys.append(_gelu_tanh(acc + d_ref[:, cs] * u[:, cs]))
    y = jnp.concatenate(ys, axis=-1)
    gate = jax.nn.sigmoid(jnp.dot(y.astype(BF16), wglu_ref[...], preferred_element_type=F32)
                          + bglu_ref[...])
    mix_ref[...] = (y * gate).astype(mix_ref.dtype)


def _s5_sample(u, h0_re, h0_im, pk, w_glu, b_glu):
    b, w = u.shape
    n_state = h0_re.shape[1]
    args = [u, h0_re, h0_im, pk["bb_re"], pk["bb_im"], pk["c_re"], pk["c_imn"], pk["a_re1"], pk["a_im1"],
            pk["d_row"], w_glu, b_glu]
    kern = functools.partial(_s5_sample_kernel, n_chunks=pk["bb_re"].shape[0])
    return pl.pallas_call(
        kern,
        out_shape=[jax.ShapeDtypeStruct((b, w), BF16), jax.ShapeDtypeStruct((b, n_state), F32),
                   jax.ShapeDtypeStruct((b, n_state), F32)],
        compiler_params=pltpu.CompilerParams(vmem_limit_bytes=VMEM_LIMIT), name="s5_sample")(*args)


def _state_tiles(a):
    return a.reshape(SUBLANE, -1, LANE).transpose(1, 0, 2)


def _state_untile(h, g, n):
    return h.transpose(0, 2, 1, 3).reshape(h.shape[0], g, n)


def _s5_pack(a_re, a_im, log_dt, b_re, b_im, c_re, c_im, d):
    g, n = a_re.shape
    p = d.shape[1]
    dt = jnp.exp(log_dt)[:, None]
    mag = jnp.exp(a_re * dt)
    ab_re = mag * jnp.cos(a_im * dt)
    ab_im = mag * jnp.sin(a_im * dt)
    num_re, num_im = ab_re - 1.0, ab_im
    den = a_re * a_re + a_im * a_im
    z_re = (num_re * a_re + num_im * a_im) / den
    z_im = (num_im * a_re - num_re * a_im) / den
    bb_re = z_re[..., None] * b_re - z_im[..., None] * b_im
    bb_im = z_re[..., None] * b_im + z_im[..., None] * b_re
    gpc = S5_CHUNK // n
    n_chunks = g // gpc
    eye = jnp.eye(gpc, dtype=F32)
    chunks_per_in_tile = LANE // (gpc * p)
    chunks_per_out_tile = S5_CHUNK // (gpc * p)

    def pack_in(bb):
        blk = jnp.einsum("qgpn,gh->qgphn", bb.transpose(0, 2, 1).reshape(n_chunks, gpc, p, n), eye)
        blk = blk.reshape(n_chunks, gpc * p, S5_CHUNK)
        sel = jax.nn.one_hot(jnp.arange(n_chunks) % chunks_per_in_tile, chunks_per_in_tile, dtype=F32)
        return jnp.einsum("qrc,qs->qsrc", blk, sel).reshape(n_chunks, LANE, S5_CHUNK).astype(BF16)

    def pack_out(cc):
        blk = jnp.einsum("qgnp,gh->qgnhp", cc.transpose(0, 2, 1).reshape(n_chunks, gpc, n, p), eye)
        blk = blk.reshape(n_chunks, S5_CHUNK, gpc * p)
        sel = jax.nn.one_hot(jnp.arange(n_chunks) % chunks_per_out_tile, chunks_per_out_tile, dtype=F32)
        return jnp.einsum("qrc,qs->qrsc", blk, sel).reshape(n_chunks, S5_CHUNK, S5_CHUNK).astype(BF16)

    return {
        "bb_re": pack_in(bb_re), "bb_im": pack_in(bb_im),
        "c_re": pack_out(c_re), "c_imn": pack_out(-c_im),
        "a_re8": _state_tiles(ab_re), "a_im8": _state_tiles(ab_im),
        "a_re1": ab_re.reshape(1, g * n), "a_im1": ab_im.reshape(1, g * n),
        "d_row": d.reshape(1, g * p),
    }


TM = 512
TN = 512
TF = 512
TQ = 512
S5_LT = 128


def _gain_row(*pieces):
    return jnp.concatenate([jnp.tile(g, reps) for g, reps in pieces]).reshape(1, -1).astype(F32)


def _head_rows(x, n_heads):
    x = x.reshape(x.shape[0], n_heads, LANE)
    return jnp.pad(x, ((0, 0), (0, HEAD_ROWS - n_heads), (0, 0)))


def kernel(x_prompt, x_sample, cache_fox_k, cache_fox_v, cache_fox_logf, state_s5_re, state_s5_im, cache_mem_k, cache_mem_v, page_table, mem_prompt, norm1_g, w_out, mem_norm_g, w_mem_kv, xq_norm_g, xk_norm_g, norm2_g, w_ffn_gate, w_ffn_up, w_ffn_down, fox_w_in, fox_b_f, fox_q_norm_g, fox_k_norm_g, s5_w_in, s5_a_re, s5_a_im, s5_log_dt, s5_b_re, s5_b_im, s5_c_re, s5_c_im, s5_d, s5_w_glu, s5_b_glu):
    bp, tp, d = x_prompt.shape
    bs, ts, _ = x_sample.shape
    depth = norm1_g.shape[0]
    n_fox, n_pool, page, fox_h, fox_hd = cache_fox_k.shape
    fox_w = fox_h * fox_hd
    n_mem, xa_h, xa_hd = cache_mem_k.shape[2:]
    xa_w = xa_h * xa_hd
    s5_g, s5_n = s5_a_re.shape[1:]
    mp, ms = bp * tp, bs * ts
    assert ts == 1 and fox_hd == LANE and xa_hd == LANE

    xp = x_prompt.reshape(mp, d)
    xs = x_sample.reshape(ms, d)
    mem = mem_prompt.reshape(bp * n_mem, d)
    ck = cache_fox_k.reshape(n_fox * n_pool, page, fox_h, fox_hd)
    cv = cache_fox_v.reshape(n_fox * n_pool, page, fox_h, fox_hd)
    clf = cache_fox_logf.reshape(n_fox * n_pool, page, fox_h)

    outs = {k: [] for k in ("fk_p", "fv_p", "fl_p", "fk_s", "fv_s", "fl_s",
                            "sr_p", "si_p", "sr_s", "si_s", "mk_p", "mv_p")}
    fox_nt, xa_nt = fox_w // TN, xa_w // TN

    for i in range(depth):
        j = i // 2
        wo = w_out[i].astype(BF16)
        wo_a, wo_x = wo[:-xa_w], wo[-xa_w:]

        gain = _gain_row((xk_norm_g[i], xa_h), (jnp.ones((xa_hd,), F32), xa_h))
        segs = ((0, xa_nt, True, (0, 1)), (xa_nt, xa_nt, False, (2, 3)))
        mk32, mk16, mv32, mv16 = _proj(mem, mem_norm_g[i], w_mem_kv[i].astype(BF16), gain, segs,
                                       (F32, BF16, F32, BF16), tm=bp * n_mem, tn=TN)
        outs["mk_p"].append(mk32.reshape(bp, n_mem, xa_h, xa_hd))
        outs["mv_p"].append(mv32.reshape(bp, n_mem, xa_h, xa_hd))

        if i % 2 == 0:
            w_in = fox_w_in[j]
            w_main = jnp.concatenate([w_in[:, :3 * fox_w], w_in[:, 3 * fox_w + fox_h:]], axis=1).astype(BF16)
            wft = jnp.zeros((HEAD_ROWS, d), F32).at[:fox_h].set(w_in[:, 3 * fox_w:3 * fox_w + fox_h].T)
            wft = wft.astype(BF16)
            bf = jnp.zeros((HEAD_ROWS, 1), F32).at[:fox_h, 0].set(fox_b_f[j])
            gain = _gain_row((fox_q_norm_g[j], fox_h), (fox_k_norm_g[j], fox_h),
                             (jnp.ones((fox_hd,), F32), fox_h), (xq_norm_g[i], xa_h))
            segs = ((0, fox_nt, True, (0,)), (fox_nt, fox_nt, True, (1, 2)),
                    (2 * fox_nt, fox_nt, False, (3, 4)), (3 * fox_nt, xa_nt, True, (5,)))
            dts = (BF16, F32, BF16, F32, BF16, BF16)
            q16, k32, k16, v32, v16, xq16, lft, ct = _proj(
                xp, norm1_g[i], w_main, gain, segs, dts, tm=TM, tn=TN, wft=wft, bf=bf, seq_len=tp)
            outs["fk_p"].append(k32.reshape(mp // page, page, fox_h, fox_hd))
            outs["fv_p"].append(v32.reshape(mp // page, page, fox_h, fox_hd))
            outs["fl_p"].append(lft[:, :fox_h].transpose(0, 2, 1).reshape(mp // page, page, fox_h))
            c_row = ct[:, :fox_h, None, :]
            c_col = ct[:, :fox_h, :, None]
            mix_p = _flash(q16.reshape(bp, tp, fox_w), k16.reshape(bp, tp, fox_w),
                           v16.reshape(bp, tp, fox_w), c_col, c_row, tq=TQ).reshape(mp, fox_w)

            dts_s = (F32, F32, F32, F32)
            segs_s = ((0, fox_nt, True, (0,)), (fox_nt, fox_nt, True, (1,)),
                      (2 * fox_nt, fox_nt, False, (2,)), (3 * fox_nt, xa_nt, True, (3,)))
            qs, ks, vs, xqs, lft_s = _proj(xs, norm1_g[i], w_main, gain, segs_s, dts_s,
                                           tm=ms, tn=TN, wft=wft, bf=bf)
            outs["fk_s"].append(ks.reshape(bs, ts, fox_h, fox_hd))
            outs["fv_s"].append(vs.reshape(bs, ts, fox_h, fox_hd))
            outs["fl_s"].append(lft_s[0, :fox_h].T.reshape(bs, ts, fox_h))
            lf_col = lft_s[0].T.reshape(bs, HEAD_ROWS, 1)
            mix_s = _dec_fox(_head_rows(qs, fox_h), _head_rows(ks, fox_h), _head_rows(vs, fox_h),
                             lf_col, ck, cv, clf, page_table, j * n_pool)
            mix_s = mix_s[:, :fox_h].reshape(ms, fox_w)
        else:
            w_main = s5_w_in[j].astype(BF16)
            s5_nt = (w_main.shape[1] - xa_w) // TN
            gain = _gain_row((jnp.ones((LANE,), F32), s5_nt * TN // LANE), (xq_norm_g[i], xa_h))
            segs = ((0, s5_nt, False, (0,)), (s5_nt, xa_nt, True, (1,)))
            pk = _s5_pack(s5_a_re[j], s5_a_im[j], s5_log_dt[j], s5_b_re[j], s5_b_im[j],
                          s5_c_re[j], s5_c_im[j], s5_d[j])
            w_glu = s5_w_glu[j].astype(BF16)
            b_glu = s5_b_glu[j].reshape(1, -1)
            u_p, xq16 = _proj(xp, norm1_g[i], w_main, gain, segs, (F32, BF16), tm=TM, tn=TN)
            s5_w = u_p.shape[1]
            mix_p, hr, hi = _s5_prompt(u_p.reshape(bp, tp, s5_w), pk, w_glu, b_glu, lt=S5_LT)
            mix_p = mix_p.reshape(mp, s5_w)
            outs["sr_p"].append(_state_untile(hr, s5_g, s5_n))
            outs["si_p"].append(_state_untile(hi, s5_g, s5_n))
            u_s, xqs = _proj(xs, norm1_g[i], w_main, gain, segs, (F32, F32), tm=ms, tn=TN)
            mix_s, hr, hi = _s5_sample(u_s, state_s5_re[j].reshape(bs, s5_g * s5_n),
                                       state_s5_im[j].reshape(bs, s5_g * s5_n), pk, w_glu, b_glu)
            outs["sr_s"].append(hr.reshape(bs, s5_g, s5_n))
            outs["si_s"].append(hi.reshape(bs, s5_g, s5_n))

        xa_p = _xattn(xq16.reshape(bp, tp, xa_w), mk16.reshape(bp, n_mem, xa_w),
                      mv16.reshape(bp, n_mem, xa_w), tq=TQ).reshape(mp, xa_w)
        xa_s = _dec_xattn(_head_rows(xqs, xa_h), cache_mem_k, cache_mem_v, i)[:, :xa_h].reshape(ms, xa_w)

        xp = _outproj(mix_p, xa_p, wo_a, wo_x, xp, tm=TM, tn=d // 2)
        xs = _outproj(mix_s, xa_s, wo_a, wo_x, xs, tm=ms, tn=d // 2)
        wg, wu, wd = (w_ffn_gate[i].astype(BF16), w_ffn_up[i].astype(BF16), w_ffn_down[i].astype(BF16))
        xp = _ffn(xp, norm2_g[i], wg, wu, wd, tm=TM, tf=TF)
        xs = _ffn(xs, norm2_g[i], wg, wu, wd, tm=ms, tf=TF)

    st = lambda k: jnp.stack(outs[k])
    return (xp.reshape(bp, tp, d), xs.reshape(bs, ts, d),
            st("fk_p"), st("fv_p"), st("fl_p"), st("fk_s"), st("fv_s"), st("fl_s"),
            st("sr_p"), st("si_p"), st("sr_s"), st("si_s"), st("mk_p"), st("mv_p"))
```

```python
import functools
import math

import jax
import jax.numpy as jnp
from jax import lax
from jax.experimental import pallas as pl
from jax.experimental.pallas import tpu as pltpu

F32 = jnp.float32
BF16 = jnp.bfloat16

EPS = 1e-6
NEG_INF = -1e30
LANE = 128
SUBLANE = 8
HEAD_ROWS = 16
VMEM_LIMIT = 52 * 1024 * 1024
NT_DIMS = (((1,), (1,)), ((), ()))


def _params(sem):
    return pltpu.CompilerParams(dimension_semantics=sem, vmem_limit_bytes=VMEM_LIMIT)


def _split3(x):
    hi = x.astype(BF16)
    r1 = x - hi.astype(F32)
    mid = r1.astype(BF16)
    lo = (r1 - mid.astype(F32)).astype(BF16)
    return hi, mid, lo


def _dot_exact01(x, w01):
    r = jnp.dot(jnp.concatenate(_split3(x), axis=0), w01, preferred_element_type=F32)
    return r[0:HEAD_ROWS] + r[HEAD_ROWS:2 * HEAD_ROWS] + r[2 * HEAD_ROWS:3 * HEAD_ROWS]


def _log_sigmoid(x):
    return jnp.minimum(x, 0.0) - jnp.log1p(jnp.exp(-jnp.abs(x)))


def _head_rmsnorm(y, gain_row):
    parts = []
    for c in range(y.shape[-1] // LANE):
        p = y[:, c * LANE:(c + 1) * LANE]
        parts.append(p * lax.rsqrt(jnp.mean(p * p, axis=-1, keepdims=True) + EPS))
    return jnp.concatenate(parts, axis=-1) * gain_row


def _proj_kernel(*refs, segs, n_out, with_f, cumsum, seq_tiles):
    x_ref, g_ref, w_ref, gain_ref = refs[:4]
    pos = 4
    if with_f:
        wft_ref, bf_ref = refs[4:6]
        pos = 6
    out_refs = refs[pos:pos + n_out]
    pos += n_out
    if with_f:
        lft_ref = refs[pos]
        pos += 1
        if cumsum:
            ct_ref = refs[pos]
            pos += 1
    h_scr = refs[pos]
    pos += 1
    if with_f and cumsum:
        tri_scr, carry_scr = refs[pos:pos + 2]

    i = pl.program_id(0)
    j = pl.program_id(1)
    tm = x_ref.shape[0]

    if with_f and cumsum:
        @pl.when((i == 0) & (j == 0))
        def _():
            u = lax.broadcasted_iota(jnp.int32, (tm, tm), 0)
            t = lax.broadcasted_iota(jnp.int32, (tm, tm), 1)
            tri_scr[...] = jnp.where(u <= t, 1.0, 0.0).astype(BF16)

    @pl.when(j == 0)
    def _():
        x = x_ref[...]
        h = x * lax.rsqrt(jnp.mean(x * x, axis=-1, keepdims=True) + EPS) * g_ref[...]
        hb = h.astype(BF16)
        h_scr[...] = hb
        if with_f:
            f = lax.dot_general(wft_ref[...], hb, NT_DIMS, preferred_element_type=F32)
            lf = _log_sigmoid(f + bf_ref[...])
            lft_ref[...] = lf
            if cumsum:
                @pl.when(i % seq_tiles == 0)
                def _():
                    carry_scr[...] = jnp.zeros_like(carry_scr)
                c = _dot_exact01(lf, tri_scr[...]) + carry_scr[...]
                ct_ref[...] = c
                carry_scr[...] = c[:, tm - 1:tm]

    y = jnp.dot(h_scr[...], w_ref[...], preferred_element_type=F32)
    for start, n_tiles, norm, outs in segs:
        @pl.when((j >= start) & (j < start + n_tiles))
        def _(norm=norm, outs=outs):
            yy = _head_rmsnorm(y, gain_ref[...]) if norm else y
            for o in outs:
                out_refs[o][...] = yy.astype(out_refs[o].dtype)


def _proj(x, g, w, gain_row, segs, out_dtypes, *, tm, tn, wft=None, bf=None, seq_len=None):
    m, k = x.shape
    n = w.shape[1]
    with_f = wft is not None
    cumsum = seq_len is not None
    grid = (m // tm, n // tn)
    seg_of_out = {}
    for start, n_tiles, _, outs in segs:
        for o in outs:
            seg_of_out[o] = (start, n_tiles)
    in_specs = [
        pl.BlockSpec((tm, k), lambda i, j: (i, 0)),
        pl.BlockSpec((1, k), lambda i, j: (0, 0)),
        pl.BlockSpec((k, tn), lambda i, j: (0, j)),
        pl.BlockSpec((1, tn), lambda i, j: (0, j)),
    ]
    args = [x, g.reshape(1, k), w, gain_row]
    if with_f:
        in_specs += [pl.BlockSpec((HEAD_ROWS, k), lambda i, j: (0, 0)),
                     pl.BlockSpec((HEAD_ROWS, 1), lambda i, j: (0, 0))]
        args += [wft, bf]
    out_shapes, out_specs = [], []
    for o, dt in enumerate(out_dtypes):
        start, n_tiles = seg_of_out[o]
        out_shapes.append(jax.ShapeDtypeStruct((m, n_tiles * tn), dt))
        out_specs.append(pl.BlockSpec(
            (tm, tn), lambda i, j, s=start, nt=n_tiles: (i, jnp.clip(j - s, 0, nt - 1))))
    scratch = [pltpu.VMEM((tm, k), BF16)]
    seq_tiles = 1
    if with_f:
        t_len = seq_len if cumsum else m
        seq_tiles = t_len // tm
        nb = m // t_len
        n_f = 2 if cumsum else 1
        for _ in range(n_f):
            out_shapes.append(jax.ShapeDtypeStruct((nb, HEAD_ROWS, t_len), F32))
            out_specs.append(pl.BlockSpec(
                (None, HEAD_ROWS, tm), lambda i, j, st=seq_tiles: (i // st, 0, i % st)))
        if cumsum:
            scratch += [pltpu.VMEM((tm, tm), BF16), pltpu.VMEM((HEAD_ROWS, 1), F32)]
    kern = functools.partial(_proj_kernel, segs=segs, n_out=len(out_dtypes), with_f=with_f,
                             cumsum=cumsum, seq_tiles=seq_tiles)
    return pl.pallas_call(
        kern, grid=grid, in_specs=in_specs, out_specs=out_specs, out_shape=out_shapes,
        scratch_shapes=scratch, compiler_params=_params(("arbitrary", "arbitrary")),
        name="norm_proj")(*args)


def _flash_kernel(q_ref, k_ref, v_ref, cq_ref, ck_ref, o_ref, m_scr, l_scr, acc_scr, *, scale):
    i = pl.program_id(2)
    j = pl.program_id(3)

    @pl.when(j == 0)
    def _():
        m_scr[...] = jnp.full_like(m_scr, NEG_INF)
        l_scr[...] = jnp.zeros_like(l_scr)
        acc_scr[...] = jnp.zeros_like(acc_scr)

    def step(masked):
        s = lax.dot_general(q_ref[...], k_ref[...], NT_DIMS, preferred_element_type=F32) * scale
        s = s + cq_ref[...] - ck_ref[...]
        if masked:
            row = lax.broadcasted_iota(jnp.int32, s.shape, 0)
            col = lax.broadcasted_iota(jnp.int32, s.shape, 1)
            s = jnp.where(col <= row, s, NEG_INF)
        m_old = m_scr[...]
        m_new = jnp.maximum(m_old, jnp.max(s, axis=-1, keepdims=True))
        alpha = jnp.exp(m_old - m_new)
        p = jnp.exp(s - m_new)
        l_scr[...] = alpha * l_scr[...] + jnp.sum(p, axis=-1, keepdims=True)
        acc_scr[...] = alpha * acc_scr[...] + jnp.dot(p.astype(BF16), v_ref[...],
                                                      preferred_element_type=F32)
        m_scr[...] = m_new

    @pl.when(j < i)
    def _():
        step(False)

    @pl.when(j == i)
    def _():
        step(True)
        o_ref[...] = (acc_scr[...] / l_scr[...]).astype(o_ref.dtype)


def _flash(q, k, v, c_col, c_row, *, tq):
    b, t, w = q.shape
    h = w // LANE
    nq = t // tq
    kern = functools.partial(_flash_kernel, scale=LANE ** -0.5)
    kv_map = lambda bb, hh, i, j: (bb, jnp.minimum(j, i), hh)
    return pl.pallas_call(
        kern, grid=(b, h, nq, nq),
        in_specs=[
            pl.BlockSpec((None, tq, LANE), lambda bb, hh, i, j: (bb, i, hh)),
            pl.BlockSpec((None, tq, LANE), kv_map),
            pl.BlockSpec((None, tq, LANE), kv_map),
            pl.BlockSpec((None, None, tq, 1), lambda bb, hh, i, j: (bb, hh, i, 0)),
            pl.BlockSpec((None, None, 1, tq), lambda bb, hh, i, j: (bb, hh, 0, jnp.minimum(j, i))),
        ],
        out_specs=pl.BlockSpec((None, tq, LANE), lambda bb, hh, i, j: (bb, i, hh)),
        out_shape=jax.ShapeDtypeStruct((b, t, w), BF16),
        scratch_shapes=[pltpu.VMEM((tq, 1), F32), pltpu.VMEM((tq, 1), F32), pltpu.VMEM((tq, LANE), F32)],
        compiler_params=_params(("arbitrary",) * 4), name="fox_prompt_attn")(q, k, v, c_col, c_row)


def _xattn_kernel(q_ref, k_ref, v_ref, o_ref, *, scale):
    for h in range(q_ref.shape[-1] // LANE):
        sl = slice(h * LANE, (h + 1) * LANE)
        s = lax.dot_general(q_ref[:, sl], k_ref[:, sl], NT_DIMS, preferred_element_type=F32) * scale
        p = jnp.exp(s - jnp.max(s, axis=-1, keepdims=True))
        p = p / jnp.sum(p, axis=-1, keepdims=True)
        o_ref[:, sl] = jnp.dot(p.astype(BF16), v_ref[:, sl],
                               preferred_element_type=F32).astype(o_ref.dtype)


def _xattn(q, k, v, *, tq):
    b, t, w = q.shape
    n_mem = k.shape[1]
    kern = functools.partial(_xattn_kernel, scale=LANE ** -0.5)
    return pl.pallas_call(
        kern, grid=(b, t // tq),
        in_specs=[pl.BlockSpec((None, tq, w), lambda bb, i: (bb, i, 0)),
                  pl.BlockSpec((None, n_mem, w), lambda bb, i: (bb, 0, 0)),
                  pl.BlockSpec((None, n_mem, w), lambda bb, i: (bb, 0, 0))],
        out_specs=pl.BlockSpec((None, tq, w), lambda bb, i: (bb, i, 0)),
        out_shape=jax.ShapeDtypeStruct((b, t, w), BF16),
        compiler_params=_params(("arbitrary", "arbitrary")), name="mem_xattn_prompt")(q, k, v)


def _head_diag(width):
    rows = lax.broadcasted_iota(jnp.int32, (HEAD_ROWS, width), 0)
    lanes = lax.broadcasted_iota(jnp.int32, (HEAD_ROWS, width), 1)
    return (lanes // LANE) == rows


def _dec_fox_kernel(pt_ref, q_ref, kn_ref, vn_ref, lfn_ref, *refs, scale, n_pages, pps):
    kv_refs, lf_refs = refs[:2 * pps], refs[2 * pps:3 * pps]
    o_ref, qbd_scr, m_scr, l_scr, acc_scr, carry_scr = refs[3 * pps:]
    b = pl.program_id(0)
    step = pl.program_id(1)
    n_heads, keys, _ = kv_refs[0].shape
    width = n_heads * LANE
    pair = 2 * LANE
    diag = _head_diag(width)

    @pl.when(step == 0)
    def _():
        qbd = jnp.where(diag, jnp.broadcast_to(q_ref[...], (HEAD_ROWS, width)), 0.0).astype(BF16)
        qbd_scr[...] = qbd
        kn = kn_ref[...].astype(BF16).astype(F32)
        m_scr[...] = jnp.sum(qbd.astype(F32) * kn, axis=-1, keepdims=True) * scale
        l_scr[...] = jnp.ones_like(l_scr)
        acc_scr[...] = jnp.broadcast_to(vn_ref[...].astype(BF16).astype(F32), (HEAD_ROWS, width))
        carry_scr[...] = lfn_ref[...]

    u = lax.broadcasted_iota(jnp.int32, (keys, keys), 0)
    kk = lax.broadcasted_iota(jnp.int32, (keys, keys), 1)
    suffix = jnp.where(u >= kk, 1.0, 0.0).astype(BF16)
    sel_h = lax.broadcasted_iota(jnp.int32, (HEAD_ROWS, n_heads * SUBLANE), 0)
    sel_c = lax.broadcasted_iota(jnp.int32, (HEAD_ROWS, n_heads * SUBLANE), 1)

    scores, incls, lfts = [], [], []
    for i in range(pps):
        k_ref, lf_ref = kv_refs[2 * i], lf_refs[i]
        s = jnp.zeros((HEAD_ROWS, keys), F32)
        for c in range(n_heads // 2):
            kp = jnp.concatenate([k_ref[2 * c], k_ref[2 * c + 1]], axis=-1).astype(BF16)
            s += lax.dot_general(qbd_scr[:, c * pair:(c + 1) * pair], kp, NT_DIMS,
                                 preferred_element_type=F32)
        scores.append(s * scale)
        r = pt_ref[b, n_pages - 1 - (step * pps + i)] % SUBLANE
        sel = jnp.where((sel_c // SUBLANE == sel_h) & (sel_c % SUBLANE == r), 1.0, 0.0).astype(BF16)
        lft = sum(jnp.dot(sel, part, preferred_element_type=F32)
                  for part in _split3(lf_ref[...].reshape(n_heads * SUBLANE, keys)))
        lfts.append(lft)
        incls.append(_dot_exact01(lft, suffix))

    carry = carry_scr[...]
    logits = []
    for i in range(pps):
        logits.append(scores[i] + (carry + incls[i] - lfts[i]))
        carry = carry + incls[i][:, 0:1]
    carry_scr[...] = carry
    logits = jnp.concatenate(logits, axis=-1)

    m_old = m_scr[...]
    m_new = jnp.maximum(m_old, jnp.max(logits, axis=-1, keepdims=True))
    alpha = jnp.exp(m_old - m_new)
    pw = jnp.exp(logits - m_new)
    l_scr[...] = alpha * l_scr[...] + jnp.sum(pw, axis=-1, keepdims=True)
    m_scr[...] = m_new
    pb = pw.astype(BF16)
    for c in range(n_heads // 2):
        vp = jnp.concatenate(
            [jnp.concatenate([kv_refs[2 * i + 1][2 * c], kv_refs[2 * i + 1][2 * c + 1]], axis=-1)
             for i in range(pps)], axis=0).astype(BF16)
        sl = slice(c * pair, (c + 1) * pair)
        acc_scr[:, sl] = alpha * acc_scr[:, sl] + jnp.dot(pb, vp, preferred_element_type=F32)

    @pl.when(step == n_pages // pps - 1)
    def _():
        o_ref[...] = jnp.sum(jnp.where(diag, acc_scr[...] / l_scr[...], 0.0), axis=0, keepdims=True)


DEC_PAGES_PER_STEP = 4


def _dec_fox(q, k_new, v_new, lf_new_col, cache_k, cache_v, cache_lf, page_table, layer):
    b, _, w = q.shape
    n_pages = page_table.shape[1]
    n_heads, page = cache_k.shape[1:3]
    n_pool = cache_lf.shape[1]
    pps = DEC_PAGES_PER_STEP
    assert n_pages % pps == 0 and n_heads % 2 == 0 and n_pool % SUBLANE == 0

    def page_id(bb, p, pt, i):
        return pt[bb, n_pages - 1 - (p * pps + i)]

    row_map = lambda bb, p, pt: (bb, 0, 0)
    row_spec = pl.BlockSpec((None, 1, w), row_map)
    kv_specs, lf_specs = [], []
    for i in range(pps):
        kv_map = lambda bb, p, pt, i=i: (page_id(bb, p, pt, i) + layer * n_pool, 0, 0, 0)
        kv_specs += [pl.BlockSpec((None, n_heads, page, LANE), kv_map)] * 2
        lf_specs.append(pl.BlockSpec((n_heads, SUBLANE, page),
                                     lambda bb, p, pt, i=i: (layer, page_id(bb, p, pt, i) // SUBLANE, 0)))
    kern = functools.partial(_dec_fox_kernel, scale=LANE ** -0.5, n_pages=n_pages, pps=pps)
    grid_spec = pltpu.PrefetchScalarGridSpec(
        num_scalar_prefetch=1, grid=(b, n_pages // pps),
        in_specs=[row_spec, row_spec, row_spec, pl.BlockSpec((None, HEAD_ROWS, 1), row_map)]
        + kv_specs + lf_specs,
        out_specs=row_spec,
        scratch_shapes=[pltpu.VMEM((HEAD_ROWS, w), BF16), pltpu.VMEM((HEAD_ROWS, 1), F32),
                        pltpu.VMEM((HEAD_ROWS, 1), F32), pltpu.VMEM((HEAD_ROWS, w), F32),
                        pltpu.VMEM((HEAD_ROWS, 1), F32)])
    return pl.pallas_call(
        kern, grid_spec=grid_spec, out_shape=jax.ShapeDtypeStruct((b, 1, w), F32),
        compiler_params=_params(("arbitrary", "arbitrary")), name="fox_decode_attn")(
            page_table, q, k_new, v_new, lf_new_col, *([cache_k, cache_v] * pps), *([cache_lf] * pps))


def _dec_xattn_kernel(q_ref, k_ref, v_ref, o_ref, *, scale, n_heads):
    s = lax.dot_general(q_ref[...].astype(BF16), k_ref[...].astype(BF16), NT_DIMS,
                        preferred_element_type=F32) * scale
    row = lax.broadcasted_iota(jnp.int32, s.shape, 0)
    col = lax.broadcasted_iota(jnp.int32, s.shape, 1)
    logits = jnp.where(col % n_heads == row, s, NEG_INF)
    p = jnp.exp(logits - jnp.max(logits, axis=-1, keepdims=True))
    p = p / jnp.sum(p, axis=-1, keepdims=True)
    o_ref[...] = jnp.dot(p.astype(BF16), v_ref[...].astype(BF16), preferred_element_type=F32)


def _dec_xattn(q, mem_k, mem_v, layer, n_heads):
    b = q.shape[0]
    n_rows = mem_k.shape[2]
    kern = functools.partial(_dec_xattn_kernel, scale=LANE ** -0.5, n_heads=n_heads)
    head_spec = pl.BlockSpec((None, HEAD_ROWS, LANE), lambda bb: (bb, 0, 0))
    mem_spec = pl.BlockSpec((None, None, n_rows, LANE), lambda bb: (layer, bb, 0, 0))
    return pl.pallas_call(
        kern, grid=(b,), in_specs=[head_spec, mem_spec, mem_spec], out_specs=head_spec,
        out_shape=jax.ShapeDtypeStruct((b, HEAD_ROWS, LANE), F32),
        compiler_params=_params(("arbitrary",)), name="mem_xattn_decode")(q, mem_k, mem_v)


def _outproj_kernel(a_ref, xa_ref, wa_ref, wx_ref, x_ref, o_ref):
    o_ref[...] = (x_ref[...]
                  + jnp.dot(a_ref[...].astype(BF16), wa_ref[...], preferred_element_type=F32)
                  + jnp.dot(xa_ref[...].astype(BF16), wx_ref[...], preferred_element_type=F32))


def _outproj(a, xa, w_a, w_x, x, *, tm, tn):
    m, d = x.shape
    ka, kx = a.shape[1], xa.shape[1]
    return pl.pallas_call(
        _outproj_kernel, grid=(m // tm, d // tn),
        in_specs=[pl.BlockSpec((tm, ka), lambda i, j: (i, 0)),
                  pl.BlockSpec((tm, kx), lambda i, j: (i, 0)),
                  pl.BlockSpec((ka, tn), lambda i, j: (0, j)),
                  pl.BlockSpec((kx, tn), lambda i, j: (0, j)),
                  pl.BlockSpec((tm, tn), lambda i, j: (i, j))],
        out_specs=pl.BlockSpec((tm, tn), lambda i, j: (i, j)),
        out_shape=jax.ShapeDtypeStruct((m, d), F32),
        compiler_params=_params(("arbitrary", "arbitrary")), name="out_proj")(a, xa, w_a, w_x, x)


def _ffn_kernel(x_ref, g_ref, wg_ref, wu_ref, wd_ref, o_ref, h_scr, acc_scr):
    f = pl.program_id(1)

    @pl.when(f == 0)
    def _():
        x = x_ref[...]
        h = x * lax.rsqrt(jnp.mean(x * x, axis=-1, keepdims=True) + EPS) * g_ref[...]
        h_scr[...] = h.astype(BF16)
        acc_scr[...] = jnp.zeros_like(acc_scr)

    h = h_scr[...]
    gate = jnp.dot(h, wg_ref[...], preferred_element_type=F32)
    up = jnp.dot(h, wu_ref[...], preferred_element_type=F32)
    act = (gate * jax.nn.sigmoid(gate)) * up
    acc_scr[...] += jnp.dot(act.astype(BF16), wd_ref[...], preferred_element_type=F32)

    @pl.when(f == pl.num_programs(1) - 1)
    def _():
        o_ref[...] = x_ref[...] + acc_scr[...]


def _ffn(x, g, w_gate, w_up, w_down, *, tm, tf):
    m, d = x.shape
    d_ff = w_gate.shape[1]
    return pl.pallas_call(
        _ffn_kernel, grid=(m // tm, d_ff // tf),
        in_specs=[pl.BlockSpec((tm, d), lambda i, f: (i, 0)),
                  pl.BlockSpec((1, d), lambda i, f: (0, 0)),
                  pl.BlockSpec((d, tf), lambda i, f: (0, f)),
                  pl.BlockSpec((d, tf), lambda i, f: (0, f)),
                  pl.BlockSpec((tf, d), lambda i, f: (f, 0))],
        out_specs=pl.BlockSpec((tm, d), lambda i, f: (i, 0)),
        out_shape=jax.ShapeDtypeStruct((m, d), F32),
        scratch_shapes=[pltpu.VMEM((tm, d), BF16), pltpu.VMEM((tm, d), F32)],
        compiler_params=_params(("arbitrary", "arbitrary")), name="swiglu_ffn")(
            x, g.reshape(1, d), w_gate, w_up, w_down)


S5_CHUNK = 2 * LANE


def _gelu_tanh(y):
    return 0.5 * y * (1.0 + jnp.tanh(math.sqrt(2.0 / math.pi) * (y + 0.044715 * (y * y * y))))


def _s5_prompt_kernel(u_ref, bbre_ref, bbim_ref, cre_ref, cimn_ref, are_ref, aim_ref, d_ref,
                      wglu_ref, bglu_ref, mix_ref, hre_ref, him_ref,
                      sre, sim, hre_s, him_s, y_scr, *, lt, n_chunks):
    t = pl.program_id(1)
    tiles_per_chunk = S5_CHUNK // LANE
    chunks_per_row = are_ref.shape[0] // tiles_per_chunk
    chunks_per_tile = S5_CHUNK // (u_ref.shape[1] // n_chunks)

    @pl.when(t == 0)
    def _():
        hre_s[...] = jnp.zeros_like(hre_s)
        him_s[...] = jnp.zeros_like(him_s)

    u = u_ref[...]
    ub = u.astype(BF16)

    def put(dst, q, val):
        j, lc = divmod(q, chunks_per_row)
        for k in range(tiles_per_chunk):
            dst[lc * tiles_per_chunk + k, pl.ds(j, lt, stride=SUBLANE), :] = val[:, k * LANE:(k + 1) * LANE]

    def get(src, q):
        j, lc = divmod(q, chunks_per_row)
        return jnp.concatenate([src[lc * tiles_per_chunk + k, pl.ds(j, lt, stride=SUBLANE), :]
                                for k in range(tiles_per_chunk)], axis=-1)

    for q in range(n_chunks):
        ut = ub[:, (q // 2) * LANE:(q // 2 + 1) * LANE]
        put(sre, q, jnp.dot(ut, bbre_ref[q], preferred_element_type=F32))
        put(sim, q, jnp.dot(ut, bbim_ref[q], preferred_element_type=F32))

    a_re = are_ref[...]
    a_im = aim_ref[...]

    def body(tt, carry):
        h_re, h_im = carry
        r0 = pl.multiple_of(tt * SUBLANE, SUBLANE)
        n_re = a_re * h_re - a_im * h_im + sre[:, pl.ds(r0, SUBLANE), :]
        n_im = a_re * h_im + a_im * h_re + sim[:, pl.ds(r0, SUBLANE), :]
        sre[:, pl.ds(r0, SUBLANE), :] = n_re
        sim[:, pl.ds(r0, SUBLANE), :] = n_im
        return n_re, n_im

    h_re, h_im = lax.fori_loop(0, lt, body, (hre_s[...], him_s[...]), unroll=8)
    hre_s[...] = h_re
    him_s[...] = h_im
    hre_ref[...] = h_re
    him_ref[...] = h_im

    for c in range(n_chunks // chunks_per_tile):
        acc = jnp.zeros((lt, S5_CHUNK), F32)
        for q in range(c * chunks_per_tile, (c + 1) * chunks_per_tile):
            acc += jnp.dot(get(sre, q).astype(BF16), cre_ref[q], preferred_element_type=F32)
            acc += jnp.dot(get(sim, q).astype(BF16), cimn_ref[q], preferred_element_type=F32)
        sl = slice(c * S5_CHUNK, (c + 1) * S5_CHUNK)
        y_scr[:, sl] = _gelu_tanh(acc + d_ref[:, sl] * u[:, sl])

    y = y_scr[...]
    gate = jax.nn.sigmoid(jnp.dot(y.astype(BF16), wglu_ref[...], preferred_element_type=F32)
                          + bglu_ref[...])
    mix_ref[...] = (y * gate).astype(mix_ref.dtype)


def _s5_prompt(u, pk, w_glu, b_glu, *, lt):
    b, t, w = u.shape
    n_chunks = pk["bb_re"].shape[0]
    st_shape = pk["a_re8"].shape
    full = lambda a: pl.BlockSpec(a.shape, lambda bb, tt, nd=a.ndim: (0,) * nd)
    consts = [pk["bb_re"], pk["bb_im"], pk["c_re"], pk["c_imn"], pk["a_re8"], pk["a_im8"], pk["d_row"],
              w_glu, b_glu]
    kern = functools.partial(_s5_prompt_kernel, lt=lt, n_chunks=n_chunks)
    st_spec = pl.BlockSpec((None,) + st_shape, lambda bb, tt: (bb, 0, 0, 0))
    return pl.pallas_call(
        kern, grid=(b, t // lt),
        in_specs=[pl.BlockSpec((None, lt, w), lambda bb, tt: (bb, tt, 0))] + [full(a) for a in consts],
        out_specs=[pl.BlockSpec((None, lt, w), lambda bb, tt: (bb, tt, 0)), st_spec, st_spec],
        out_shape=[jax.ShapeDtypeStruct((b, t, w), BF16),
                   jax.ShapeDtypeStruct((b,) + st_shape, F32),
                   jax.ShapeDtypeStruct((b,) + st_shape, F32)],
        scratch_shapes=[pltpu.VMEM((st_shape[0], lt * SUBLANE, LANE), F32),
                        pltpu.VMEM((st_shape[0], lt * SUBLANE, LANE), F32),
                        pltpu.VMEM(st_shape, F32), pltpu.VMEM(st_shape, F32),
                        pltpu.VMEM((lt, w), F32)],
        compiler_params=_params(("arbitrary", "arbitrary")), name="s5_prompt")(u, *consts)


def _s5_sample_kernel(u_ref, h0re_ref, h0im_ref, bbre_ref, bbim_ref, cre_ref, cimn_ref, are_ref, aim_ref,
                      d_ref, wglu_ref, bglu_ref, mix_ref, hre_ref, him_ref, *, n_chunks):
    u = u_ref[...]
    ub = u.astype(BF16)
    chunks_per_tile = S5_CHUNK // (u.shape[1] // n_chunks)
    ys = []
    for c in range(n_chunks // chunks_per_tile):
        acc = jnp.zeros((u.shape[0], S5_CHUNK), F32)
        for q in range(c * chunks_per_tile, (c + 1) * chunks_per_tile):
            ut = ub[:, (q // 2) * LANE:(q // 2 + 1) * LANE]
            sl = slice(q * S5_CHUNK, (q + 1) * S5_CHUNK)
            a_re, a_im = are_ref[:, sl], aim_ref[:, sl]
            h_re, h_im = h0re_ref[:, sl], h0im_ref[:, sl]
            n_re = a_re * h_re - a_im * h_im + jnp.dot(ut, bbre_ref[q], preferred_element_type=F32)
            n_im = a_re * h_im + a_im * h_re + jnp.dot(ut, bbim_ref[q], preferred_element_type=F32)
            hre_ref[:, sl] = n_re
            him_ref[:, sl] = n_im
            acc += jnp.dot(n_re.astype(BF16), cre_ref[q], preferred_element_type=F32)
            acc += jnp.dot(n_im.astype(BF16), cimn_ref[q], preferred_element_type=F32)
        cs = slice(c * S5_CHUNK, (c + 1) * S5_CHUNK)
        ys.append(_gelu_tanh(acc + d_ref[:, cs] * u[:, cs]))
    y = jnp.concatenate(ys, axis=-1)
    gate = jax.nn.sigmoid(jnp.dot(y.astype(BF16), wglu_ref[...], preferred_element_type=F32)
                          + bglu_ref[...])
    mix_ref[...] = (y * gate).astype(mix_ref.dtype)


def _s5_sample(u, h0_re, h0_im, pk, w_glu, b_glu):
    b, w = u.shape
    n_state = h0_re.shape[1]
    args = [u, h0_re, h0_im, pk["bb_re"], pk["bb_im"], pk["c_re"], pk["c_imn"], pk["a_re1"], pk["a_im1"],
            pk["d_row"], w_glu, b_glu]
    kern = functools.partial(_s5_sample_kernel, n_chunks=pk["bb_re"].shape[0])
    return pl.pallas_call(
        kern,
        out_shape=[jax.ShapeDtypeStruct((b, w), BF16), jax.ShapeDtypeStruct((b, n_state), F32),
                   jax.ShapeDtypeStruct((b, n_state), F32)],
        compiler_params=pltpu.CompilerParams(vmem_limit_bytes=VMEM_LIMIT), name="s5_sample")(*args)


def _state_tiles(a):
    return a.reshape(SUBLANE, -1, LANE).transpose(1, 0, 2)


def _state_untile(h, g, n):
    return h.transpose(0, 2, 1, 3).reshape(h.shape[0], g, n)


def _s5_pack(a_re, a_im, log_dt, b_re, b_im, c_re, c_im, d):
    g, n = a_re.shape
    p = d.shape[1]
    dt = jnp.exp(log_dt)[:, None]
    mag = jnp.exp(a_re * dt)
    ab_re = mag * jnp.cos(a_im * dt)
    ab_im = mag * jnp.sin(a_im * dt)
    num_re, num_im = ab_re - 1.0, ab_im
    den = a_re * a_re + a_im * a_im
    z_re = (num_re * a_re + num_im * a_im) / den
    z_im = (num_im * a_re - num_re * a_im) / den
    bb_re = z_re[..., None] * b_re - z_im[..., None] * b_im
    bb_im = z_re[..., None] * b_im + z_im[..., None] * b_re
    gpc = S5_CHUNK // n
    n_chunks = g // gpc
    eye = jnp.eye(gpc, dtype=F32)
    chunks_per_in_tile = LANE // (gpc * p)
    chunks_per_out_tile = S5_CHUNK // (gpc * p)

    def pack_in(bb):
        blk = jnp.einsum("qgpn,gh->qgphn", bb.transpose(0, 2, 1).reshape(n_chunks, gpc, p, n), eye)
        blk = blk.reshape(n_chunks, gpc * p, S5_CHUNK)
        sel = jax.nn.one_hot(jnp.arange(n_chunks) % chunks_per_in_tile, chunks_per_in_tile, dtype=F32)
        return jnp.einsum("qrc,qs->qsrc", blk, sel).reshape(n_chunks, LANE, S5_CHUNK).astype(BF16)

    def pack_out(cc):
        blk = jnp.einsum("qgnp,gh->qgnhp", cc.transpose(0, 2, 1).reshape(n_chunks, gpc, n, p), eye)
        blk = blk.reshape(n_chunks, S5_CHUNK, gpc * p)
        sel = jax.nn.one_hot(jnp.arange(n_chunks) % chunks_per_out_tile, chunks_per_out_tile, dtype=F32)
        return jnp.einsum("qrc,qs->qrsc", blk, sel).reshape(n_chunks, S5_CHUNK, S5_CHUNK).astype(BF16)

    return {
        "bb_re": pack_in(bb_re), "bb_im": pack_in(bb_im),
        "c_re": pack_out(c_re), "c_imn": pack_out(-c_im),
        "a_re8": _state_tiles(ab_re), "a_im8": _state_tiles(ab_im),
        "a_re1": ab_re.reshape(1, g * n), "a_im1": ab_im.reshape(1, g * n),
        "d_row": d.reshape(1, g * p),
    }


TM = 512
TN = 512
TF = 512
TQ = 512
S5_LT = 128


def _gain_row(*pieces):
    return jnp.concatenate([jnp.tile(g, reps) for g, reps in pieces]).reshape(1, -1).astype(F32)


def _head_rows(x, n_heads):
    x = x.reshape(x.shape[0], n_heads, LANE)
    return jnp.pad(x, ((0, 0), (0, HEAD_ROWS - n_heads), (0, 0)))


def kernel(x_prompt, x_sample, cache_fox_k, cache_fox_v, cache_fox_logf, state_s5_re, state_s5_im, cache_mem_k, cache_mem_v, page_table, mem_prompt, norm1_g, w_out, mem_norm_g, w_mem_kv, xq_norm_g, xk_norm_g, norm2_g, w_ffn_gate, w_ffn_up, w_ffn_down, fox_w_in, fox_b_f, fox_q_norm_g, fox_k_norm_g, s5_w_in, s5_a_re, s5_a_im, s5_log_dt, s5_b_re, s5_b_im, s5_c_re, s5_c_im, s5_d, s5_w_glu, s5_b_glu):
    bp, tp, d = x_prompt.shape
    bs, ts, _ = x_sample.shape
    depth = norm1_g.shape[0]
    n_fox, n_pool, page, fox_h, fox_hd = cache_fox_k.shape
    fox_w = fox_h * fox_hd
    n_mem, xa_h, xa_hd = cache_mem_k.shape[2:]
    xa_w = xa_h * xa_hd
    s5_g, s5_n = s5_a_re.shape[1:]
    mp, ms = bp * tp, bs * ts
    assert ts == 1 and fox_hd == LANE and xa_hd == LANE

    xp = x_prompt.reshape(mp, d)
    xs = x_sample.reshape(ms, d)
    mem = mem_prompt.reshape(bp * n_mem, d)
    ck = cache_fox_k.transpose(0, 1, 3, 2, 4).reshape(n_fox * n_pool, fox_h, page, fox_hd)
    cv = cache_fox_v.transpose(0, 1, 3, 2, 4).reshape(n_fox * n_pool, fox_h, page, fox_hd)
    clf = cache_fox_logf.transpose(0, 3, 1, 2).reshape(n_fox * fox_h, n_pool, page)
    cmk = cache_mem_k.reshape(depth, bs, n_mem * xa_h, xa_hd)
    cmv = cache_mem_v.reshape(depth, bs, n_mem * xa_h, xa_hd)

    outs = {k: [] for k in ("fk_p", "fv_p", "fl_p", "fk_s", "fv_s", "fl_s",
                            "sr_p", "si_p", "sr_s", "si_s", "mk_p", "mv_p")}
    fox_nt, xa_nt = fox_w // TN, xa_w // TN

    for i in range(depth):
        j = i // 2
        wo = w_out[i].astype(BF16)
        wo_a, wo_x = wo[:-xa_w], wo[-xa_w:]

        gain = _gain_row((xk_norm_g[i], xa_h), (jnp.ones((xa_hd,), F32), xa_h))
        segs = ((0, xa_nt, True, (0, 1)), (xa_nt, xa_nt, False, (2, 3)))
        mk32, mk16, mv32, mv16 = _proj(mem, mem_norm_g[i], w_mem_kv[i].astype(BF16), gain, segs,
                                       (F32, BF16, F32, BF16), tm=bp * n_mem, tn=TN)
        outs["mk_p"].append(mk32.reshape(bp, n_mem, xa_h, xa_hd))
        outs["mv_p"].append(mv32.reshape(bp, n_mem, xa_h, xa_hd))

        if i % 2 == 0:
            w_in = fox_w_in[j]
            w_main = jnp.concatenate([w_in[:, :3 * fox_w], w_in[:, 3 * fox_w + fox_h:]], axis=1).astype(BF16)
            wft = jnp.zeros((HEAD_ROWS, d), F32).at[:fox_h].set(w_in[:, 3 * fox_w:3 * fox_w + fox_h].T)
            wft = wft.astype(BF16)
            bf = jnp.zeros((HEAD_ROWS, 1), F32).at[:fox_h, 0].set(fox_b_f[j])
            gain = _gain_row((fox_q_norm_g[j], fox_h), (fox_k_norm_g[j], fox_h),
                             (jnp.ones((fox_hd,), F32), fox_h), (xq_norm_g[i], xa_h))
            segs = ((0, fox_nt, True, (0,)), (fox_nt, fox_nt, True, (1, 2)),
                    (2 * fox_nt, fox_nt, False, (3, 4)), (3 * fox_nt, xa_nt, True, (5,)))
            dts = (BF16, F32, BF16, F32, BF16, BF16)
            q16, k32, k16, v32, v16, xq16, lft, ct = _proj(
                xp, norm1_g[i], w_main, gain, segs, dts, tm=TM, tn=TN, wft=wft, bf=bf, seq_len=tp)
            outs["fk_p"].append(k32.reshape(mp // page, page, fox_h, fox_hd))
            outs["fv_p"].append(v32.reshape(mp // page, page, fox_h, fox_hd))
            outs["fl_p"].append(lft[:, :fox_h].transpose(0, 2, 1).reshape(mp // page, page, fox_h))
            c_row = ct[:, :fox_h, None, :]
            c_col = ct[:, :fox_h, :, None]
            mix_p = _flash(q16.reshape(bp, tp, fox_w), k16.reshape(bp, tp, fox_w),
                           v16.reshape(bp, tp, fox_w), c_col, c_row, tq=TQ).reshape(mp, fox_w)

            dts_s = (F32, F32, F32, F32)
            segs_s = ((0, fox_nt, True, (0,)), (fox_nt, fox_nt, True, (1,)),
                      (2 * fox_nt, fox_nt, False, (2,)), (3 * fox_nt, xa_nt, True, (3,)))
            qs, ks, vs, xqs, lft_s = _proj(xs, norm1_g[i], w_main, gain, segs_s, dts_s,
                                           tm=ms, tn=TN, wft=wft, bf=bf)
            outs["fk_s"].append(ks.reshape(bs, ts, fox_h, fox_hd))
            outs["fv_s"].append(vs.reshape(bs, ts, fox_h, fox_hd))
            outs["fl_s"].append(lft_s[0, :fox_h].T.reshape(bs, ts, fox_h))
            lf_col = lft_s[0].T.reshape(bs, HEAD_ROWS, 1)
            mix_s = _dec_fox(qs.reshape(bs, 1, fox_w), ks.reshape(bs, 1, fox_w), vs.reshape(bs, 1, fox_w),
                             lf_col, ck, cv, clf, page_table, j).reshape(ms, fox_w)
        else:
            w_main = s5_w_in[j].astype(BF16)
            s5_nt = (w_main.shape[1] - xa_w) // TN
            gain = _gain_row((jnp.ones((LANE,), F32), s5_nt * TN // LANE), (xq_norm_g[i], xa_h))
            segs = ((0, s5_nt, False, (0,)), (s5_nt, xa_nt, True, (1,)))
            pk = _s5_pack(s5_a_re[j], s5_a_im[j], s5_log_dt[j], s5_b_re[j], s5_b_im[j],
                          s5_c_re[j], s5_c_im[j], s5_d[j])
            w_glu = s5_w_glu[j].astype(BF16)
            b_glu = s5_b_glu[j].reshape(1, -1)
            u_p, xq16 = _proj(xp, norm1_g[i], w_main, gain, segs, (F32, BF16), tm=TM, tn=TN)
            s5_w = u_p.shape[1]
            mix_p, hr, hi = _s5_prompt(u_p.reshape(bp, tp, s5_w), pk, w_glu, b_glu, lt=S5_LT)
            mix_p = mix_p.reshape(mp, s5_w)
            outs["sr_p"].append(_state_untile(hr, s5_g, s5_n))
            outs["si_p"].append(_state_untile(hi, s5_g, s5_n))
            u_s, xqs = _proj(xs, norm1_g[i], w_main, gain, segs, (F32, F32), tm=ms, tn=TN)
            mix_s, hr, hi = _s5_sample(u_s, state_s5_re[j].reshape(bs, s5_g * s5_n),
                                       state_s5_im[j].reshape(bs, s5_g * s5_n), pk, w_glu, b_glu)
            outs["sr_s"].append(hr.reshape(bs, s5_g, s5_n))
            outs["si_s"].append(hi.reshape(bs, s5_g, s5_n))

        xa_p = _xattn(xq16.reshape(bp, tp, xa_w), mk16.reshape(bp, n_mem, xa_w),
                      mv16.reshape(bp, n_mem, xa_w), tq=TQ).reshape(mp, xa_w)
        xa_s = _dec_xattn(_head_rows(xqs, xa_h), cmk, cmv, i, xa_h)[:, :xa_h].reshape(ms, xa_w)

        xp = _outproj(mix_p, xa_p, wo_a, wo_x, xp, tm=TM, tn=d // 2)
        xs = _outproj(mix_s, xa_s, wo_a, wo_x, xs, tm=ms, tn=d // 2)
        wg, wu, wd = (w_ffn_gate[i].astype(BF16), w_ffn_up[i].astype(BF16), w_ffn_down[i].astype(BF16))
        xp = _ffn(xp, norm2_g[i], wg, wu, wd, tm=TM, tf=TF)
        xs = _ffn(xs, norm2_g[i], wg, wu, wd, tm=ms, tf=TF)

    st = lambda k: jnp.stack(outs[k])
    return (xp.reshape(bp, tp, d), xs.reshape(bs, ts, d),
            st("fk_p"), st("fv_p"), st("fl_p"), st("fk_s"), st("fv_s"), st("fl_s"),
            st("sr_p"), st("si_p"), st("sr_s"), st("si_s"), st("mk_p"), st("mv_p"))
```

```python
import functools
import math

import jax
import jax.numpy as jnp
import numpy as np
from jax import lax
from jax.experimental import pallas as pl
from jax.experimental.pallas import tpu as pltpu

F32 = jnp.float32
BF16 = jnp.bfloat16

EPS = 1e-6
NEG_INF = -1e30
LANE = 128
SUBLANE = 8
HEAD_ROWS = 16
VMEM_LIMIT = 52 * 1024 * 1024
NT_DIMS = (((1,), (1,)), ((), ()))


def _params(sem):
    return pltpu.CompilerParams(dimension_semantics=sem, vmem_limit_bytes=VMEM_LIMIT)


def _split3(x):
    hi = x.astype(BF16)
    r1 = x - hi.astype(F32)
    mid = r1.astype(BF16)
    lo = (r1 - mid.astype(F32)).astype(BF16)
    return hi, mid, lo


def _dot_exact01(x, w01):
    r = jnp.dot(jnp.concatenate(_split3(x), axis=0), w01, preferred_element_type=F32)
    return r[0:HEAD_ROWS] + r[HEAD_ROWS:2 * HEAD_ROWS] + r[2 * HEAD_ROWS:3 * HEAD_ROWS]


def _log_sigmoid(x):
    return jnp.minimum(x, 0.0) - jnp.log1p(jnp.exp(-jnp.abs(x)))


def _head_rmsnorm(y, gain_row):
    parts = []
    for c in range(y.shape[-1] // LANE):
        p = y[:, c * LANE:(c + 1) * LANE]
        parts.append(p * lax.rsqrt(jnp.mean(p * p, axis=-1, keepdims=True) + EPS))
    return jnp.concatenate(parts, axis=-1) * gain_row


AUG_LANES = 8
SQRT2 = math.sqrt(2.0)
Q_PRESCALE = 0.125


def _lane_split3(x):
    return jnp.concatenate(_split3(x), axis=-1)


def _lane_sum3(r):
    return r[:, 0:LANE] + r[:, LANE:2 * LANE] + r[:, 2 * LANE:3 * LANE]


def _proj_kernel(*refs, segs, n_out, with_f, cumsum, seq_tiles):
    x_ref, g_ref, w_ref, gain_ref = refs[:4]
    pos = 4
    if with_f:
        wf_ref, bf_ref = refs[4:6]
        pos = 6
        if cumsum:
            wqa_ref, wka_ref, oneq_ref, onek_ref = refs[6:10]
            pos = 10
    out_refs = refs[pos:pos + n_out]
    pos += n_out
    if with_f:
        lf_ref = refs[pos]
        pos += 1
        if cumsum:
            qa_ref, ka_ref = refs[pos:pos + 2]
            pos += 2
    h_scr = refs[pos]
    pos += 1
    if with_f and cumsum:
        tri_scr, carry_scr = refs[pos:pos + 2]

    i = pl.program_id(0)
    j = pl.program_id(1)
    tm = x_ref.shape[0]

    if with_f and cumsum:
        @pl.when((i == 0) & (j == 0))
        def _():
            t = lax.broadcasted_iota(jnp.int32, (tm, tm), 0)
            u = lax.broadcasted_iota(jnp.int32, (tm, tm), 1)
            tri_scr[...] = jnp.where(u <= t, 1.0, 0.0).astype(BF16)

    @pl.when(j == 0)
    def _():
        x = x_ref[...]
        h = x * lax.rsqrt(jnp.mean(x * x, axis=-1, keepdims=True) + EPS) * g_ref[...]
        hb = h.astype(BF16)
        h_scr[...] = hb
        if with_f:
            lf = _log_sigmoid(jnp.dot(hb, wf_ref[...], preferred_element_type=F32) + bf_ref[...])
            lf_ref[...] = lf
            if cumsum:
                @pl.when(i % seq_tiles == 0)
                def _():
                    carry_scr[...] = jnp.zeros_like(carry_scr)
                c = _lane_sum3(jnp.dot(tri_scr[...], _lane_split3(lf), preferred_element_type=F32))
                c = c + carry_scr[...]
                carry_scr[...] = c[tm - 1:tm, :]
                pieces = _lane_split3(c * SQRT2)
                qa_ref[...] = (jnp.dot(pieces, wqa_ref[...], preferred_element_type=F32)
                               + oneq_ref[...]).astype(qa_ref.dtype)
                ka_ref[...] = (jnp.dot(pieces, wka_ref[...], preferred_element_type=F32)
                               + onek_ref[...]).astype(ka_ref.dtype)

    y = jnp.dot(h_scr[...], w_ref[...], preferred_element_type=F32)
    for start, n_tiles, norm, outs in segs:
        @pl.when((j >= start) & (j < start + n_tiles))
        def _(norm=norm, outs=outs):
            yy = _head_rmsnorm(y, gain_ref[...]) if norm else y
            for o in outs:
                out_refs[o][...] = yy.astype(out_refs[o].dtype)


def _aug_maps(n_heads):
    wqa = np.zeros((3 * LANE, LANE), np.float32)
    wka = np.zeros((3 * LANE, LANE), np.float32)
    oneq = np.zeros((1, LANE), np.float32)
    onek = np.zeros((1, LANE), np.float32)
    for h in range(n_heads):
        for piece in range(3):
            wqa[piece * LANE + h, AUG_LANES * h + piece] = 1.0
            wka[piece * LANE + h, AUG_LANES * h + 3 + piece] = -1.0
            oneq[0, AUG_LANES * h + 3 + piece] = 1.0
            onek[0, AUG_LANES * h + piece] = 1.0
    return jnp.asarray(wqa, BF16), jnp.asarray(wka, BF16), jnp.asarray(oneq), jnp.asarray(onek)


def _proj(x, g, w, gain_row, segs, out_dtypes, *, tm, tn, layer=None, wf=None, bf=None, seq_len=None,
          n_heads=None):
    m, k = x.shape
    n = w.shape[-1]
    with_f = wf is not None
    cumsum = seq_len is not None
    grid = (m // tm, n // tn)
    seg_of_out = {}
    for start, n_tiles, _, outs in segs:
        for o in outs:
            seg_of_out[o] = (start, n_tiles)
    const = lambda shape: pl.BlockSpec(shape, lambda i, j: (0,) * len(shape))
    w_spec = (pl.BlockSpec((k, tn), lambda i, j: (0, j)) if layer is None
              else pl.BlockSpec((None, k, tn), lambda i, j: (layer, 0, j)))
    in_specs = [pl.BlockSpec((tm, k), lambda i, j: (i, 0)), const((1, k)), w_spec,
                pl.BlockSpec((1, tn), lambda i, j: (0, j))]
    args = [x, g.reshape(1, k), w, gain_row]
    if with_f:
        in_specs += [const((k, LANE)), const((1, LANE))]
        args += [wf, bf]
        if cumsum:
            maps = _aug_maps(n_heads)
            in_specs += [const(a.shape) for a in maps]
            args += list(maps)
    out_shapes, out_specs = [], []
    for o, dt in enumerate(out_dtypes):
        start, n_tiles = seg_of_out[o]
        out_shapes.append(jax.ShapeDtypeStruct((m, n_tiles * tn), dt))
        out_specs.append(pl.BlockSpec(
            (tm, tn), lambda i, j, s=start, nt=n_tiles: (i, jnp.clip(j - s, 0, nt - 1))))
    scratch = [pltpu.VMEM((tm, k), BF16)]
    seq_tiles = 1
    if with_f:
        row_spec = pl.BlockSpec((tm, LANE), lambda i, j: (i, 0))
        out_shapes.append(jax.ShapeDtypeStruct((m, LANE), F32))
        out_specs.append(row_spec)
        if cumsum:
            seq_tiles = seq_len // tm
            out_shapes += [jax.ShapeDtypeStruct((m, LANE), BF16)] * 2
            out_specs += [row_spec] * 2
            scratch += [pltpu.VMEM((tm, tm), BF16), pltpu.VMEM((1, LANE), F32)]
    kern = functools.partial(_proj_kernel, segs=segs, n_out=len(out_dtypes), with_f=with_f,
                             cumsum=cumsum, seq_tiles=seq_tiles)
    return pl.pallas_call(
        kern, grid=grid, in_specs=in_specs, out_specs=out_specs, out_shape=out_shapes,
        scratch_shapes=scratch, compiler_params=_params(("arbitrary", "arbitrary")),
        name="norm_proj")(*args)


LOG2_SCALE = math.log2(math.e) / SQRT2


def _flash_kernel(q_ref, qa_ref, k_ref, ka_ref, v_ref, o_ref, qaug_scr, m_scr, l_scr, acc_scr):
    h = pl.program_id(1)
    i = pl.program_id(2)
    j = pl.program_id(3)
    tq, tk = q_ref.shape[0], k_ref.shape[0]

    @pl.when(j == 0)
    def _():
        lane = lax.broadcasted_iota(jnp.int32, (tq, LANE), 1)
        qaug_scr[:, 0:LANE] = q_ref[...]
        qaug_scr[:, LANE:2 * LANE] = jnp.where(lane // AUG_LANES == h, qa_ref[...], jnp.zeros_like(qa_ref))
        m_scr[...] = jnp.full_like(m_scr, NEG_INF)
        l_scr[...] = jnp.zeros_like(l_scr)
        acc_scr[...] = jnp.zeros_like(acc_scr)

    def step(masked):
        k_aug = jnp.concatenate([k_ref[...], ka_ref[...]], axis=-1)
        x = lax.dot_general(qaug_scr[...], k_aug, NT_DIMS, preferred_element_type=F32) * LOG2_SCALE
        if masked:
            row = lax.broadcasted_iota(jnp.int32, x.shape, 0)
            col = lax.broadcasted_iota(jnp.int32, x.shape, 1)
            x = jnp.where(col <= row, x, NEG_INF)
        m_old = m_scr[...]
        m_new = jnp.maximum(m_old, jnp.max(x, axis=-1, keepdims=True))
        alpha = jnp.exp2(m_old - m_new)
        l_part = alpha * l_scr[...]
        ps = []
        for c in range(tk // LANE):
            p = jnp.exp2(x[:, c * LANE:(c + 1) * LANE] - m_new)
            l_part = l_part + p
            ps.append(p.astype(BF16))
        l_scr[...] = l_part
        m_scr[...] = m_new
        acc_scr[...] = alpha * acc_scr[...] + jnp.dot(jnp.concatenate(ps, axis=-1), v_ref[...],
                                                      preferred_element_type=F32)

    @pl.when(j < i)
    def _():
        step(False)

    @pl.when(j == i)
    def _():
        step(True)
        o_ref[...] = (acc_scr[...] / jnp.sum(l_scr[...], axis=-1, keepdims=True)).astype(o_ref.dtype)


def _flash(q, qa, k, ka, v, *, tq):
    b, t, w = q.shape
    h = w // LANE
    nq = t // tq
    q_map = lambda bb, hh, i, j: (bb, i, hh)
    kv_map = lambda bb, hh, i, j: (bb, jnp.minimum(j, i), hh)
    return pl.pallas_call(
        _flash_kernel, grid=(b, h, nq, nq),
        in_specs=[
            pl.BlockSpec((None, tq, LANE), q_map),
            pl.BlockSpec((None, tq, LANE), lambda bb, hh, i, j: (bb, i, 0)),
            pl.BlockSpec((None, tq, LANE), kv_map),
            pl.BlockSpec((None, tq, LANE), lambda bb, hh, i, j: (bb, jnp.minimum(j, i), 0)),
            pl.BlockSpec((None, tq, LANE), kv_map),
        ],
        out_specs=pl.BlockSpec((None, tq, LANE), q_map),
        out_shape=jax.ShapeDtypeStruct((b, t, w), BF16),
        scratch_shapes=[pltpu.VMEM((tq, 2 * LANE), BF16), pltpu.VMEM((tq, LANE), F32),
                        pltpu.VMEM((tq, LANE), F32), pltpu.VMEM((tq, LANE), F32)],
        compiler_params=_params(("arbitrary",) * 4), name="fox_prompt_attn")(q, qa, k, ka, v)


def _xattn_kernel(q_ref, k_ref, v_ref, o_ref, *, scale):
    for h in range(q_ref.shape[-1] // LANE):
        sl = slice(h * LANE, (h + 1) * LANE)
        s = lax.dot_general(q_ref[:, sl], k_ref[:, sl], NT_DIMS, preferred_element_type=F32) * scale
        p = jnp.exp(s - jnp.max(s, axis=-1, keepdims=True))
        p = p / jnp.sum(p, axis=-1, keepdims=True)
        o_ref[:, sl] = jnp.dot(p.astype(BF16), v_ref[:, sl],
                               preferred_element_type=F32).astype(o_ref.dtype)


def _xattn(q, k, v, *, tq):
    b, t, w = q.shape
    n_mem = k.shape[1]
    kern = functools.partial(_xattn_kernel, scale=LANE ** -0.5)
    return pl.pallas_call(
        kern, grid=(b, t // tq),
        in_specs=[pl.BlockSpec((None, tq, w), lambda bb, i: (bb, i, 0)),
                  pl.BlockSpec((None, n_mem, w), lambda bb, i: (bb, 0, 0)),
                  pl.BlockSpec((None, n_mem, w), lambda bb, i: (bb, 0, 0))],
        out_specs=pl.BlockSpec((None, tq, w), lambda bb, i: (bb, i, 0)),
        out_shape=jax.ShapeDtypeStruct((b, t, w), BF16),
        compiler_params=_params(("arbitrary", "arbitrary")), name="mem_xattn_prompt")(q, k, v)


def _head_diag(width):
    rows = lax.broadcasted_iota(jnp.int32, (HEAD_ROWS, width), 0)
    lanes = lax.broadcasted_iota(jnp.int32, (HEAD_ROWS, width), 1)
    return (lanes // LANE) == rows


def _dec_fox_kernel(pt_ref, q_ref, kn_ref, vn_ref, lfn_ref, *refs, scale, n_pages, pps):
    kv_refs, lf_refs = refs[:2 * pps], refs[2 * pps:3 * pps]
    o_ref, qbd_scr, m_scr, l_scr, acc_scr, carry_scr = refs[3 * pps:]
    b = pl.program_id(0)
    step = pl.program_id(1)
    n_heads, keys, _ = kv_refs[0].shape
    width = n_heads * LANE
    pair = 2 * LANE
    diag = _head_diag(width)

    @pl.when(step == 0)
    def _():
        qbd = jnp.where(diag, jnp.broadcast_to(q_ref[...], (HEAD_ROWS, width)), 0.0).astype(BF16)
        qbd_scr[...] = qbd
        kn = kn_ref[...].astype(BF16).astype(F32)
        m_scr[...] = jnp.sum(qbd.astype(F32) * kn, axis=-1, keepdims=True) * scale
        l_scr[...] = jnp.ones_like(l_scr)
        acc_scr[...] = jnp.broadcast_to(vn_ref[...].astype(BF16).astype(F32), (HEAD_ROWS, width))
        carry_scr[...] = lfn_ref[...]

    u = lax.broadcasted_iota(jnp.int32, (keys, keys), 0)
    kk = lax.broadcasted_iota(jnp.int32, (keys, keys), 1)
    suffix = jnp.where(u >= kk, 1.0, 0.0).astype(BF16)
    sel_h = lax.broadcasted_iota(jnp.int32, (HEAD_ROWS, n_heads * SUBLANE), 0)
    sel_c = lax.broadcasted_iota(jnp.int32, (HEAD_ROWS, n_heads * SUBLANE), 1)

    scores, incls, lfts = [], [], []
    for i in range(pps):
        k_ref, lf_ref = kv_refs[2 * i], lf_refs[i]
        s = jnp.zeros((HEAD_ROWS, keys), F32)
        for c in range(n_heads // 2):
            kp = jnp.concatenate([k_ref[2 * c], k_ref[2 * c + 1]], axis=-1).astype(BF16)
            s += lax.dot_general(qbd_scr[:, c * pair:(c + 1) * pair], kp, NT_DIMS,
                                 preferred_element_type=F32)
        scores.append(s * scale)
        r = pt_ref[b, n_pages - 1 - (step * pps + i)] % SUBLANE
        sel = jnp.where((sel_c // SUBLANE == sel_h) & (sel_c % SUBLANE == r), 1.0, 0.0).astype(BF16)
        lft = sum(jnp.dot(sel, part, preferred_element_type=F32)
                  for part in _split3(lf_ref[...].reshape(n_heads * SUBLANE, keys)))
        lfts.append(lft)
        incls.append(_dot_exact01(lft, suffix))

    carry = carry_scr[...]
    logits = []
    for i in range(pps):
        logits.append(scores[i] + (carry + incls[i] - lfts[i]))
        carry = carry + incls[i][:, 0:1]
    carry_scr[...] = carry
    logits = jnp.concatenate(logits, axis=-1)

    m_old = m_scr[...]
    m_new = jnp.maximum(m_old, jnp.max(logits, axis=-1, keepdims=True))
    alpha = jnp.exp(m_old - m_new)
    pw = jnp.exp(logits - m_new)
    l_scr[...] = alpha * l_scr[...] + jnp.sum(pw, axis=-1, keepdims=True)
    m_scr[...] = m_new
    pb = pw.astype(BF16)
    for c in range(n_heads // 2):
        vp = jnp.concatenate(
            [jnp.concatenate([kv_refs[2 * i + 1][2 * c], kv_refs[2 * i + 1][2 * c + 1]], axis=-1)
             for i in range(pps)], axis=0).astype(BF16)
        sl = slice(c * pair, (c + 1) * pair)
        acc_scr[:, sl] = alpha * acc_scr[:, sl] + jnp.dot(pb, vp, preferred_element_type=F32)

    @pl.when(step == n_pages // pps - 1)
    def _():
        o_ref[...] = jnp.sum(jnp.where(diag, acc_scr[...] / l_scr[...], 0.0), axis=0, keepdims=True)


DEC_PAGES_PER_STEP = 8


def _dec_fox(q, k_new, v_new, lf_new_col, cache_k, cache_v, cache_lf, page_table, layer):
    b, _, w = q.shape
    n_pages = page_table.shape[1]
    n_heads, page = cache_k.shape[1:3]
    n_pool = cache_lf.shape[1]
    pps = DEC_PAGES_PER_STEP
    assert n_pages % pps == 0 and n_heads % 2 == 0 and n_pool % SUBLANE == 0

    def page_id(bb, p, pt, i):
        return pt[bb, n_pages - 1 - (p * pps + i)]

    row_map = lambda bb, p, pt: (bb, 0, 0)
    row_spec = pl.BlockSpec((None, 1, w), row_map)
    kv_specs, lf_specs = [], []
    for i in range(pps):
        kv_map = lambda bb, p, pt, i=i: (page_id(bb, p, pt, i) + layer * n_pool, 0, 0, 0)
        kv_specs += [pl.BlockSpec((None, n_heads, page, LANE), kv_map)] * 2
        lf_specs.append(pl.BlockSpec((n_heads, SUBLANE, page),
                                     lambda bb, p, pt, i=i: (layer, page_id(bb, p, pt, i) // SUBLANE, 0)))
    kern = functools.partial(_dec_fox_kernel, scale=LANE ** -0.5 / Q_PRESCALE, n_pages=n_pages, pps=pps)
    grid_spec = pltpu.PrefetchScalarGridSpec(
        num_scalar_prefetch=1, grid=(b, n_pages // pps),
        in_specs=[row_spec, row_spec, row_spec, pl.BlockSpec((None, HEAD_ROWS, 1), row_map)]
        + kv_specs + lf_specs,
        out_specs=row_spec,
        scratch_shapes=[pltpu.VMEM((HEAD_ROWS, w), BF16), pltpu.VMEM((HEAD_ROWS, 1), F32),
                        pltpu.VMEM((HEAD_ROWS, 1), F32), pltpu.VMEM((HEAD_ROWS, w), F32),
                        pltpu.VMEM((HEAD_ROWS, 1), F32)])
    return pl.pallas_call(
        kern, grid_spec=grid_spec, out_shape=jax.ShapeDtypeStruct((b, 1, w), F32),
        compiler_params=_params(("arbitrary", "arbitrary")), name="fox_decode_attn")(
            page_table, q, k_new, v_new, lf_new_col, *([cache_k, cache_v] * pps), *([cache_lf] * pps))


def _dec_xattn_kernel(q_ref, k_ref, v_ref, o_ref, *, scale, n_heads):
    s = lax.dot_general(q_ref[...].astype(BF16), k_ref[...].astype(BF16), NT_DIMS,
                        preferred_element_type=F32) * scale
    row = lax.broadcasted_iota(jnp.int32, s.shape, 0)
    col = lax.broadcasted_iota(jnp.int32, s.shape, 1)
    logits = jnp.where(col % n_heads == row, s, NEG_INF)
    p = jnp.exp(logits - jnp.max(logits, axis=-1, keepdims=True))
    p = p / jnp.sum(p, axis=-1, keepdims=True)
    o_ref[...] = jnp.dot(p.astype(BF16), v_ref[...].astype(BF16), preferred_element_type=F32)


def _dec_xattn(q, mem_k, mem_v, layer, n_heads):
    b = q.shape[0]
    n_rows = mem_k.shape[2]
    kern = functools.partial(_dec_xattn_kernel, scale=LANE ** -0.5, n_heads=n_heads)
    head_spec = pl.BlockSpec((None, HEAD_ROWS, LANE), lambda bb: (bb, 0, 0))
    mem_spec = pl.BlockSpec((None, None, n_rows, LANE), lambda bb: (layer, bb, 0, 0))
    return pl.pallas_call(
        kern, grid=(b,), in_specs=[head_spec, mem_spec, mem_spec], out_specs=head_spec,
        out_shape=jax.ShapeDtypeStruct((b, HEAD_ROWS, LANE), F32),
        compiler_params=_params(("arbitrary",)), name="mem_xattn_decode")(q, mem_k, mem_v)


def _outproj_kernel(a_ref, xa_ref, wa_ref, wx_ref, x_ref, o_ref):
    o_ref[...] = (x_ref[...]
                  + jnp.dot(a_ref[...].astype(BF16), wa_ref[...], preferred_element_type=F32)
                  + jnp.dot(xa_ref[...].astype(BF16), wx_ref[...], preferred_element_type=F32))


def _outproj(a, xa, w, layer, x, *, tm, tn):
    m, d = x.shape
    ka, kx = a.shape[1], xa.shape[1]
    assert ka % kx == 0
    return pl.pallas_call(
        _outproj_kernel, grid=(m // tm, d // tn),
        in_specs=[pl.BlockSpec((tm, ka), lambda i, j: (i, 0)),
                  pl.BlockSpec((tm, kx), lambda i, j: (i, 0)),
                  pl.BlockSpec((None, ka, tn), lambda i, j: (layer, 0, j)),
                  pl.BlockSpec((None, kx, tn), lambda i, j: (layer, ka // kx, j)),
                  pl.BlockSpec((tm, tn), lambda i, j: (i, j))],
        out_specs=pl.BlockSpec((tm, tn), lambda i, j: (i, j)),
        out_shape=jax.ShapeDtypeStruct((m, d), F32),
        compiler_params=_params(("arbitrary", "arbitrary")), name="out_proj")(a, xa, w, w, x)


def _ffn_kernel(x_ref, g_ref, wg_ref, wu_ref, wd_ref, o_ref, h_scr, acc_scr):
    f = pl.program_id(1)

    @pl.when(f == 0)
    def _():
        x = x_ref[...]
        h = x * lax.rsqrt(jnp.mean(x * x, axis=-1, keepdims=True) + EPS) * g_ref[...]
        h_scr[...] = h.astype(BF16)
        acc_scr[...] = jnp.zeros_like(acc_scr)

    h = h_scr[...]
    gate = jnp.dot(h, wg_ref[...], preferred_element_type=F32)
    up = jnp.dot(h, wu_ref[...], preferred_element_type=F32)
    act = (gate * jax.nn.sigmoid(gate)) * up
    acc_scr[...] += jnp.dot(act.astype(BF16), wd_ref[...], preferred_element_type=F32)

    @pl.when(f == pl.num_programs(1) - 1)
    def _():
        o_ref[...] = x_ref[...] + acc_scr[...]


def _ffn(x, g, w_gate, w_up, w_down, layer, *, tm, tf):
    m, d = x.shape
    d_ff = w_gate.shape[-1]
    return pl.pallas_call(
        _ffn_kernel, grid=(m // tm, d_ff // tf),
        in_specs=[pl.BlockSpec((tm, d), lambda i, f: (i, 0)),
                  pl.BlockSpec((1, d), lambda i, f: (0, 0)),
                  pl.BlockSpec((None, d, tf), lambda i, f: (layer, 0, f)),
                  pl.BlockSpec((None, d, tf), lambda i, f: (layer, 0, f)),
                  pl.BlockSpec((None, tf, d), lambda i, f: (layer, f, 0))],
        out_specs=pl.BlockSpec((tm, d), lambda i, f: (i, 0)),
        out_shape=jax.ShapeDtypeStruct((m, d), F32),
        scratch_shapes=[pltpu.VMEM((tm, d), BF16), pltpu.VMEM((tm, d), F32)],
        compiler_params=_params(("arbitrary", "arbitrary")), name="swiglu_ffn")(
            x, g.reshape(1, d), w_gate, w_up, w_down)


S5_CHUNK = 2 * LANE


def _gelu_tanh(y):
    return 0.5 * y * (1.0 + jnp.tanh(math.sqrt(2.0 / math.pi) * (y + 0.044715 * (y * y * y))))


def _s5_prompt_kernel(u_ref, bbre_ref, bbim_ref, cre_ref, cimn_ref, are_ref, aim_ref, d_ref,
                      wglu_ref, bglu_ref, mix_ref, hre_ref, him_ref,
                      sre, sim, hre_s, him_s, y_scr, *, lt, n_chunks):
    t = pl.program_id(1)
    tiles_per_chunk = S5_CHUNK // LANE
    chunks_per_row = are_ref.shape[0] // tiles_per_chunk
    chunks_per_tile = S5_CHUNK // (u_ref.shape[1] // n_chunks)

    @pl.when(t == 0)
    def _():
        hre_s[...] = jnp.zeros_like(hre_s)
        him_s[...] = jnp.zeros_like(him_s)

    u = u_ref[...]
    ub = u.astype(BF16)

    def put(dst, q, val):
        j, lc = divmod(q, chunks_per_row)
        for k in range(tiles_per_chunk):
            dst[lc * tiles_per_chunk + k, pl.ds(j, lt, stride=SUBLANE), :] = val[:, k * LANE:(k + 1) * LANE]

    def get(src, q):
        j, lc = divmod(q, chunks_per_row)
        return jnp.concatenate([src[lc * tiles_per_chunk + k, pl.ds(j, lt, stride=SUBLANE), :]
                                for k in range(tiles_per_chunk)], axis=-1)

    for q in range(n_chunks):
        ut = ub[:, (q // 2) * LANE:(q // 2 + 1) * LANE]
        put(sre, q, jnp.dot(ut, bbre_ref[q], preferred_element_type=F32))
        put(sim, q, jnp.dot(ut, bbim_ref[q], preferred_element_type=F32))

    a_re = are_ref[...]
    a_im = aim_ref[...]

    def body(tt, carry):
        h_re, h_im = carry
        r0 = pl.multiple_of(tt * SUBLANE, SUBLANE)
        n_re = a_re * h_re - a_im * h_im + sre[:, pl.ds(r0, SUBLANE), :]
        n_im = a_re * h_im + a_im * h_re + sim[:, pl.ds(r0, SUBLANE), :]
        sre[:, pl.ds(r0, SUBLANE), :] = n_re
        sim[:, pl.ds(r0, SUBLANE), :] = n_im
        return n_re, n_im

    h_re, h_im = lax.fori_loop(0, lt, body, (hre_s[...], him_s[...]), unroll=8)
    hre_s[...] = h_re
    him_s[...] = h_im
    hre_ref[...] = h_re
    him_ref[...] = h_im

    for c in range(n_chunks // chunks_per_tile):
        acc = jnp.zeros((lt, S5_CHUNK), F32)
        for q in range(c * chunks_per_tile, (c + 1) * chunks_per_tile):
            acc += jnp.dot(get(sre, q).astype(BF16), cre_ref[q], preferred_element_type=F32)
            acc += jnp.dot(get(sim, q).astype(BF16), cimn_ref[q], preferred_element_type=F32)
        sl = slice(c * S5_CHUNK, (c + 1) * S5_CHUNK)
        y_scr[:, sl] = _gelu_tanh(acc + d_ref[:, sl] * u[:, sl])

    y = y_scr[...]
    gate = jax.nn.sigmoid(jnp.dot(y.astype(BF16), wglu_ref[...], preferred_element_type=F32)
                          + bglu_ref[...])
    mix_ref[...] = (y * gate).astype(mix_ref.dtype)


def _s5_prompt(u, pk, w_glu, b_glu, *, lt):
    b, t, w = u.shape
    n_chunks = pk["bb_re"].shape[0]
    st_shape = pk["a_re8"].shape
    full = lambda a: pl.BlockSpec(a.shape, lambda bb, tt, nd=a.ndim: (0,) * nd)
    consts = [pk["bb_re"], pk["bb_im"], pk["c_re"], pk["c_imn"], pk["a_re8"], pk["a_im8"], pk["d_row"],
              w_glu, b_glu]
    kern = functools.partial(_s5_prompt_kernel, lt=lt, n_chunks=n_chunks)
    st_spec = pl.BlockSpec((None,) + st_shape, lambda bb, tt: (bb, 0, 0, 0))
    return pl.pallas_call(
        kern, grid=(b, t // lt),
        in_specs=[pl.BlockSpec((None, lt, w), lambda bb, tt: (bb, tt, 0))] + [full(a) for a in consts],
        out_specs=[pl.BlockSpec((None, lt, w), lambda bb, tt: (bb, tt, 0)), st_spec, st_spec],
        out_shape=[jax.ShapeDtypeStruct((b, t, w), BF16),
                   jax.ShapeDtypeStruct((b,) + st_shape, F32),
                   jax.ShapeDtypeStruct((b,) + st_shape, F32)],
        scratch_shapes=[pltpu.VMEM((st_shape[0], lt * SUBLANE, LANE), F32),
                        pltpu.VMEM((st_shape[0], lt * SUBLANE, LANE), F32),
                        pltpu.VMEM(st_shape, F32), pltpu.VMEM(st_shape, F32),
                        pltpu.VMEM((lt, w), F32)],
        compiler_params=_params(("arbitrary", "arbitrary")), name="s5_prompt")(u, *consts)


def _s5_sample_kernel(u_ref, h0re_ref, h0im_ref, bbre_ref, bbim_ref, cre_ref, cimn_ref, are_ref, aim_ref,
                      d_ref, wglu_ref, bglu_ref, mix_ref, hre_ref, him_ref, *, n_chunks):
    u = u_ref[...]
    ub = u.astype(BF16)
    chunks_per_tile = S5_CHUNK // (u.shape[1] // n_chunks)
    ys = []
    for c in range(n_chunks // chunks_per_tile):
        acc = jnp.zeros((u.shape[0], S5_CHUNK), F32)
        for q in range(c * chunks_per_tile, (c + 1) * chunks_per_tile):
            ut = ub[:, (q // 2) * LANE:(q // 2 + 1) * LANE]
            sl = slice(q * S5_CHUNK, (q + 1) * S5_CHUNK)
            a_re, a_im = are_ref[:, sl], aim_ref[:, sl]
            h_re, h_im = h0re_ref[:, sl], h0im_ref[:, sl]
            n_re = a_re * h_re - a_im * h_im + jnp.dot(ut, bbre_ref[q], preferred_element_type=F32)
            n_im = a_re * h_im + a_im * h_re + jnp.dot(ut, bbim_ref[q], preferred_element_type=F32)
            hre_ref[:, sl] = n_re
            him_ref[:, sl] = n_im
            acc += jnp.dot(n_re.astype(BF16), cre_ref[q], preferred_element_type=F32)
            acc += jnp.dot(n_im.astype(BF16), cimn_ref[q], preferred_element_type=F32)
        cs = slice(c * S5_CHUNK, (c + 1) * S5_CHUNK)
        ys.append(_gelu_tanh(acc + d_ref[:, cs] * u[:, cs]))
    y = jnp.concatenate(ys, axis=-1)
    gate = jax.nn.sigmoid(jnp.dot(y.astype(BF16), wglu_ref[...], preferred_element_type=F32)
                          + bglu_ref[...])
    mix_ref[...] = (y * gate).astype(mix_ref.dtype)


def _s5_sample(u, h0_re, h0_im, pk, w_glu, b_glu):
    b, w = u.shape
    n_state = h0_re.shape[1]
    args = [u, h0_re, h0_im, pk["bb_re"], pk["bb_im"], pk["c_re"], pk["c_imn"], pk["a_re1"], pk["a_im1"],
            pk["d_row"], w_glu, b_glu]
    kern = functools.partial(_s5_sample_kernel, n_chunks=pk["bb_re"].shape[0])
    return pl.pallas_call(
        kern,
        out_shape=[jax.ShapeDtypeStruct((b, w), BF16), jax.ShapeDtypeStruct((b, n_state), F32),
                   jax.ShapeDtypeStruct((b, n_state), F32)],
        compiler_params=pltpu.CompilerParams(vmem_limit_bytes=VMEM_LIMIT), name="s5_sample")(*args)


def _state_tiles(a):
    return a.reshape(SUBLANE, -1, LANE).transpose(1, 0, 2)


def _state_untile(h, g, n):
    return h.transpose(0, 2, 1, 3).reshape(h.shape[0], g, n)


def _s5_pack(a_re, a_im, log_dt, b_re, b_im, c_re, c_im, d):
    g, n = a_re.shape
    p = d.shape[1]
    dt = jnp.exp(log_dt)[:, None]
    mag = jnp.exp(a_re * dt)
    ab_re = mag * jnp.cos(a_im * dt)
    ab_im = mag * jnp.sin(a_im * dt)
    num_re, num_im = ab_re - 1.0, ab_im
    den = a_re * a_re + a_im * a_im
    z_re = (num_re * a_re + num_im * a_im) / den
    z_im = (num_im * a_re - num_re * a_im) / den
    bb_re = z_re[..., None] * b_re - z_im[..., None] * b_im
    bb_im = z_re[..., None] * b_im + z_im[..., None] * b_re
    gpc = S5_CHUNK // n
    n_chunks = g // gpc
    eye = jnp.eye(gpc, dtype=F32)
    chunks_per_in_tile = LANE // (gpc * p)
    chunks_per_out_tile = S5_CHUNK // (gpc * p)

    def pack_in(bb):
        blk = jnp.einsum("qgpn,gh->qgphn", bb.transpose(0, 2, 1).reshape(n_chunks, gpc, p, n), eye)
        blk = blk.reshape(n_chunks, gpc * p, S5_CHUNK)
        sel = jax.nn.one_hot(jnp.arange(n_chunks) % chunks_per_in_tile, chunks_per_in_tile, dtype=F32)
        return jnp.einsum("qrc,qs->qsrc", blk, sel).reshape(n_chunks, LANE, S5_CHUNK).astype(BF16)

    def pack_out(cc):
        blk = jnp.einsum("qgnp,gh->qgnhp", cc.transpose(0, 2, 1).reshape(n_chunks, gpc, n, p), eye)
        blk = blk.reshape(n_chunks, S5_CHUNK, gpc * p)
        sel = jax.nn.one_hot(jnp.arange(n_chunks) % chunks_per_out_tile, chunks_per_out_tile, dtype=F32)
        return jnp.einsum("qrc,qs->qrsc", blk, sel).reshape(n_chunks, S5_CHUNK, S5_CHUNK).astype(BF16)

    return {
        "bb_re": pack_in(bb_re), "bb_im": pack_in(bb_im),
        "c_re": pack_out(c_re), "c_imn": pack_out(-c_im),
        "a_re8": _state_tiles(ab_re), "a_im8": _state_tiles(ab_im),
        "a_re1": ab_re.reshape(1, g * n), "a_im1": ab_im.reshape(1, g * n),
        "d_row": d.reshape(1, g * p),
    }


TM = 512
TN = 512
TF = 512
TQ = 512
S5_LT = 128


def _gain_row(*pieces):
    return jnp.concatenate([jnp.tile(g, reps) for g, reps in pieces]).reshape(1, -1).astype(F32)


def _head_rows(x, n_heads):
    x = x.reshape(x.shape[0], n_heads, LANE)
    return jnp.pad(x, ((0, 0), (0, HEAD_ROWS - n_heads), (0, 0)))


def kernel(x_prompt, x_sample, cache_fox_k, cache_fox_v, cache_fox_logf, state_s5_re, state_s5_im, cache_mem_k, cache_mem_v, page_table, mem_prompt, norm1_g, w_out, mem_norm_g, w_mem_kv, xq_norm_g, xk_norm_g, norm2_g, w_ffn_gate, w_ffn_up, w_ffn_down, fox_w_in, fox_b_f, fox_q_norm_g, fox_k_norm_g, s5_w_in, s5_a_re, s5_a_im, s5_log_dt, s5_b_re, s5_b_im, s5_c_re, s5_c_im, s5_d, s5_w_glu, s5_b_glu):
    bp, tp, d = x_prompt.shape
    bs, ts, _ = x_sample.shape
    depth = norm1_g.shape[0]
    n_fox, n_pool, page, fox_h, fox_hd = cache_fox_k.shape
    fox_w = fox_h * fox_hd
    n_mem, xa_h, xa_hd = cache_mem_k.shape[2:]
    xa_w = xa_h * xa_hd
    s5_g, s5_n = s5_a_re.shape[1:]
    mp, ms = bp * tp, bs * ts
    assert ts == 1 and fox_hd == LANE and xa_hd == LANE

    xp = x_prompt.reshape(mp, d)
    xs = x_sample.reshape(ms, d)
    mem = mem_prompt.reshape(bp * n_mem, d)
    ck = cache_fox_k.transpose(0, 1, 3, 2, 4).reshape(n_fox * n_pool, fox_h, page, fox_hd)
    cv = cache_fox_v.transpose(0, 1, 3, 2, 4).reshape(n_fox * n_pool, fox_h, page, fox_hd)
    clf = cache_fox_logf.transpose(0, 3, 1, 2).reshape(n_fox * fox_h, n_pool, page)
    cmk = cache_mem_k.reshape(depth, bs, n_mem * xa_h, xa_hd)
    cmv = cache_mem_v.reshape(depth, bs, n_mem * xa_h, xa_hd)

    outs = {k: [] for k in ("fk_p", "fv_p", "fl_p", "fk_s", "fv_s", "fl_s",
                            "sr_p", "si_p", "sr_s", "si_s", "mk_p", "mv_p")}
    fox_nt, xa_nt = fox_w // TN, xa_w // TN

    wo16 = w_out.astype(BF16)
    wkv16 = w_mem_kv.astype(BF16)
    wg16, wu16, wd16 = w_ffn_gate.astype(BF16), w_ffn_up.astype(BF16), w_ffn_down.astype(BF16)

    for i in range(depth):
        j = i // 2

        gain = _gain_row((xk_norm_g[i], xa_h), (jnp.ones((xa_hd,), F32), xa_h))
        segs = ((0, xa_nt, True, (0, 1)), (xa_nt, xa_nt, False, (2, 3)))
        mk32, mk16, mv32, mv16 = _proj(mem, mem_norm_g[i], wkv16, gain, segs,
                                       (F32, BF16, F32, BF16), tm=bp * n_mem, tn=TN, layer=i)
        outs["mk_p"].append(mk32.reshape(bp, n_mem, xa_h, xa_hd))
        outs["mv_p"].append(mv32.reshape(bp, n_mem, xa_h, xa_hd))

        if i % 2 == 0:
            w_in = fox_w_in[j]
            w_main = jnp.concatenate([w_in[:, :3 * fox_w], w_in[:, 3 * fox_w + fox_h:]], axis=1).astype(BF16)
            wf = jnp.pad(w_in[:, 3 * fox_w:3 * fox_w + fox_h], ((0, 0), (0, LANE - fox_h))).astype(BF16)
            bf = jnp.pad(fox_b_f[j], (0, LANE - fox_h)).reshape(1, LANE)
            gain = _gain_row((fox_q_norm_g[j] * Q_PRESCALE, fox_h), (fox_k_norm_g[j], fox_h),
                             (jnp.ones((fox_hd,), F32), fox_h), (xq_norm_g[i], xa_h))
            segs = ((0, fox_nt, True, (0,)), (fox_nt, fox_nt, True, (1, 2)),
                    (2 * fox_nt, fox_nt, False, (3, 4)), (3 * fox_nt, xa_nt, True, (5,)))
            dts = (BF16, F32, BF16, F32, BF16, BF16)
            q16, k32, k16, v32, v16, xq16, lf_p, qa, ka = _proj(
                xp, norm1_g[i], w_main, gain, segs, dts, tm=TM, tn=TN, wf=wf, bf=bf, seq_len=tp,
                n_heads=fox_h)
            outs["fk_p"].append(k32.reshape(mp // page, page, fox_h, fox_hd))
            outs["fv_p"].append(v32.reshape(mp // page, page, fox_h, fox_hd))
            outs["fl_p"].append(lf_p[:, :fox_h].reshape(mp // page, page, fox_h))
            mix_p = _flash(q16.reshape(bp, tp, fox_w), qa.reshape(bp, tp, LANE), k16.reshape(bp, tp, fox_w),
                           ka.reshape(bp, tp, LANE), v16.reshape(bp, tp, fox_w), tq=TQ).reshape(mp, fox_w)

            dts_s = (F32, F32, F32, F32)
            segs_s = ((0, fox_nt, True, (0,)), (fox_nt, fox_nt, True, (1,)),
                      (2 * fox_nt, fox_nt, False, (2,)), (3 * fox_nt, xa_nt, True, (3,)))
            qs, ks, vs, xqs, lf_s = _proj(xs, norm1_g[i], w_main, gain, segs_s, dts_s,
                                          tm=ms, tn=TN, wf=wf, bf=bf)
            outs["fk_s"].append(ks.reshape(bs, ts, fox_h, fox_hd))
            outs["fv_s"].append(vs.reshape(bs, ts, fox_h, fox_hd))
            outs["fl_s"].append(lf_s[:, :fox_h].reshape(bs, ts, fox_h))
            lf_col = lf_s[:, :HEAD_ROWS].reshape(bs, HEAD_ROWS, 1)
            mix_s = _dec_fox(qs.reshape(bs, 1, fox_w), ks.reshape(bs, 1, fox_w), vs.reshape(bs, 1, fox_w),
                             lf_col, ck, cv, clf, page_table, j).reshape(ms, fox_w)
        else:
            w_main = s5_w_in[j].astype(BF16)
            s5_nt = (w_main.shape[1] - xa_w) // TN
            gain = _gain_row((jnp.ones((LANE,), F32), s5_nt * TN // LANE), (xq_norm_g[i], xa_h))
            segs = ((0, s5_nt, False, (0,)), (s5_nt, xa_nt, True, (1,)))
            pk = _s5_pack(s5_a_re[j], s5_a_im[j], s5_log_dt[j], s5_b_re[j], s5_b_im[j],
                          s5_c_re[j], s5_c_im[j], s5_d[j])
            w_glu = s5_w_glu[j].astype(BF16)
            b_glu = s5_b_glu[j].reshape(1, -1)
            u_p, xq16 = _proj(xp, norm1_g[i], w_main, gain, segs, (F32, BF16), tm=TM, tn=TN)
            s5_w = u_p.shape[1]
            mix_p, hr, hi = _s5_prompt(u_p.reshape(bp, tp, s5_w), pk, w_glu, b_glu, lt=S5_LT)
            mix_p = mix_p.reshape(mp, s5_w)
            outs["sr_p"].append(_state_untile(hr, s5_g, s5_n))
            outs["si_p"].append(_state_untile(hi, s5_g, s5_n))
            u_s, xqs = _proj(xs, norm1_g[i], w_main, gain, segs, (F32, F32), tm=ms, tn=TN)
            mix_s, hr, hi = _s5_sample(u_s, state_s5_re[j].reshape(bs, s5_g * s5_n),
                                       state_s5_im[j].reshape(bs, s5_g * s5_n), pk, w_glu, b_glu)
            outs["sr_s"].append(hr.reshape(bs, s5_g, s5_n))
            outs["si_s"].append(hi.reshape(bs, s5_g, s5_n))

        xa_p = _xattn(xq16.reshape(bp, tp, xa_w), mk16.reshape(bp, n_mem, xa_w),
                      mv16.reshape(bp, n_mem, xa_w), tq=TQ).reshape(mp, xa_w)
        xa_s = _dec_xattn(_head_rows(xqs, xa_h), cmk, cmv, i, xa_h)[:, :xa_h].reshape(ms, xa_w)

        xp = _outproj(mix_p, xa_p, wo16, i, xp, tm=TM, tn=d // 2)
        xs = _outproj(mix_s, xa_s, wo16, i, xs, tm=ms, tn=d // 2)
        xp = _ffn(xp, norm2_g[i], wg16, wu16, wd16, i, tm=TM, tf=TF)
        xs = _ffn(xs, norm2_g[i], wg16, wu16, wd16, i, tm=ms, tf=TF)

    st = lambda k: jnp.stack(outs[k])
    return (xp.reshape(bp, tp, d), xs.reshape(bs, ts, d),
            st("fk_p"), st("fv_p"), st("fl_p"), st("fk_s"), st("fv_s"), st("fl_s"),
            st("sr_p"), st("si_p"), st("sr_s"), st("si_s"), st("mk_p"), st("mv_p"))
```

```python
import functools
import math

import jax
import jax.numpy as jnp
import numpy as np
from jax import lax
from jax.experimental import pallas as pl
from jax.experimental.pallas import tpu as pltpu

F32 = jnp.float32
BF16 = jnp.bfloat16

EPS = 1e-6
NEG_INF = -1e30
LANE = 128
SUBLANE = 8
HEAD_ROWS = 16
VMEM_LIMIT = 52 * 1024 * 1024
VMEM_LIMIT_FFN = 58 * 1024 * 1024
NT_DIMS = (((1,), (1,)), ((), ()))


def _params(sem, vmem=VMEM_LIMIT):
    return pltpu.CompilerParams(dimension_semantics=sem, vmem_limit_bytes=vmem)


def _split3(x):
    hi = x.astype(BF16)
    r1 = x - hi.astype(F32)
    mid = r1.astype(BF16)
    lo = (r1 - mid.astype(F32)).astype(BF16)
    return hi, mid, lo


def _dot_exact01(x, w01):
    r = jnp.dot(jnp.concatenate(_split3(x), axis=0), w01, preferred_element_type=F32)
    return r[0:HEAD_ROWS] + r[HEAD_ROWS:2 * HEAD_ROWS] + r[2 * HEAD_ROWS:3 * HEAD_ROWS]


def _log_sigmoid(x):
    return jnp.minimum(x, 0.0) - jnp.log1p(jnp.exp(-jnp.abs(x)))


def _head_rmsnorm(y, gain_row):
    parts = []
    for c in range(y.shape[-1] // LANE):
        p = y[:, c * LANE:(c + 1) * LANE]
        parts.append(p * lax.rsqrt(jnp.mean(p * p, axis=-1, keepdims=True) + EPS))
    return jnp.concatenate(parts, axis=-1) * gain_row


AUG_LANES = 8
SQRT2 = math.sqrt(2.0)
Q_PRESCALE = 0.125


def _lane_split3(x):
    return jnp.concatenate(_split3(x), axis=-1)


def _lane_sum3(r):
    return r[:, 0:LANE] + r[:, LANE:2 * LANE] + r[:, 2 * LANE:3 * LANE]


def _proj_kernel(*refs, segs, n_out, with_f, cumsum, seq_tiles):
    x_ref, g_ref, w_ref, gain_ref = refs[:4]
    pos = 4
    if with_f:
        wf_ref, bf_ref = refs[4:6]
        pos = 6
        if cumsum:
            wqa_ref, wka_ref, oneq_ref, onek_ref = refs[6:10]
            pos = 10
    out_refs = refs[pos:pos + n_out]
    pos += n_out
    if with_f:
        lf_ref = refs[pos]
        pos += 1
        if cumsum:
            qa_ref, ka_ref = refs[pos:pos + 2]
            pos += 2
    h_scr = refs[pos]
    pos += 1
    if with_f and cumsum:
        tri_scr, carry_scr = refs[pos:pos + 2]

    i = pl.program_id(0)
    j = pl.program_id(1)
    tm = x_ref.shape[0]

    if with_f and cumsum:
        @pl.when((i == 0) & (j == 0))
        def _():
            t = lax.broadcasted_iota(jnp.int32, (tm, tm), 0)
            u = lax.broadcasted_iota(jnp.int32, (tm, tm), 1)
            tri_scr[...] = jnp.where(u <= t, 1.0, 0.0).astype(BF16)

    @pl.when(j == 0)
    def _():
        x = x_ref[...]
        h = x * lax.rsqrt(jnp.mean(x * x, axis=-1, keepdims=True) + EPS) * g_ref[...]
        hb = h.astype(BF16)
        h_scr[...] = hb
        if with_f:
            lf = _log_sigmoid(jnp.dot(hb, wf_ref[...], preferred_element_type=F32) + bf_ref[...])
            lf_ref[...] = lf
            if cumsum:
                @pl.when(i % seq_tiles == 0)
                def _():
                    carry_scr[...] = jnp.zeros_like(carry_scr)
                c = _lane_sum3(jnp.dot(tri_scr[...], _lane_split3(lf), preferred_element_type=F32))
                c = c + carry_scr[...]
                carry_scr[...] = c[tm - 1:tm, :]
                pieces = _lane_split3(c * SQRT2)
                qa_ref[...] = (jnp.dot(pieces, wqa_ref[...], preferred_element_type=F32)
                               + oneq_ref[...]).astype(qa_ref.dtype)
                ka_ref[...] = (jnp.dot(pieces, wka_ref[...], preferred_element_type=F32)
                               + onek_ref[...]).astype(ka_ref.dtype)

    n_sub = 2 if tm % (2 * HEAD_ROWS) == 0 and tm >= 2 * LANE else 1
    rows = tm // n_sub
    for start, n_tiles, norm, outs in segs:
        @pl.when((j >= start) & (j < start + n_tiles))
        def _(norm=norm, outs=outs):
            for r in range(n_sub):
                rs = slice(r * rows, (r + 1) * rows)
                y = jnp.dot(h_scr[rs, :], w_ref[...], preferred_element_type=F32)
                yy = _head_rmsnorm(y, gain_ref[...]) if norm else y
                for o in outs:
                    out_refs[o][rs, :] = yy.astype(out_refs[o].dtype)


def _aug_maps(n_heads):
    wqa = np.zeros((3 * LANE, LANE), np.float32)
    wka = np.zeros((3 * LANE, LANE), np.float32)
    oneq = np.zeros((1, LANE), np.float32)
    onek = np.zeros((1, LANE), np.float32)
    for h in range(n_heads):
        for piece in range(3):
            wqa[piece * LANE + h, AUG_LANES * h + piece] = 1.0
            wka[piece * LANE + h, AUG_LANES * h + 3 + piece] = -1.0
            oneq[0, AUG_LANES * h + 3 + piece] = 1.0
            onek[0, AUG_LANES * h + piece] = 1.0
    return jnp.asarray(wqa, BF16), jnp.asarray(wka, BF16), jnp.asarray(oneq), jnp.asarray(onek)


def _proj(x, g, w, gain_row, segs, out_dtypes, *, tm, tn, layer=None, wf=None, bf=None, seq_len=None,
          n_heads=None):
    m, k = x.shape
    n = w.shape[-1]
    with_f = wf is not None
    cumsum = seq_len is not None
    grid = (m // tm, n // tn)
    seg_of_out = {}
    for start, n_tiles, _, outs in segs:
        for o in outs:
            seg_of_out[o] = (start, n_tiles)
    const = lambda shape: pl.BlockSpec(shape, lambda i, j: (0,) * len(shape))
    w_spec = (pl.BlockSpec((k, tn), lambda i, j: (0, j)) if layer is None
              else pl.BlockSpec((None, k, tn), lambda i, j: (layer, 0, j)))
    in_specs = [pl.BlockSpec((tm, k), lambda i, j: (i, 0)), const((1, k)), w_spec,
                pl.BlockSpec((1, tn), lambda i, j: (0, j))]
    args = [x, g.reshape(1, k), w, gain_row]
    if with_f:
        in_specs += [const((k, LANE)), const((1, LANE))]
        args += [wf, bf]
        if cumsum:
            maps = _aug_maps(n_heads)
            in_specs += [const(a.shape) for a in maps]
            args += list(maps)
    out_shapes, out_specs = [], []
    for o, dt in enumerate(out_dtypes):
        start, n_tiles = seg_of_out[o]
        out_shapes.append(jax.ShapeDtypeStruct((m, n_tiles * tn), dt))
        out_specs.append(pl.BlockSpec(
            (tm, tn), lambda i, j, s=start, nt=n_tiles: (i, jnp.clip(j - s, 0, nt - 1))))
    scratch = [pltpu.VMEM((tm, k), BF16)]
    seq_tiles = 1
    if with_f:
        row_spec = pl.BlockSpec((tm, LANE), lambda i, j: (i, 0))
        out_shapes.append(jax.ShapeDtypeStruct((m, LANE), F32))
        out_specs.append(row_spec)
        if cumsum:
            seq_tiles = seq_len // tm
            out_shapes += [jax.ShapeDtypeStruct((m, LANE), BF16)] * 2
            out_specs += [row_spec] * 2
            scratch += [pltpu.VMEM((tm, tm), BF16), pltpu.VMEM((1, LANE), F32)]
    kern = functools.partial(_proj_kernel, segs=segs, n_out=len(out_dtypes), with_f=with_f,
                             cumsum=cumsum, seq_tiles=seq_tiles)
    return pl.pallas_call(
        kern, grid=grid, in_specs=in_specs, out_specs=out_specs, out_shape=out_shapes,
        scratch_shapes=scratch, compiler_params=_params(("arbitrary", "arbitrary")),
        name="norm_proj")(*args)


LOG2_SCALE = math.log2(math.e) / SQRT2


def _flash_kernel(qi_ref, kj_ref, q_ref, qa_ref, k_ref, ka_ref, v_ref, o_ref,
                  qaug_scr, m_scr, l_scr, acc_scr):
    h = pl.program_id(1)
    i = qi_ref[pl.program_id(2)]
    j = kj_ref[pl.program_id(2)]
    tq, tk = q_ref.shape[0], k_ref.shape[0]

    @pl.when(j == 0)
    def _():
        lane = lax.broadcasted_iota(jnp.int32, (tq, LANE), 1)
        qaug_scr[:, 0:LANE] = q_ref[...]
        qaug_scr[:, LANE:2 * LANE] = jnp.where(lane // AUG_LANES == h, qa_ref[...], jnp.zeros_like(qa_ref))
        m_scr[...] = jnp.full_like(m_scr, NEG_INF)
        l_scr[...] = jnp.zeros_like(l_scr)
        acc_scr[...] = jnp.zeros_like(acc_scr)

    def step(masked):
        k_aug = jnp.concatenate([k_ref[...], ka_ref[...]], axis=-1)
        x = lax.dot_general(qaug_scr[...], k_aug, NT_DIMS, preferred_element_type=F32) * LOG2_SCALE
        if masked:
            row = lax.broadcasted_iota(jnp.int32, x.shape, 0)
            col = lax.broadcasted_iota(jnp.int32, x.shape, 1)
            x = jnp.where(col <= row, x, NEG_INF)
        m_old = m_scr[...]
        m_new = jnp.maximum(m_old, jnp.max(x, axis=-1, keepdims=True))
        alpha = jnp.exp2(m_old - m_new)
        l_part = alpha * l_scr[...]
        ps = []
        for c in range(tk // LANE):
            p = jnp.exp2(x[:, c * LANE:(c + 1) * LANE] - m_new)
            l_part = l_part + p
            ps.append(p.astype(BF16))
        l_scr[...] = l_part
        m_scr[...] = m_new
        acc_scr[...] = alpha * acc_scr[...] + jnp.dot(jnp.concatenate(ps, axis=-1), v_ref[...],
                                                      preferred_element_type=F32)

    @pl.when(j < i)
    def _():
        step(False)

    @pl.when(j == i)
    def _():
        step(True)
        o_ref[...] = (acc_scr[...] / jnp.sum(l_scr[...], axis=-1, keepdims=True)).astype(o_ref.dtype)


def _flash(q, qa, k, ka, v, *, tq):
    b, t, w = q.shape
    h = w // LANE
    nq = t // tq
    pairs = [(i, j) for i in range(nq) for j in range(i + 1)]
    qi = jnp.asarray([p[0] for p in pairs], jnp.int32)
    kj = jnp.asarray([p[1] for p in pairs], jnp.int32)
    q_map = lambda bb, hh, s, qi, kj: (bb, qi[s], hh)
    kv_map = lambda bb, hh, s, qi, kj: (bb, kj[s], hh)
    grid_spec = pltpu.PrefetchScalarGridSpec(
        num_scalar_prefetch=2, grid=(b, h, len(pairs)),
        in_specs=[
            pl.BlockSpec((None, tq, LANE), q_map),
            pl.BlockSpec((None, tq, LANE), lambda bb, hh, s, qi, kj: (bb, qi[s], 0)),
            pl.BlockSpec((None, tq, LANE), kv_map),
            pl.BlockSpec((None, tq, LANE), lambda bb, hh, s, qi, kj: (bb, kj[s], 0)),
            pl.BlockSpec((None, tq, LANE), kv_map),
        ],
        out_specs=pl.BlockSpec((None, tq, LANE), q_map),
        scratch_shapes=[pltpu.VMEM((tq, 2 * LANE), BF16), pltpu.VMEM((tq, LANE), F32),
                        pltpu.VMEM((tq, LANE), F32), pltpu.VMEM((tq, LANE), F32)])
    return pl.pallas_call(
        _flash_kernel, grid_spec=grid_spec, out_shape=jax.ShapeDtypeStruct((b, t, w), BF16),
        compiler_params=_params(("arbitrary",) * 3), name="fox_prompt_attn")(qi, kj, q, qa, k, ka, v)


def _xattn_kernel(q_ref, k_ref, v_ref, o_ref, *, scale):
    for h in range(q_ref.shape[-1] // LANE):
        sl = slice(h * LANE, (h + 1) * LANE)
        s = lax.dot_general(q_ref[:, sl], k_ref[:, sl], NT_DIMS, preferred_element_type=F32) * scale
        p = jnp.exp(s - jnp.max(s, axis=-1, keepdims=True))
        p = p / jnp.sum(p, axis=-1, keepdims=True)
        o_ref[:, sl] = jnp.dot(p.astype(BF16), v_ref[:, sl],
                               preferred_element_type=F32).astype(o_ref.dtype)


def _xattn(q, k, v, *, tq):
    b, t, w = q.shape
    n_mem = k.shape[1]
    kern = functools.partial(_xattn_kernel, scale=LANE ** -0.5)
    return pl.pallas_call(
        kern, grid=(b, t // tq),
        in_specs=[pl.BlockSpec((None, tq, w), lambda bb, i: (bb, i, 0)),
                  pl.BlockSpec((None, n_mem, w), lambda bb, i: (bb, 0, 0)),
                  pl.BlockSpec((None, n_mem, w), lambda bb, i: (bb, 0, 0))],
        out_specs=pl.BlockSpec((None, tq, w), lambda bb, i: (bb, i, 0)),
        out_shape=jax.ShapeDtypeStruct((b, t, w), BF16),
        compiler_params=_params(("arbitrary", "arbitrary")), name="mem_xattn_prompt")(q, k, v)


def _head_diag(width):
    rows = lax.broadcasted_iota(jnp.int32, (HEAD_ROWS, width), 0)
    lanes = lax.broadcasted_iota(jnp.int32, (HEAD_ROWS, width), 1)
    return (lanes // LANE) == rows


def _dec_fox_kernel(pt_ref, q_ref, kn_ref, vn_ref, lfn_ref, *refs, scale, n_pages, pps):
    kv_refs, lf_refs = refs[:2 * pps], refs[2 * pps:3 * pps]
    o_ref, qbd_scr, m_scr, l_scr, acc_scr, carry_scr = refs[3 * pps:]
    b = pl.program_id(0)
    step = pl.program_id(1)
    n_heads, keys, _ = kv_refs[0].shape
    width = n_heads * LANE
    pair = 2 * LANE
    diag = _head_diag(width)

    @pl.when(step == 0)
    def _():
        qbd = jnp.where(diag, jnp.broadcast_to(q_ref[...], (HEAD_ROWS, width)), 0.0).astype(BF16)
        qbd_scr[...] = qbd
        kn = kn_ref[...].astype(BF16).astype(F32)
        m_scr[...] = jnp.sum(qbd.astype(F32) * kn, axis=-1, keepdims=True) * scale
        l_scr[...] = jnp.ones_like(l_scr)
        acc_scr[...] = jnp.broadcast_to(vn_ref[...].astype(BF16).astype(F32), (HEAD_ROWS, width))
        carry_scr[...] = lfn_ref[...]

    u = lax.broadcasted_iota(jnp.int32, (keys, keys), 0)
    kk = lax.broadcasted_iota(jnp.int32, (keys, keys), 1)
    suffix = jnp.where(u >= kk, 1.0, 0.0).astype(BF16)
    sel_h = lax.broadcasted_iota(jnp.int32, (HEAD_ROWS, n_heads * SUBLANE), 0)
    sel_c = lax.broadcasted_iota(jnp.int32, (HEAD_ROWS, n_heads * SUBLANE), 1)

    scores, incls, lfts = [], [], []
    for i in range(pps):
        k_ref, lf_ref = kv_refs[2 * i], lf_refs[i]
        s = jnp.zeros((HEAD_ROWS, keys), F32)
        for c in range(n_heads // 2):
            kp = jnp.concatenate([k_ref[2 * c], k_ref[2 * c + 1]], axis=-1).astype(BF16)
            s += lax.dot_general(qbd_scr[:, c * pair:(c + 1) * pair], kp, NT_DIMS,
                                 preferred_element_type=F32)
        scores.append(s * scale)
        r = pt_ref[b, n_pages - 1 - (step * pps + i)] % SUBLANE
        sel = jnp.where((sel_c // SUBLANE == sel_h) & (sel_c % SUBLANE == r), 1.0, 0.0).astype(BF16)
        lft = sum(jnp.dot(sel, part, preferred_element_type=F32)
                  for part in _split3(lf_ref[...].reshape(n_heads * SUBLANE, keys)))
        lfts.append(lft)
        incls.append(_dot_exact01(lft, suffix))

    carry = carry_scr[...]
    logits = []
    for i in range(pps):
        logits.append(scores[i] + (carry + incls[i] - lfts[i]))
        carry = carry + incls[i][:, 0:1]
    carry_scr[...] = carry
    logits = jnp.concatenate(logits, axis=-1)

    m_old = m_scr[...]
    m_new = jnp.maximum(m_old, jnp.max(logits, axis=-1, keepdims=True))
    alpha = jnp.exp(m_old - m_new)
    pw = jnp.exp(logits - m_new)
    l_scr[...] = alpha * l_scr[...] + jnp.sum(pw, axis=-1, keepdims=True)
    m_scr[...] = m_new
    pb = pw.astype(BF16)
    for c in range(n_heads // 2):
        vp = jnp.concatenate(
            [jnp.concatenate([kv_refs[2 * i + 1][2 * c], kv_refs[2 * i + 1][2 * c + 1]], axis=-1)
             for i in range(pps)], axis=0).astype(BF16)
        sl = slice(c * pair, (c + 1) * pair)
        acc_scr[:, sl] = alpha * acc_scr[:, sl] + jnp.dot(pb, vp, preferred_element_type=F32)

    @pl.when(step == n_pages // pps - 1)
    def _():
        o_ref[...] = jnp.sum(jnp.where(diag, acc_scr[...] / l_scr[...], 0.0), axis=0, keepdims=True)


DEC_PAGES_PER_STEP = 8


def _dec_fox(q, k_new, v_new, lf_new_col, cache_k, cache_v, cache_lf, page_table, layer):
    b, _, w = q.shape
    n_pages = page_table.shape[1]
    n_heads, page = cache_k.shape[1:3]
    n_pool = cache_lf.shape[1]
    pps = DEC_PAGES_PER_STEP
    assert n_pages % pps == 0 and n_heads % 2 == 0 and n_pool % SUBLANE == 0

    def page_id(bb, p, pt, i):
        return pt[bb, n_pages - 1 - (p * pps + i)]

    row_map = lambda bb, p, pt: (bb, 0, 0)
    row_spec = pl.BlockSpec((None, 1, w), row_map)
    kv_specs, lf_specs = [], []
    for i in range(pps):
        kv_map = lambda bb, p, pt, i=i: (page_id(bb, p, pt, i) + layer * n_pool, 0, 0, 0)
        kv_specs += [pl.BlockSpec((None, n_heads, page, LANE), kv_map)] * 2
        lf_specs.append(pl.BlockSpec((n_heads, SUBLANE, page),
                                     lambda bb, p, pt, i=i: (layer, page_id(bb, p, pt, i) // SUBLANE, 0)))
    kern = functools.partial(_dec_fox_kernel, scale=LANE ** -0.5 / Q_PRESCALE, n_pages=n_pages, pps=pps)
    grid_spec = pltpu.PrefetchScalarGridSpec(
        num_scalar_prefetch=1, grid=(b, n_pages // pps),
        in_specs=[row_spec, row_spec, row_spec, pl.BlockSpec((None, HEAD_ROWS, 1), row_map)]
        + kv_specs + lf_specs,
        out_specs=row_spec,
        scratch_shapes=[pltpu.VMEM((HEAD_ROWS, w), BF16), pltpu.VMEM((HEAD_ROWS, 1), F32),
                        pltpu.VMEM((HEAD_ROWS, 1), F32), pltpu.VMEM((HEAD_ROWS, w), F32),
                        pltpu.VMEM((HEAD_ROWS, 1), F32)])
    return pl.pallas_call(
        kern, grid_spec=grid_spec, out_shape=jax.ShapeDtypeStruct((b, 1, w), F32),
        compiler_params=_params(("arbitrary", "arbitrary")), name="fox_decode_attn")(
            page_table, q, k_new, v_new, lf_new_col, *([cache_k, cache_v] * pps), *([cache_lf] * pps))


def _dec_xattn_kernel(q_ref, k_ref, v_ref, o_ref, *, scale, n_heads):
    s = lax.dot_general(q_ref[...].astype(BF16), k_ref[...].astype(BF16), NT_DIMS,
                        preferred_element_type=F32) * scale
    row = lax.broadcasted_iota(jnp.int32, s.shape, 0)
    col = lax.broadcasted_iota(jnp.int32, s.shape, 1)
    logits = jnp.where(col % n_heads == row, s, NEG_INF)
    p = jnp.exp(logits - jnp.max(logits, axis=-1, keepdims=True))
    p = p / jnp.sum(p, axis=-1, keepdims=True)
    o_ref[...] = jnp.dot(p.astype(BF16), v_ref[...].astype(BF16), preferred_element_type=F32)


def _dec_xattn(q, mem_k, mem_v, layer, n_heads):
    b = q.shape[0]
    n_rows = mem_k.shape[2]
    kern = functools.partial(_dec_xattn_kernel, scale=LANE ** -0.5, n_heads=n_heads)
    head_spec = pl.BlockSpec((None, HEAD_ROWS, LANE), lambda bb: (bb, 0, 0))
    mem_spec = pl.BlockSpec((None, None, n_rows, LANE), lambda bb: (layer, bb, 0, 0))
    return pl.pallas_call(
        kern, grid=(b,), in_specs=[head_spec, mem_spec, mem_spec], out_specs=head_spec,
        out_shape=jax.ShapeDtypeStruct((b, HEAD_ROWS, LANE), F32),
        compiler_params=_params(("arbitrary",)), name="mem_xattn_decode")(q, mem_k, mem_v)


def _outproj_kernel(a_ref, xa_ref, w_ref, x_ref, o_ref):
    lhs = jnp.concatenate([a_ref[...].astype(BF16), xa_ref[...].astype(BF16)], axis=-1)
    o_ref[...] = x_ref[...] + jnp.dot(lhs, w_ref[...], preferred_element_type=F32)


def _outproj(a, xa, w, layer, x, *, tm, tn):
    m, d = x.shape
    ka, kx = a.shape[1], xa.shape[1]
    return pl.pallas_call(
        _outproj_kernel, grid=(m // tm, d // tn),
        in_specs=[pl.BlockSpec((tm, ka), lambda i, j: (i, 0)),
                  pl.BlockSpec((tm, kx), lambda i, j: (i, 0)),
                  pl.BlockSpec((None, ka + kx, tn), lambda i, j: (layer, 0, j)),
                  pl.BlockSpec((tm, tn), lambda i, j: (i, j))],
        out_specs=pl.BlockSpec((tm, tn), lambda i, j: (i, j)),
        out_shape=jax.ShapeDtypeStruct((m, d), F32),
        compiler_params=_params(("arbitrary", "arbitrary")), name="out_proj")(a, xa, w, x)


def _ffn_kernel(x_ref, g_ref, wg_ref, wu_ref, wd_ref, o_ref, h_scr):
    f = pl.program_id(1)

    @pl.when(f == 0)
    def _():
        x = x_ref[...]
        h = x * lax.rsqrt(jnp.mean(x * x, axis=-1, keepdims=True) + EPS) * g_ref[...]
        h_scr[...] = h.astype(BF16)
        o_ref[...] = x

    h = h_scr[...]
    gate = jnp.dot(h, wg_ref[...], preferred_element_type=F32)
    up = jnp.dot(h, wu_ref[...], preferred_element_type=F32)
    act = (gate * jax.nn.sigmoid(gate)) * up
    o_ref[...] += jnp.dot(act.astype(BF16), wd_ref[...], preferred_element_type=F32)


def _ffn(x, g, w_gate, w_up, w_down, layer, *, tm, tf):
    m, d = x.shape
    d_ff = w_gate.shape[-1]
    return pl.pallas_call(
        _ffn_kernel, grid=(m // tm, d_ff // tf),
        in_specs=[pl.BlockSpec((tm, d), lambda i, f: (i, 0)),
                  pl.BlockSpec((1, d), lambda i, f: (0, 0)),
                  pl.BlockSpec((None, d, tf), lambda i, f: (layer, 0, f)),
                  pl.BlockSpec((None, d, tf), lambda i, f: (layer, 0, f)),
                  pl.BlockSpec((None, tf, d), lambda i, f: (layer, f, 0))],
        out_specs=pl.BlockSpec((tm, d), lambda i, f: (i, 0)),
        out_shape=jax.ShapeDtypeStruct((m, d), F32),
        scratch_shapes=[pltpu.VMEM((tm, d), BF16)],
        compiler_params=_params(("arbitrary", "arbitrary"), VMEM_LIMIT_FFN), name="swiglu_ffn")(
            x, g.reshape(1, d), w_gate, w_up, w_down)


S5_CHUNK = 2 * LANE


def _gelu_tanh(y):
    return 0.5 * y * (1.0 + jnp.tanh(math.sqrt(2.0 / math.pi) * (y + 0.044715 * (y * y * y))))


def _s5_prompt_kernel(u_ref, bbre_ref, bbim_ref, cre_ref, cimn_ref, are_ref, aim_ref, d_ref,
                      wglu_ref, bglu_ref, mix_ref, hre_ref, him_ref,
                      sre, sim, hre_s, him_s, y_scr, *, lt, n_chunks):
    t = pl.program_id(1)
    tiles_per_chunk = S5_CHUNK // LANE
    chunks_per_row = are_ref.shape[0] // tiles_per_chunk
    chunks_per_tile = S5_CHUNK // (u_ref.shape[1] // n_chunks)

    @pl.when(t == 0)
    def _():
        hre_s[...] = jnp.zeros_like(hre_s)
        him_s[...] = jnp.zeros_like(him_s)

    u = u_ref[...]
    ub = u.astype(BF16)

    def put(dst, q, val):
        j, lc = divmod(q, chunks_per_row)
        for k in range(tiles_per_chunk):
            dst[lc * tiles_per_chunk + k, pl.ds(j, lt, stride=SUBLANE), :] = val[:, k * LANE:(k + 1) * LANE]

    def get(src, q):
        j, lc = divmod(q, chunks_per_row)
        return jnp.concatenate([src[lc * tiles_per_chunk + k, pl.ds(j, lt, stride=SUBLANE), :]
                                for k in range(tiles_per_chunk)], axis=-1)

    for q in range(n_chunks):
        ut = ub[:, (q // 2) * LANE:(q // 2 + 1) * LANE]
        put(sre, q, jnp.dot(ut, bbre_ref[q], preferred_element_type=F32))
        put(sim, q, jnp.dot(ut, bbim_ref[q], preferred_element_type=F32))

    a_re = are_ref[...]
    a_im = aim_ref[...]

    def body(tt, carry):
        h_re, h_im = carry
        r0 = pl.multiple_of(tt * SUBLANE, SUBLANE)
        n_re = a_re * h_re - a_im * h_im + sre[:, pl.ds(r0, SUBLANE), :]
        n_im = a_re * h_im + a_im * h_re + sim[:, pl.ds(r0, SUBLANE), :]
        sre[:, pl.ds(r0, SUBLANE), :] = n_re
        sim[:, pl.ds(r0, SUBLANE), :] = n_im
        return n_re, n_im

    h_re, h_im = lax.fori_loop(0, lt, body, (hre_s[...], him_s[...]), unroll=8)
    hre_s[...] = h_re
    him_s[...] = h_im
    hre_ref[...] = h_re
    him_ref[...] = h_im

    for c in range(n_chunks // chunks_per_tile):
        acc = jnp.zeros((lt, S5_CHUNK), F32)
        for q in range(c * chunks_per_tile, (c + 1) * chunks_per_tile):
            acc += jnp.dot(get(sre, q).astype(BF16), cre_ref[q], preferred_element_type=F32)
            acc += jnp.dot(get(sim, q).astype(BF16), cimn_ref[q], preferred_element_type=F32)
        sl = slice(c * S5_CHUNK, (c + 1) * S5_CHUNK)
        y_scr[:, sl] = _gelu_tanh(acc + d_ref[:, sl] * u[:, sl])

    y = y_scr[...]
    gate = jax.nn.sigmoid(jnp.dot(y.astype(BF16), wglu_ref[...], preferred_element_type=F32)
                          + bglu_ref[...])
    mix_ref[...] = (y * gate).astype(mix_ref.dtype)


def _s5_prompt(u, pk, w_glu, b_glu, *, lt):
    b, t, w = u.shape
    n_chunks = pk["bb_re"].shape[0]
    st_shape = pk["a_re8"].shape
    full = lambda a: pl.BlockSpec(a.shape, lambda bb, tt, nd=a.ndim: (0,) * nd)
    consts = [pk["bb_re"], pk["bb_im"], pk["c_re"], pk["c_imn"], pk["a_re8"], pk["a_im8"], pk["d_row"],
              w_glu, b_glu]
    kern = functools.partial(_s5_prompt_kernel, lt=lt, n_chunks=n_chunks)
    st_spec = pl.BlockSpec((None,) + st_shape, lambda bb, tt: (bb, 0, 0, 0))
    return pl.pallas_call(
        kern, grid=(b, t // lt),
        in_specs=[pl.BlockSpec((None, lt, w), lambda bb, tt: (bb, tt, 0))] + [full(a) for a in consts],
        out_specs=[pl.BlockSpec((None, lt, w), lambda bb, tt: (bb, tt, 0)), st_spec, st_spec],
        out_shape=[jax.ShapeDtypeStruct((b, t, w), BF16),
                   jax.ShapeDtypeStruct((b,) + st_shape, F32),
                   jax.ShapeDtypeStruct((b,) + st_shape, F32)],
        scratch_shapes=[pltpu.VMEM((st_shape[0], lt * SUBLANE, LANE), F32),
                        pltpu.VMEM((st_shape[0], lt * SUBLANE, LANE), F32),
                        pltpu.VMEM(st_shape, F32), pltpu.VMEM(st_shape, F32),
                        pltpu.VMEM((lt, w), F32)],
        compiler_params=_params(("arbitrary", "arbitrary")), name="s5_prompt")(u, *consts)


def _s5_sample_kernel(u_ref, h0re_ref, h0im_ref, bbre_ref, bbim_ref, cre_ref, cimn_ref, are_ref, aim_ref,
                      d_ref, wglu_ref, bglu_ref, mix_ref, hre_ref, him_ref, *, n_chunks):
    u = u_ref[...]
    ub = u.astype(BF16)
    chunks_per_tile = S5_CHUNK // (u.shape[1] // n_chunks)
    ys = []
    for c in range(n_chunks // chunks_per_tile):
        acc = jnp.zeros((u.shape[0], S5_CHUNK), F32)
        for q in range(c * chunks_per_tile, (c + 1) * chunks_per_tile):
            ut = ub[:, (q // 2) * LANE:(q // 2 + 1) * LANE]
            sl = slice(q * S5_CHUNK, (q + 1) * S5_CHUNK)
            a_re, a_im = are_ref[:, sl], aim_ref[:, sl]
            h_re, h_im = h0re_ref[:, sl], h0im_ref[:, sl]
            n_re = a_re * h_re - a_im * h_im + jnp.dot(ut, bbre_ref[q], preferred_element_type=F32)
            n_im = a_re * h_im + a_im * h_re + jnp.dot(ut, bbim_ref[q], preferred_element_type=F32)
            hre_ref[:, sl] = n_re
            him_ref[:, sl] = n_im
            acc += jnp.dot(n_re.astype(BF16), cre_ref[q], preferred_element_type=F32)
            acc += jnp.dot(n_im.astype(BF16), cimn_ref[q], preferred_element_type=F32)
        cs = slice(c * S5_CHUNK, (c + 1) * S5_CHUNK)
        ys.append(_gelu_tanh(acc + d_ref[:, cs] * u[:, cs]))
    y = jnp.concatenate(ys, axis=-1)
    gate = jax.nn.sigmoid(jnp.dot(y.astype(BF16), wglu_ref[...], preferred_element_type=F32)
                          + bglu_ref[...])
    mix_ref[...] = (y * gate).astype(mix_ref.dtype)


def _s5_sample(u, h0_re, h0_im, pk, w_glu, b_glu):
    b, w = u.shape
    n_state = h0_re.shape[1]
    args = [u, h0_re, h0_im, pk["bb_re"], pk["bb_im"], pk["c_re"], pk["c_imn"], pk["a_re1"], pk["a_im1"],
            pk["d_row"], w_glu, b_glu]
    kern = functools.partial(_s5_sample_kernel, n_chunks=pk["bb_re"].shape[0])
    return pl.pallas_call(
        kern,
        out_shape=[jax.ShapeDtypeStruct((b, w), BF16), jax.ShapeDtypeStruct((b, n_state), F32),
                   jax.ShapeDtypeStruct((b, n_state), F32)],
        compiler_params=pltpu.CompilerParams(vmem_limit_bytes=VMEM_LIMIT), name="s5_sample")(*args)


def _state_tiles(a):
    return a.reshape(SUBLANE, -1, LANE).transpose(1, 0, 2)


def _state_untile(h, g, n):
    return h.transpose(0, 2, 1, 3).reshape(h.shape[0], g, n)


def _s5_pack(a_re, a_im, log_dt, b_re, b_im, c_re, c_im, d):
    g, n = a_re.shape
    p = d.shape[1]
    dt = jnp.exp(log_dt)[:, None]
    mag = jnp.exp(a_re * dt)
    ab_re = mag * jnp.cos(a_im * dt)
    ab_im = mag * jnp.sin(a_im * dt)
    num_re, num_im = ab_re - 1.0, ab_im
    den = a_re * a_re + a_im * a_im
    z_re = (num_re * a_re + num_im * a_im) / den
    z_im = (num_im * a_re - num_re * a_im) / den
    bb_re = z_re[..., None] * b_re - z_im[..., None] * b_im
    bb_im = z_re[..., None] * b_im + z_im[..., None] * b_re
    gpc = S5_CHUNK // n
    n_chunks = g // gpc
    eye = jnp.eye(gpc, dtype=F32)
    chunks_per_in_tile = LANE // (gpc * p)
    chunks_per_out_tile = S5_CHUNK // (gpc * p)

    def pack_in(bb):
        blk = jnp.einsum("qgpn,gh->qgphn", bb.transpose(0, 2, 1).reshape(n_chunks, gpc, p, n), eye)
        blk = blk.reshape(n_chunks, gpc * p, S5_CHUNK)
        sel = jax.nn.one_hot(jnp.arange(n_chunks) % chunks_per_in_tile, chunks_per_in_tile, dtype=F32)
        return jnp.einsum("qrc,qs->qsrc", blk, sel).reshape(n_chunks, LANE, S5_CHUNK).astype(BF16)

    def pack_out(cc):
        blk = jnp.einsum("qgnp,gh->qgnhp", cc.transpose(0, 2, 1).reshape(n_chunks, gpc, n, p), eye)
        blk = blk.reshape(n_chunks, S5_CHUNK, gpc * p)
        sel = jax.nn.one_hot(jnp.arange(n_chunks) % chunks_per_out_tile, chunks_per_out_tile, dtype=F32)
        return jnp.einsum("qrc,qs->qrsc", blk, sel).reshape(n_chunks, S5_CHUNK, S5_CHUNK).astype(BF16)

    return {
        "bb_re": pack_in(bb_re), "bb_im": pack_in(bb_im),
        "c_re": pack_out(c_re), "c_imn": pack_out(-c_im),
        "a_re8": _state_tiles(ab_re), "a_im8": _state_tiles(ab_im),
        "a_re1": ab_re.reshape(1, g * n), "a_im1": ab_im.reshape(1, g * n),
        "d_row": d.reshape(1, g * p),
    }


TM = 512
TN = 512
TM_FFN = 1024
TF = 512
TQ = 512
S5_LT = 256


def _gain_row(*pieces):
    return jnp.concatenate([jnp.tile(g, reps) for g, reps in pieces]).reshape(1, -1).astype(F32)


def _head_rows(x, n_heads):
    x = x.reshape(x.shape[0], n_heads, LANE)
    return jnp.pad(x, ((0, 0), (0, HEAD_ROWS - n_heads), (0, 0)))


def kernel(x_prompt, x_sample, cache_fox_k, cache_fox_v, cache_fox_logf, state_s5_re, state_s5_im, cache_mem_k, cache_mem_v, page_table, mem_prompt, norm1_g, w_out, mem_norm_g, w_mem_kv, xq_norm_g, xk_norm_g, norm2_g, w_ffn_gate, w_ffn_up, w_ffn_down, fox_w_in, fox_b_f, fox_q_norm_g, fox_k_norm_g, s5_w_in, s5_a_re, s5_a_im, s5_log_dt, s5_b_re, s5_b_im, s5_c_re, s5_c_im, s5_d, s5_w_glu, s5_b_glu):
    bp, tp, d = x_prompt.shape
    bs, ts, _ = x_sample.shape
    depth = norm1_g.shape[0]
    n_fox, n_pool, page, fox_h, fox_hd = cache_fox_k.shape
    fox_w = fox_h * fox_hd
    n_mem, xa_h, xa_hd = cache_mem_k.shape[2:]
    xa_w = xa_h * xa_hd
    s5_g, s5_n = s5_a_re.shape[1:]
    mp, ms = bp * tp, bs * ts
    assert ts == 1 and fox_hd == LANE and xa_hd == LANE

    xp = x_prompt.reshape(mp, d)
    xs = x_sample.reshape(ms, d)
    mem = mem_prompt.reshape(bp * n_mem, d)
    ck = cache_fox_k.transpose(0, 1, 3, 2, 4).reshape(n_fox * n_pool, fox_h, page, fox_hd)
    cv = cache_fox_v.transpose(0, 1, 3, 2, 4).reshape(n_fox * n_pool, fox_h, page, fox_hd)
    clf = cache_fox_logf.transpose(0, 3, 1, 2).reshape(n_fox * fox_h, n_pool, page)
    cmk = cache_mem_k.reshape(depth, bs, n_mem * xa_h, xa_hd)
    cmv = cache_mem_v.reshape(depth, bs, n_mem * xa_h, xa_hd)

    outs = {k: [] for k in ("fk_p", "fv_p", "fl_p", "fk_s", "fv_s", "fl_s",
                            "sr_p", "si_p", "sr_s", "si_s", "mk_p", "mv_p")}
    fox_nt, xa_nt = fox_w // TN, xa_w // TN

    wo16 = w_out.astype(BF16)
    wkv16 = w_mem_kv.astype(BF16)
    wg16, wu16, wd16 = w_ffn_gate.astype(BF16), w_ffn_up.astype(BF16), w_ffn_down.astype(BF16)

    for i in range(depth):
        j = i // 2

        gain = _gain_row((xk_norm_g[i], xa_h), (jnp.ones((xa_hd,), F32), xa_h))
        segs = ((0, xa_nt, True, (0, 1)), (xa_nt, xa_nt, False, (2, 3)))
        mk32, mk16, mv32, mv16 = _proj(mem, mem_norm_g[i], wkv16, gain, segs,
                                       (F32, BF16, F32, BF16), tm=bp * n_mem, tn=TN, layer=i)
        outs["mk_p"].append(mk32.reshape(bp, n_mem, xa_h, xa_hd))
        outs["mv_p"].append(mv32.reshape(bp, n_mem, xa_h, xa_hd))

        if i % 2 == 0:
            w_in = fox_w_in[j]
            w_main = jnp.concatenate([w_in[:, :3 * fox_w], w_in[:, 3 * fox_w + fox_h:]], axis=1).astype(BF16)
            wf = jnp.pad(w_in[:, 3 * fox_w:3 * fox_w + fox_h], ((0, 0), (0, LANE - fox_h))).astype(BF16)
            bf = jnp.pad(fox_b_f[j], (0, LANE - fox_h)).reshape(1, LANE)
            gain = _gain_row((fox_q_norm_g[j] * Q_PRESCALE, fox_h), (fox_k_norm_g[j], fox_h),
                             (jnp.ones((fox_hd,), F32), fox_h), (xq_norm_g[i], xa_h))
            segs = ((0, fox_nt, True, (0,)), (fox_nt, fox_nt, True, (1, 2)),
                    (2 * fox_nt, fox_nt, False, (3, 4)), (3 * fox_nt, xa_nt, True, (5,)))
            dts = (BF16, F32, BF16, F32, BF16, BF16)
            q16, k32, k16, v32, v16, xq16, lf_p, qa, ka = _proj(
                xp, norm1_g[i], w_main, gain, segs, dts, tm=TM, tn=TN, wf=wf, bf=bf, seq_len=tp,
                n_heads=fox_h)
            outs["fk_p"].append(k32.reshape(mp // page, page, fox_h, fox_hd))
            outs["fv_p"].append(v32.reshape(mp // page, page, fox_h, fox_hd))
            outs["fl_p"].append(lf_p[:, :fox_h].reshape(mp // page, page, fox_h))
            mix_p = _flash(q16.reshape(bp, tp, fox_w), qa.reshape(bp, tp, LANE), k16.reshape(bp, tp, fox_w),
                           ka.reshape(bp, tp, LANE), v16.reshape(bp, tp, fox_w), tq=TQ).reshape(mp, fox_w)

            dts_s = (F32, F32, F32, F32)
            segs_s = ((0, fox_nt, True, (0,)), (fox_nt, fox_nt, True, (1,)),
                      (2 * fox_nt, fox_nt, False, (2,)), (3 * fox_nt, xa_nt, True, (3,)))
            qs, ks, vs, xqs, lf_s = _proj(xs, norm1_g[i], w_main, gain, segs_s, dts_s,
                                          tm=ms, tn=TN, wf=wf, bf=bf)
            outs["fk_s"].append(ks.reshape(bs, ts, fox_h, fox_hd))
            outs["fv_s"].append(vs.reshape(bs, ts, fox_h, fox_hd))
            outs["fl_s"].append(lf_s[:, :fox_h].reshape(bs, ts, fox_h))
            lf_col = lf_s[:, :HEAD_ROWS].reshape(bs, HEAD_ROWS, 1)
            mix_s = _dec_fox(qs.reshape(bs, 1, fox_w), ks.reshape(bs, 1, fox_w), vs.reshape(bs, 1, fox_w),
                             lf_col, ck, cv, clf, page_table, j).reshape(ms, fox_w)
        else:
            w_main = s5_w_in[j].astype(BF16)
            s5_nt = (w_main.shape[1] - xa_w) // TN
            gain = _gain_row((jnp.ones((LANE,), F32), s5_nt * TN // LANE), (xq_norm_g[i], xa_h))
            segs = ((0, s5_nt, False, (0,)), (s5_nt, xa_nt, True, (1,)))
            pk = _s5_pack(s5_a_re[j], s5_a_im[j], s5_log_dt[j], s5_b_re[j], s5_b_im[j],
                          s5_c_re[j], s5_c_im[j], s5_d[j])
            w_glu = s5_w_glu[j].astype(BF16)
            b_glu = s5_b_glu[j].reshape(1, -1)
            u_p, xq16 = _proj(xp, norm1_g[i], w_main, gain, segs, (F32, BF16), tm=TM, tn=TN)
            s5_w = u_p.shape[1]
            mix_p, hr, hi = _s5_prompt(u_p.reshape(bp, tp, s5_w), pk, w_glu, b_glu, lt=S5_LT)
            mix_p = mix_p.reshape(mp, s5_w)
            outs["sr_p"].append(_state_untile(hr, s5_g, s5_n))
            outs["si_p"].append(_state_untile(hi, s5_g, s5_n))
            u_s, xqs = _proj(xs, norm1_g[i], w_main, gain, segs, (F32, F32), tm=ms, tn=TN)
            mix_s, hr, hi = _s5_sample(u_s, state_s5_re[j].reshape(bs, s5_g * s5_n),
                                       state_s5_im[j].reshape(bs, s5_g * s5_n), pk, w_glu, b_glu)
            outs["sr_s"].append(hr.reshape(bs, s5_g, s5_n))
            outs["si_s"].append(hi.reshape(bs, s5_g, s5_n))

        xa_p = _xattn(xq16.reshape(bp, tp, xa_w), mk16.reshape(bp, n_mem, xa_w),
                      mv16.reshape(bp, n_mem, xa_w), tq=TQ).reshape(mp, xa_w)
        xa_s = _dec_xattn(_head_rows(xqs, xa_h), cmk, cmv, i, xa_h)[:, :xa_h].reshape(ms, xa_w)

        xp = _outproj(mix_p, xa_p, wo16, i, xp, tm=TM, tn=d // 2)
        xs = _outproj(mix_s, xa_s, wo16, i, xs, tm=ms, tn=d // 2)
        xp = _ffn(xp, norm2_g[i], wg16, wu16, wd16, i, tm=TM_FFN, tf=TF)
        xs = _ffn(xs, norm2_g[i], wg16, wu16, wd16, i, tm=ms, tf=TF)

    st = lambda k: jnp.stack(outs[k])
    return (xp.reshape(bp, tp, d), xs.reshape(bs, ts, d),
            st("fk_p"), st("fv_p"), st("fl_p"), st("fk_s"), st("fv_s"), st("fl_s"),
            st("sr_p"), st("si_p"), st("sr_s"), st("si_s"), st("mk_p"), st("mv_p"))
```

```python
import functools
import math

import jax
import jax.numpy as jnp
import numpy as np
from jax import lax
from jax.experimental import pallas as pl
from jax.experimental.pallas import tpu as pltpu

F32 = jnp.float32
BF16 = jnp.bfloat16

EPS = 1e-6
NEG_INF = -1e30
LANE = 128
SUBLANE = 8
HEAD_ROWS = 16
VMEM_LIMIT = 52 * 1024 * 1024
VMEM_LIMIT_FFN = 58 * 1024 * 1024
NT_DIMS = (((1,), (1,)), ((), ()))


def _params(sem, vmem=VMEM_LIMIT):
    return pltpu.CompilerParams(dimension_semantics=sem, vmem_limit_bytes=vmem)


def _split3(x):
    hi = x.astype(BF16)
    r1 = x - hi.astype(F32)
    mid = r1.astype(BF16)
    lo = (r1 - mid.astype(F32)).astype(BF16)
    return hi, mid, lo


def _dot_exact01(x, w01):
    r = jnp.dot(jnp.concatenate(_split3(x), axis=0), w01, preferred_element_type=F32)
    return r[0:HEAD_ROWS] + r[HEAD_ROWS:2 * HEAD_ROWS] + r[2 * HEAD_ROWS:3 * HEAD_ROWS]


def _log_sigmoid(x):
    return jnp.minimum(x, 0.0) - jnp.log1p(jnp.exp(-jnp.abs(x)))


def _head_rmsnorm(y, gain_row):
    parts = []
    for c in range(y.shape[-1] // LANE):
        p = y[:, c * LANE:(c + 1) * LANE]
        parts.append(p * lax.rsqrt(jnp.mean(p * p, axis=-1, keepdims=True) + EPS))
    return jnp.concatenate(parts, axis=-1) * gain_row


AUG_LANES = 8
SQRT2 = math.sqrt(2.0)
Q_PRESCALE = 0.125


def _lane_split3(x):
    return jnp.concatenate(_split3(x), axis=-1)


def _lane_sum3(r):
    return r[:, 0:LANE] + r[:, LANE:2 * LANE] + r[:, 2 * LANE:3 * LANE]


def _proj_kernel(*refs, segs, n_out, with_f, cumsum, seq_tiles):
    x_ref, g_ref, w_ref, gain_ref = refs[:4]
    pos = 4
    if with_f:
        wf_ref, bf_ref = refs[4:6]
        pos = 6
        if cumsum:
            wqa_ref, wka_ref, oneq_ref, onek_ref = refs[6:10]
            pos = 10
    out_refs = refs[pos:pos + n_out]
    pos += n_out
    if with_f:
        lf_ref = refs[pos]
        pos += 1
        if cumsum:
            qa_ref, ka_ref = refs[pos:pos + 2]
            pos += 2
    h_scr = refs[pos]
    pos += 1
    if with_f and cumsum:
        tri_scr, carry_scr = refs[pos:pos + 2]

    i = pl.program_id(0)
    j = pl.program_id(1)
    tm = x_ref.shape[0]

    if with_f and cumsum:
        @pl.when((i == 0) & (j == 0))
        def _():
            t = lax.broadcasted_iota(jnp.int32, (tm, tm), 0)
            u = lax.broadcasted_iota(jnp.int32, (tm, tm), 1)
            tri_scr[...] = jnp.where(u <= t, 1.0, 0.0).astype(BF16)

    @pl.when(j == 0)
    def _():
        x = x_ref[...]
        h = x * lax.rsqrt(jnp.mean(x * x, axis=-1, keepdims=True) + EPS) * g_ref[...]
        hb = h.astype(BF16)
        h_scr[...] = hb
        if with_f:
            lf = _log_sigmoid(jnp.dot(hb, wf_ref[...], preferred_element_type=F32) + bf_ref[...])
            lf_ref[...] = lf
            if cumsum:
                @pl.when(i % seq_tiles == 0)
                def _():
                    carry_scr[...] = jnp.zeros_like(carry_scr)
                c = _lane_sum3(jnp.dot(tri_scr[...], _lane_split3(lf), preferred_element_type=F32))
                c = c + carry_scr[...]
                carry_scr[...] = c[tm - 1:tm, :]
                pieces = _lane_split3(c * SQRT2)
                qa_ref[...] = (jnp.dot(pieces, wqa_ref[...], preferred_element_type=F32)
                               + oneq_ref[...]).astype(qa_ref.dtype)
                ka_ref[...] = (jnp.dot(pieces, wka_ref[...], preferred_element_type=F32)
                               + onek_ref[...]).astype(ka_ref.dtype)

    n_sub = 2 if tm % (2 * HEAD_ROWS) == 0 and tm >= 2 * LANE else 1
    rows = tm // n_sub
    for start, n_tiles, norm, outs in segs:
        @pl.when((j >= start) & (j < start + n_tiles))
        def _(norm=norm, outs=outs):
            for r in range(n_sub):
                rs = slice(r * rows, (r + 1) * rows)
                y = jnp.dot(h_scr[rs, :], w_ref[...], preferred_element_type=F32)
                yy = _head_rmsnorm(y, gain_ref[...]) if norm else y
                for o in outs:
                    out_refs[o][rs, :] = yy.astype(out_refs[o].dtype)


def _aug_maps(n_heads):
    wqa = np.zeros((3 * LANE, LANE), np.float32)
    wka = np.zeros((3 * LANE, LANE), np.float32)
    oneq = np.zeros((1, LANE), np.float32)
    onek = np.zeros((1, LANE), np.float32)
    for h in range(n_heads):
        for piece in range(3):
            wqa[piece * LANE + h, AUG_LANES * h + piece] = 1.0
            wka[piece * LANE + h, AUG_LANES * h + 3 + piece] = -1.0
            oneq[0, AUG_LANES * h + 3 + piece] = 1.0
            onek[0, AUG_LANES * h + piece] = 1.0
    return jnp.asarray(wqa, BF16), jnp.asarray(wka, BF16), jnp.asarray(oneq), jnp.asarray(onek)


def _proj(x, g, w, gain_row, segs, out_dtypes, *, tm, tn, layer=None, wf=None, bf=None, seq_len=None,
          n_heads=None):
    m, k = x.shape
    n = w.shape[-1]
    with_f = wf is not None
    cumsum = seq_len is not None
    grid = (m // tm, n // tn)
    seg_of_out = {}
    for start, n_tiles, _, outs in segs:
        for o in outs:
            seg_of_out[o] = (start, n_tiles)
    const = lambda shape: pl.BlockSpec(shape, lambda i, j: (0,) * len(shape))
    w_spec = (pl.BlockSpec((k, tn), lambda i, j: (0, j)) if layer is None
              else pl.BlockSpec((None, k, tn), lambda i, j: (layer, 0, j)))
    in_specs = [pl.BlockSpec((tm, k), lambda i, j: (i, 0)), const((1, k)), w_spec,
                pl.BlockSpec((1, tn), lambda i, j: (0, j))]
    args = [x, g.reshape(1, k), w, gain_row]
    if with_f:
        in_specs += [const((k, LANE)), const((1, LANE))]
        args += [wf, bf]
        if cumsum:
            maps = _aug_maps(n_heads)
            in_specs += [const(a.shape) for a in maps]
            args += list(maps)
    out_shapes, out_specs = [], []
    for o, dt in enumerate(out_dtypes):
        start, n_tiles = seg_of_out[o]
        out_shapes.append(jax.ShapeDtypeStruct((m, n_tiles * tn), dt))
        out_specs.append(pl.BlockSpec(
            (tm, tn), lambda i, j, s=start, nt=n_tiles: (i, jnp.clip(j - s, 0, nt - 1))))
    scratch = [pltpu.VMEM((tm, k), BF16)]
    seq_tiles = 1
    if with_f:
        row_spec = pl.BlockSpec((tm, LANE), lambda i, j: (i, 0))
        out_shapes.append(jax.ShapeDtypeStruct((m, LANE), F32))
        out_specs.append(row_spec)
        if cumsum:
            seq_tiles = seq_len // tm
            out_shapes += [jax.ShapeDtypeStruct((m, LANE), BF16)] * 2
            out_specs += [row_spec] * 2
            scratch += [pltpu.VMEM((tm, tm), BF16), pltpu.VMEM((1, LANE), F32)]
    kern = functools.partial(_proj_kernel, segs=segs, n_out=len(out_dtypes), with_f=with_f,
                             cumsum=cumsum, seq_tiles=seq_tiles)
    return pl.pallas_call(
        kern, grid=grid, in_specs=in_specs, out_specs=out_specs, out_shape=out_shapes,
        scratch_shapes=scratch, compiler_params=_params(("arbitrary", "arbitrary")),
        name="norm_proj")(*args)


LOG2_SCALE = math.log2(math.e) / SQRT2


def _flash_kernel(qi_ref, kj_ref, q_ref, qa_ref, k_ref, ka_ref, v_ref, o_ref,
                  qaug_scr, m_scr, l_scr, acc_scr):
    hg = pl.program_id(1)
    i = qi_ref[pl.program_id(2)]
    j = kj_ref[pl.program_id(2)]
    tq, tk = q_ref.shape[0], k_ref.shape[0]
    n_hp = q_ref.shape[1] // LANE

    @pl.when(j == 0)
    def _():
        lane = lax.broadcasted_iota(jnp.int32, (tq, LANE), 1)
        for hp in range(n_hp):
            qaug_scr[hp, :, 0:LANE] = q_ref[:, hp * LANE:(hp + 1) * LANE]
            qaug_scr[hp, :, LANE:2 * LANE] = jnp.where(lane // AUG_LANES == hg * n_hp + hp, qa_ref[...],
                                                       jnp.zeros_like(qa_ref))
        m_scr[...] = jnp.full_like(m_scr, NEG_INF)
        l_scr[...] = jnp.zeros_like(l_scr)
        acc_scr[...] = jnp.zeros_like(acc_scr)

    def step(masked):
        for hp in range(n_hp):
            hs = slice(hp * LANE, (hp + 1) * LANE)
            k_aug = jnp.concatenate([k_ref[:, hs], ka_ref[...]], axis=-1)
            x = lax.dot_general(qaug_scr[hp], k_aug, NT_DIMS, preferred_element_type=F32) * LOG2_SCALE
            if masked:
                row = lax.broadcasted_iota(jnp.int32, x.shape, 0)
                col = lax.broadcasted_iota(jnp.int32, x.shape, 1)
                x = jnp.where(col <= row, x, NEG_INF)
            m_old = m_scr[hp]
            m_new = jnp.maximum(m_old, jnp.max(x, axis=-1, keepdims=True))
            alpha = jnp.exp2(m_old - m_new)
            l_part = alpha * l_scr[hp]
            ps = []
            for c in range(tk // LANE):
                p = jnp.exp2(x[:, c * LANE:(c + 1) * LANE] - m_new)
                l_part = l_part + p
                ps.append(p.astype(BF16))
            l_scr[hp] = l_part
            m_scr[hp] = m_new
            acc_scr[hp] = alpha * acc_scr[hp] + jnp.dot(jnp.concatenate(ps, axis=-1), v_ref[:, hs],
                                                        preferred_element_type=F32)

    @pl.when(j < i)
    def _():
        step(False)

    @pl.when(j == i)
    def _():
        step(True)
        for hp in range(n_hp):
            o_ref[:, hp * LANE:(hp + 1) * LANE] = (
                acc_scr[hp] / jnp.sum(l_scr[hp], axis=-1, keepdims=True)).astype(o_ref.dtype)


FLASH_HEADS_PER_STEP = 2


def _flash(q, qa, k, ka, v, *, tq):
    b, t, w = q.shape
    n_hp = FLASH_HEADS_PER_STEP
    hw = n_hp * LANE
    assert w % hw == 0
    nq = t // tq
    pairs = [(i, j) for i in range(nq) for j in range(i + 1)]
    qi = jnp.asarray([p[0] for p in pairs], jnp.int32)
    kj = jnp.asarray([p[1] for p in pairs], jnp.int32)
    q_map = lambda bb, hh, s, qi, kj: (bb, qi[s], hh)
    kv_map = lambda bb, hh, s, qi, kj: (bb, kj[s], hh)
    grid_spec = pltpu.PrefetchScalarGridSpec(
        num_scalar_prefetch=2, grid=(b, w // hw, len(pairs)),
        in_specs=[
            pl.BlockSpec((None, tq, hw), q_map),
            pl.BlockSpec((None, tq, LANE), lambda bb, hh, s, qi, kj: (bb, qi[s], 0)),
            pl.BlockSpec((None, tq, hw), kv_map),
            pl.BlockSpec((None, tq, LANE), lambda bb, hh, s, qi, kj: (bb, kj[s], 0)),
            pl.BlockSpec((None, tq, hw), kv_map),
        ],
        out_specs=pl.BlockSpec((None, tq, hw), q_map),
        scratch_shapes=[pltpu.VMEM((n_hp, tq, 2 * LANE), BF16), pltpu.VMEM((n_hp, tq, LANE), F32),
                        pltpu.VMEM((n_hp, tq, LANE), F32), pltpu.VMEM((n_hp, tq, LANE), F32)])
    return pl.pallas_call(
        _flash_kernel, grid_spec=grid_spec, out_shape=jax.ShapeDtypeStruct((b, t, w), BF16),
        compiler_params=_params(("arbitrary",) * 3), name="fox_prompt_attn")(qi, kj, q, qa, k, ka, v)


def _xattn_kernel(q_ref, k_ref, v_ref, o_ref, *, scale):
    for h in range(q_ref.shape[-1] // LANE):
        sl = slice(h * LANE, (h + 1) * LANE)
        s = lax.dot_general(q_ref[:, sl], k_ref[:, sl], NT_DIMS, preferred_element_type=F32) * scale
        p = jnp.exp(s - jnp.max(s, axis=-1, keepdims=True))
        p = p / jnp.sum(p, axis=-1, keepdims=True)
        o_ref[:, sl] = jnp.dot(p.astype(BF16), v_ref[:, sl],
                               preferred_element_type=F32).astype(o_ref.dtype)


def _xattn(q, k, v, *, tq):
    b, t, w = q.shape
    n_mem = k.shape[1]
    kern = functools.partial(_xattn_kernel, scale=LANE ** -0.5)
    return pl.pallas_call(
        kern, grid=(b, t // tq),
        in_specs=[pl.BlockSpec((None, tq, w), lambda bb, i: (bb, i, 0)),
                  pl.BlockSpec((None, n_mem, w), lambda bb, i: (bb, 0, 0)),
                  pl.BlockSpec((None, n_mem, w), lambda bb, i: (bb, 0, 0))],
        out_specs=pl.BlockSpec((None, tq, w), lambda bb, i: (bb, i, 0)),
        out_shape=jax.ShapeDtypeStruct((b, t, w), BF16),
        compiler_params=_params(("arbitrary", "arbitrary")), name="mem_xattn_prompt")(q, k, v)


def _head_diag(width):
    rows = lax.broadcasted_iota(jnp.int32, (HEAD_ROWS, width), 0)
    lanes = lax.broadcasted_iota(jnp.int32, (HEAD_ROWS, width), 1)
    return (lanes // LANE) == rows


def _dec_fox_kernel(pt_ref, q_ref, kn_ref, vn_ref, lfn_ref, *refs, scale, n_pages, pps):
    kv_refs, lf_refs = refs[:2 * pps], refs[2 * pps:3 * pps]
    o_ref, qbd_scr, m_scr, l_scr, acc_scr, carry_scr = refs[3 * pps:]
    b = pl.program_id(0)
    step = pl.program_id(1)
    n_heads, keys, _ = kv_refs[0].shape
    width = n_heads * LANE
    pair = 2 * LANE
    diag = _head_diag(width)

    @pl.when(step == 0)
    def _():
        qbd = jnp.where(diag, jnp.broadcast_to(q_ref[...], (HEAD_ROWS, width)), 0.0).astype(BF16)
        qbd_scr[...] = qbd
        kn = kn_ref[...].astype(BF16).astype(F32)
        m_scr[...] = jnp.sum(qbd.astype(F32) * kn, axis=-1, keepdims=True) * scale
        l_scr[...] = jnp.ones_like(l_scr)
        acc_scr[...] = jnp.broadcast_to(vn_ref[...].astype(BF16).astype(F32), (HEAD_ROWS, width))
        carry_scr[...] = lfn_ref[...]

    u = lax.broadcasted_iota(jnp.int32, (keys, keys), 0)
    kk = lax.broadcasted_iota(jnp.int32, (keys, keys), 1)
    suffix = jnp.where(u >= kk, 1.0, 0.0).astype(BF16)
    sel_h = lax.broadcasted_iota(jnp.int32, (HEAD_ROWS, n_heads * SUBLANE), 0)
    sel_c = lax.broadcasted_iota(jnp.int32, (HEAD_ROWS, n_heads * SUBLANE), 1)

    scores, incls, lfts = [], [], []
    for i in range(pps):
        k_ref, lf_ref = kv_refs[2 * i], lf_refs[i]
        s = jnp.zeros((HEAD_ROWS, keys), F32)
        for c in range(n_heads // 2):
            kp = jnp.concatenate([k_ref[2 * c], k_ref[2 * c + 1]], axis=-1).astype(BF16)
            s += lax.dot_general(qbd_scr[:, c * pair:(c + 1) * pair], kp, NT_DIMS,
                                 preferred_element_type=F32)
        scores.append(s * scale)
        r = pt_ref[b, n_pages - 1 - (step * pps + i)] % SUBLANE
        sel = jnp.where((sel_c // SUBLANE == sel_h) & (sel_c % SUBLANE == r), 1.0, 0.0).astype(BF16)
        lft = sum(jnp.dot(sel, part, preferred_element_type=F32)
                  for part in _split3(lf_ref[...].reshape(n_heads * SUBLANE, keys)))
        lfts.append(lft)
        incls.append(_dot_exact01(lft, suffix))

    carry = carry_scr[...]
    logits = []
    for i in range(pps):
        logits.append(scores[i] + (carry + incls[i] - lfts[i]))
        carry = carry + incls[i][:, 0:1]
    carry_scr[...] = carry
    logits = jnp.concatenate(logits, axis=-1)

    m_old = m_scr[...]
    m_new = jnp.maximum(m_old, jnp.max(logits, axis=-1, keepdims=True))
    alpha = jnp.exp(m_old - m_new)
    pw = jnp.exp(logits - m_new)
    l_scr[...] = alpha * l_scr[...] + jnp.sum(pw, axis=-1, keepdims=True)
    m_scr[...] = m_new
    pb = pw.astype(BF16)
    for c in range(n_heads // 2):
        vp = jnp.concatenate(
            [jnp.concatenate([kv_refs[2 * i + 1][2 * c], kv_refs[2 * i + 1][2 * c + 1]], axis=-1)
             for i in range(pps)], axis=0).astype(BF16)
        sl = slice(c * pair, (c + 1) * pair)
        acc_scr[:, sl] = alpha * acc_scr[:, sl] + jnp.dot(pb, vp, preferred_element_type=F32)

    @pl.when(step == n_pages // pps - 1)
    def _():
        o_ref[...] = jnp.sum(jnp.where(diag, acc_scr[...] / l_scr[...], 0.0), axis=0, keepdims=True)


DEC_PAGES_PER_STEP = 8


def _dec_fox(q, k_new, v_new, lf_new_col, cache_k, cache_v, cache_lf, page_table, layer):
    b, _, w = q.shape
    n_pages = page_table.shape[1]
    n_heads, page = cache_k.shape[1:3]
    n_pool = cache_lf.shape[1]
    pps = DEC_PAGES_PER_STEP
    assert n_pages % pps == 0 and n_heads % 2 == 0 and n_pool % SUBLANE == 0

    def page_id(bb, p, pt, i):
        return pt[bb, n_pages - 1 - (p * pps + i)]

    row_map = lambda bb, p, pt: (bb, 0, 0)
    row_spec = pl.BlockSpec((None, 1, w), row_map)
    kv_specs, lf_specs = [], []
    for i in range(pps):
        kv_map = lambda bb, p, pt, i=i: (page_id(bb, p, pt, i) + layer * n_pool, 0, 0, 0)
        kv_specs += [pl.BlockSpec((None, n_heads, page, LANE), kv_map)] * 2
        lf_specs.append(pl.BlockSpec((n_heads, SUBLANE, page),
                                     lambda bb, p, pt, i=i: (layer, page_id(bb, p, pt, i) // SUBLANE, 0)))
    kern = functools.partial(_dec_fox_kernel, scale=LANE ** -0.5 / Q_PRESCALE, n_pages=n_pages, pps=pps)
    grid_spec = pltpu.PrefetchScalarGridSpec(
        num_scalar_prefetch=1, grid=(b, n_pages // pps),
        in_specs=[row_spec, row_spec, row_spec, pl.BlockSpec((None, HEAD_ROWS, 1), row_map)]
        + kv_specs + lf_specs,
        out_specs=row_spec,
        scratch_shapes=[pltpu.VMEM((HEAD_ROWS, w), BF16), pltpu.VMEM((HEAD_ROWS, 1), F32),
                        pltpu.VMEM((HEAD_ROWS, 1), F32), pltpu.VMEM((HEAD_ROWS, w), F32),
                        pltpu.VMEM((HEAD_ROWS, 1), F32)])
    return pl.pallas_call(
        kern, grid_spec=grid_spec, out_shape=jax.ShapeDtypeStruct((b, 1, w), F32),
        compiler_params=_params(("arbitrary", "arbitrary")), name="fox_decode_attn")(
            page_table, q, k_new, v_new, lf_new_col, *([cache_k, cache_v] * pps), *([cache_lf] * pps))


def _dec_xattn_kernel(q_ref, k_ref, v_ref, o_ref, *, scale, n_heads):
    for b in range(q_ref.shape[0]):
        s = lax.dot_general(q_ref[b].astype(BF16), k_ref[b].astype(BF16), NT_DIMS,
                            preferred_element_type=F32) * scale
        row = lax.broadcasted_iota(jnp.int32, s.shape, 0)
        col = lax.broadcasted_iota(jnp.int32, s.shape, 1)
        logits = jnp.where(col % n_heads == row, s, NEG_INF)
        p = jnp.exp(logits - jnp.max(logits, axis=-1, keepdims=True))
        p = p / jnp.sum(p, axis=-1, keepdims=True)
        o_ref[b] = jnp.dot(p.astype(BF16), v_ref[b].astype(BF16), preferred_element_type=F32)


DEC_XATTN_SEQS_PER_STEP = 4


def _dec_xattn(q, mem_k, mem_v, layer, n_heads):
    b = q.shape[0]
    n_rows = mem_k.shape[2]
    spb = DEC_XATTN_SEQS_PER_STEP
    assert b % spb == 0
    kern = functools.partial(_dec_xattn_kernel, scale=LANE ** -0.5, n_heads=n_heads)
    head_spec = pl.BlockSpec((spb, HEAD_ROWS, LANE), lambda bb: (bb, 0, 0))
    mem_spec = pl.BlockSpec((None, spb, n_rows, LANE), lambda bb: (layer, bb, 0, 0))
    return pl.pallas_call(
        kern, grid=(b // spb,), in_specs=[head_spec, mem_spec, mem_spec], out_specs=head_spec,
        out_shape=jax.ShapeDtypeStruct((b, HEAD_ROWS, LANE), F32),
        compiler_params=_params(("arbitrary",)), name="mem_xattn_decode")(q, mem_k, mem_v)


def _outproj_kernel(a_ref, xa_ref, w_ref, x_ref, o_ref):
    lhs = jnp.concatenate([a_ref[...].astype(BF16), xa_ref[...].astype(BF16)], axis=-1)
    o_ref[...] = x_ref[...] + jnp.dot(lhs, w_ref[...], preferred_element_type=F32)


def _outproj(a, xa, w, layer, x, *, tm, tn):
    m, d = x.shape
    ka, kx = a.shape[1], xa.shape[1]
    return pl.pallas_call(
        _outproj_kernel, grid=(m // tm, d // tn),
        in_specs=[pl.BlockSpec((tm, ka), lambda i, j: (i, 0)),
                  pl.BlockSpec((tm, kx), lambda i, j: (i, 0)),
                  pl.BlockSpec((None, ka + kx, tn), lambda i, j: (layer, 0, j)),
                  pl.BlockSpec((tm, tn), lambda i, j: (i, j))],
        out_specs=pl.BlockSpec((tm, tn), lambda i, j: (i, j)),
        out_shape=jax.ShapeDtypeStruct((m, d), F32),
        compiler_params=_params(("arbitrary", "arbitrary")), name="out_proj")(a, xa, w, x)


def _ffn_kernel(x_ref, g_ref, wg_ref, wu_ref, wd_ref, o_ref, h_scr):
    f = pl.program_id(1)

    @pl.when(f == 0)
    def _():
        x = x_ref[...]
        h = x * lax.rsqrt(jnp.mean(x * x, axis=-1, keepdims=True) + EPS) * g_ref[...]
        h_scr[...] = h.astype(BF16)
        o_ref[...] = x

    h = h_scr[...]
    gate = jnp.dot(h, wg_ref[...], preferred_element_type=F32)
    up = jnp.dot(h, wu_ref[...], preferred_element_type=F32)
    act = (gate * jax.nn.sigmoid(gate)) * up
    o_ref[...] += jnp.dot(act.astype(BF16), wd_ref[...], preferred_element_type=F32)


def _ffn(x, g, w_gate, w_up, w_down, layer, *, tm, tf):
    m, d = x.shape
    d_ff = w_gate.shape[-1]
    return pl.pallas_call(
        _ffn_kernel, grid=(m // tm, d_ff // tf),
        in_specs=[pl.BlockSpec((tm, d), lambda i, f: (i, 0)),
                  pl.BlockSpec((1, d), lambda i, f: (0, 0)),
                  pl.BlockSpec((None, d, tf), lambda i, f: (layer, 0, f)),
                  pl.BlockSpec((None, d, tf), lambda i, f: (layer, 0, f)),
                  pl.BlockSpec((None, tf, d), lambda i, f: (layer, f, 0))],
        out_specs=pl.BlockSpec((tm, d), lambda i, f: (i, 0)),
        out_shape=jax.ShapeDtypeStruct((m, d), F32),
        scratch_shapes=[pltpu.VMEM((tm, d), BF16)],
        compiler_params=_params(("arbitrary", "arbitrary"), VMEM_LIMIT_FFN), name="swiglu_ffn")(
            x, g.reshape(1, d), w_gate, w_up, w_down)


S5_CHUNK = 2 * LANE


def _gelu_tanh(y):
    return 0.5 * y * (1.0 + jnp.tanh(math.sqrt(2.0 / math.pi) * (y + 0.044715 * (y * y * y))))


def _s5_prompt_kernel(u_ref, bbre_ref, bbim_ref, cre_ref, cimn_ref, are_ref, aim_ref, d_ref,
                      wglu_ref, bglu_ref, mix_ref, hre_ref, him_ref,
                      sre, sim, hre_s, him_s, y_scr, *, lt, n_chunks):
    t = pl.program_id(1)
    tiles_per_chunk = S5_CHUNK // LANE
    chunks_per_row = are_ref.shape[0] // tiles_per_chunk
    chunks_per_tile = S5_CHUNK // (u_ref.shape[1] // n_chunks)

    @pl.when(t == 0)
    def _():
        hre_s[...] = jnp.zeros_like(hre_s)
        him_s[...] = jnp.zeros_like(him_s)

    u = u_ref[...]
    ub = u.astype(BF16)

    def put(dst, q, val):
        j, lc = divmod(q, chunks_per_row)
        for k in range(tiles_per_chunk):
            dst[lc * tiles_per_chunk + k, pl.ds(j, lt, stride=SUBLANE), :] = val[:, k * LANE:(k + 1) * LANE]

    def get(src, q):
        j, lc = divmod(q, chunks_per_row)
        return jnp.concatenate([src[lc * tiles_per_chunk + k, pl.ds(j, lt, stride=SUBLANE), :]
                                for k in range(tiles_per_chunk)], axis=-1)

    for q in range(n_chunks):
        ut = ub[:, (q // 2) * LANE:(q // 2 + 1) * LANE]
        put(sre, q, jnp.dot(ut, bbre_ref[q], preferred_element_type=F32))
        put(sim, q, jnp.dot(ut, bbim_ref[q], preferred_element_type=F32))

    a_re = are_ref[...]
    a_im = aim_ref[...]

    def body(tt, carry):
        h_re, h_im = carry
        r0 = pl.multiple_of(tt * SUBLANE, SUBLANE)
        n_re = a_re * h_re - a_im * h_im + sre[:, pl.ds(r0, SUBLANE), :]
        n_im = a_re * h_im + a_im * h_re + sim[:, pl.ds(r0, SUBLANE), :]
        sre[:, pl.ds(r0, SUBLANE), :] = n_re
        sim[:, pl.ds(r0, SUBLANE), :] = n_im
        return n_re, n_im

    h_re, h_im = lax.fori_loop(0, lt, body, (hre_s[...], him_s[...]), unroll=8)
    hre_s[...] = h_re
    him_s[...] = h_im
    hre_ref[...] = h_re
    him_ref[...] = h_im

    for c in range(n_chunks // chunks_per_tile):
        acc = jnp.zeros((lt, S5_CHUNK), F32)
        for q in range(c * chunks_per_tile, (c + 1) * chunks_per_tile):
            acc += jnp.dot(get(sre, q).astype(BF16), cre_ref[q], preferred_element_type=F32)
            acc += jnp.dot(get(sim, q).astype(BF16), cimn_ref[q], preferred_element_type=F32)
        sl = slice(c * S5_CHUNK, (c + 1) * S5_CHUNK)
        y_scr[:, sl] = _gelu_tanh(acc + d_ref[:, sl] * u[:, sl])

    y = y_scr[...]
    gate = jax.nn.sigmoid(jnp.dot(y.astype(BF16), wglu_ref[...], preferred_element_type=F32)
                          + bglu_ref[...])
    mix_ref[...] = (y * gate).astype(mix_ref.dtype)


def _s5_prompt(u, pk, w_glu, b_glu, *, lt):
    b, t, w = u.shape
    n_chunks = pk["bb_re"].shape[0]
    st_shape = pk["a_re8"].shape
    full = lambda a: pl.BlockSpec(a.shape, lambda bb, tt, nd=a.ndim: (0,) * nd)
    consts = [pk["bb_re"], pk["bb_im"], pk["c_re"], pk["c_imn"], pk["a_re8"], pk["a_im8"], pk["d_row"],
              w_glu, b_glu]
    kern = functools.partial(_s5_prompt_kernel, lt=lt, n_chunks=n_chunks)
    st_spec = pl.BlockSpec((None,) + st_shape, lambda bb, tt: (bb, 0, 0, 0))
    return pl.pallas_call(
        kern, grid=(b, t // lt),
        in_specs=[pl.BlockSpec((None, lt, w), lambda bb, tt: (bb, tt, 0))] + [full(a) for a in consts],
        out_specs=[pl.BlockSpec((None, lt, w), lambda bb, tt: (bb, tt, 0)), st_spec, st_spec],
        out_shape=[jax.ShapeDtypeStruct((b, t, w), BF16),
                   jax.ShapeDtypeStruct((b,) + st_shape, F32),
                   jax.ShapeDtypeStruct((b,) + st_shape, F32)],
        scratch_shapes=[pltpu.VMEM((st_shape[0], lt * SUBLANE, LANE), F32),
                        pltpu.VMEM((st_shape[0], lt * SUBLANE, LANE), F32),
                        pltpu.VMEM(st_shape, F32), pltpu.VMEM(st_shape, F32),
                        pltpu.VMEM((lt, w), F32)],
        compiler_params=_params(("arbitrary", "arbitrary")), name="s5_prompt")(u, *consts)


def _s5_sample_kernel(u_ref, h0re_ref, h0im_ref, bbre_ref, bbim_ref, cre_ref, cimn_ref, are_ref, aim_ref,
                      d_ref, wglu_ref, bglu_ref, mix_ref, hre_ref, him_ref, *, n_chunks):
    u = u_ref[...]
    ub = u.astype(BF16)
    chunks_per_tile = S5_CHUNK // (u.shape[1] // n_chunks)
    ys = []
    for c in range(n_chunks // chunks_per_tile):
        acc = jnp.zeros((u.shape[0], S5_CHUNK), F32)
        for q in range(c * chunks_per_tile, (c + 1) * chunks_per_tile):
            ut = ub[:, (q // 2) * LANE:(q // 2 + 1) * LANE]
            sl = slice(q * S5_CHUNK, (q + 1) * S5_CHUNK)
            a_re, a_im = are_ref[:, sl], aim_ref[:, sl]
            h_re, h_im = h0re_ref[:, sl], h0im_ref[:, sl]
            n_re = a_re * h_re - a_im * h_im + jnp.dot(ut, bbre_ref[q], preferred_element_type=F32)
            n_im = a_re * h_im + a_im * h_re + jnp.dot(ut, bbim_ref[q], preferred_element_type=F32)
            hre_ref[:, sl] = n_re
            him_ref[:, sl] = n_im
            acc += jnp.dot(n_re.astype(BF16), cre_ref[q], preferred_element_type=F32)
            acc += jnp.dot(n_im.astype(BF16), cimn_ref[q], preferred_element_type=F32)
        cs = slice(c * S5_CHUNK, (c + 1) * S5_CHUNK)
        ys.append(_gelu_tanh(acc + d_ref[:, cs] * u[:, cs]))
    y = jnp.concatenate(ys, axis=-1)
    gate = jax.nn.sigmoid(jnp.dot(y.astype(BF16), wglu_ref[...], preferred_element_type=F32)
                          + bglu_ref[...])
    mix_ref[...] = (y * gate).astype(mix_ref.dtype)


def _s5_sample(u, h0_re, h0_im, pk, w_glu, b_glu):
    b, w = u.shape
    n_state = h0_re.shape[1]
    args = [u, h0_re, h0_im, pk["bb_re"], pk["bb_im"], pk["c_re"], pk["c_imn"], pk["a_re1"], pk["a_im1"],
            pk["d_row"], w_glu, b_glu]
    kern = functools.partial(_s5_sample_kernel, n_chunks=pk["bb_re"].shape[0])
    return pl.pallas_call(
        kern,
        out_shape=[jax.ShapeDtypeStruct((b, w), BF16), jax.ShapeDtypeStruct((b, n_state), F32),
                   jax.ShapeDtypeStruct((b, n_state), F32)],
        compiler_params=pltpu.CompilerParams(vmem_limit_bytes=VMEM_LIMIT), name="s5_sample")(*args)


def _state_tiles(a):
    return a.reshape(SUBLANE, -1, LANE).transpose(1, 0, 2)


def _state_untile(h, g, n):
    return h.transpose(0, 2, 1, 3).reshape(h.shape[0], g, n)


def _s5_pack(a_re, a_im, log_dt, b_re, b_im, c_re, c_im, d):
    g, n = a_re.shape
    p = d.shape[1]
    dt = jnp.exp(log_dt)[:, None]
    mag = jnp.exp(a_re * dt)
    ab_re = mag * jnp.cos(a_im * dt)
    ab_im = mag * jnp.sin(a_im * dt)
    num_re, num_im = ab_re - 1.0, ab_im
    den = a_re * a_re + a_im * a_im
    z_re = (num_re * a_re + num_im * a_im) / den
    z_im = (num_im * a_re - num_re * a_im) / den
    bb_re = z_re[..., None] * b_re - z_im[..., None] * b_im
    bb_im = z_re[..., None] * b_im + z_im[..., None] * b_re
    gpc = S5_CHUNK // n
    n_chunks = g // gpc
    eye = jnp.eye(gpc, dtype=F32)
    chunks_per_in_tile = LANE // (gpc * p)
    chunks_per_out_tile = S5_CHUNK // (gpc * p)

    def pack_in(bb):
        blk = jnp.einsum("qgpn,gh->qgphn", bb.transpose(0, 2, 1).reshape(n_chunks, gpc, p, n), eye)
        blk = blk.reshape(n_chunks, gpc * p, S5_CHUNK)
        sel = jax.nn.one_hot(jnp.arange(n_chunks) % chunks_per_in_tile, chunks_per_in_tile, dtype=F32)
        return jnp.einsum("qrc,qs->qsrc", blk, sel).reshape(n_chunks, LANE, S5_CHUNK).astype(BF16)

    def pack_out(cc):
        blk = jnp.einsum("qgnp,gh->qgnhp", cc.transpose(0, 2, 1).reshape(n_chunks, gpc, n, p), eye)
        blk = blk.reshape(n_chunks, S5_CHUNK, gpc * p)
        sel = jax.nn.one_hot(jnp.arange(n_chunks) % chunks_per_out_tile, chunks_per_out_tile, dtype=F32)
        return jnp.einsum("qrc,qs->qrsc", blk, sel).reshape(n_chunks, S5_CHUNK, S5_CHUNK).astype(BF16)

    return {
        "bb_re": pack_in(bb_re), "bb_im": pack_in(bb_im),
        "c_re": pack_out(c_re), "c_imn": pack_out(-c_im),
        "a_re8": _state_tiles(ab_re), "a_im8": _state_tiles(ab_im),
        "a_re1": ab_re.reshape(1, g * n), "a_im1": ab_im.reshape(1, g * n),
        "d_row": d.reshape(1, g * p),
    }


TM = 512
TN = 512
TM_FFN = 1024
TF = 512
TQ = 512
S5_LT = 256


def _gain_row(*pieces):
    return jnp.concatenate([jnp.tile(g, reps) for g, reps in pieces]).reshape(1, -1).astype(F32)


def _head_rows(x, n_heads):
    x = x.reshape(x.shape[0], n_heads, LANE)
    return jnp.pad(x, ((0, 0), (0, HEAD_ROWS - n_heads), (0, 0)))


def kernel(x_prompt, x_sample, cache_fox_k, cache_fox_v, cache_fox_logf, state_s5_re, state_s5_im, cache_mem_k, cache_mem_v, page_table, mem_prompt, norm1_g, w_out, mem_norm_g, w_mem_kv, xq_norm_g, xk_norm_g, norm2_g, w_ffn_gate, w_ffn_up, w_ffn_down, fox_w_in, fox_b_f, fox_q_norm_g, fox_k_norm_g, s5_w_in, s5_a_re, s5_a_im, s5_log_dt, s5_b_re, s5_b_im, s5_c_re, s5_c_im, s5_d, s5_w_glu, s5_b_glu):
    bp, tp, d = x_prompt.shape
    bs, ts, _ = x_sample.shape
    depth = norm1_g.shape[0]
    n_fox, n_pool, page, fox_h, fox_hd = cache_fox_k.shape
    fox_w = fox_h * fox_hd
    n_mem, xa_h, xa_hd = cache_mem_k.shape[2:]
    xa_w = xa_h * xa_hd
    s5_g, s5_n = s5_a_re.shape[1:]
    mp, ms = bp * tp, bs * ts
    assert ts == 1 and fox_hd == LANE and xa_hd == LANE

    xp = x_prompt.reshape(mp, d)
    xs = x_sample.reshape(ms, d)
    mem = mem_prompt.reshape(bp * n_mem, d)
    ck = cache_fox_k.transpose(0, 1, 3, 2, 4).reshape(n_fox * n_pool, fox_h, page, fox_hd)
    cv = cache_fox_v.transpose(0, 1, 3, 2, 4).reshape(n_fox * n_pool, fox_h, page, fox_hd)
    clf = cache_fox_logf.transpose(0, 3, 1, 2).reshape(n_fox * fox_h, n_pool, page)
    cmk = cache_mem_k.reshape(depth, bs, n_mem * xa_h, xa_hd)
    cmv = cache_mem_v.reshape(depth, bs, n_mem * xa_h, xa_hd)

    outs = {k: [] for k in ("fk_p", "fv_p", "fl_p", "fk_s", "fv_s", "fl_s",
                            "sr_p", "si_p", "sr_s", "si_s", "mk_p", "mv_p")}
    fox_nt, xa_nt = fox_w // TN, xa_w // TN

    wo16 = w_out.astype(BF16)
    wkv16 = w_mem_kv.astype(BF16)
    wg16, wu16, wd16 = w_ffn_gate.astype(BF16), w_ffn_up.astype(BF16), w_ffn_down.astype(BF16)

    for i in range(depth):
        j = i // 2

        gain = _gain_row((xk_norm_g[i], xa_h), (jnp.ones((xa_hd,), F32), xa_h))
        segs = ((0, xa_nt, True, (0, 1)), (xa_nt, xa_nt, False, (2, 3)))
        mk32, mk16, mv32, mv16 = _proj(mem, mem_norm_g[i], wkv16, gain, segs,
                                       (F32, BF16, F32, BF16), tm=bp * n_mem, tn=TN, layer=i)
        outs["mk_p"].append(mk32.reshape(bp, n_mem, xa_h, xa_hd))
        outs["mv_p"].append(mv32.reshape(bp, n_mem, xa_h, xa_hd))

        if i % 2 == 0:
            w_in = fox_w_in[j]
            w_main = jnp.concatenate([w_in[:, :3 * fox_w], w_in[:, 3 * fox_w + fox_h:]], axis=1).astype(BF16)
            wf = jnp.pad(w_in[:, 3 * fox_w:3 * fox_w + fox_h], ((0, 0), (0, LANE - fox_h))).astype(BF16)
            bf = jnp.pad(fox_b_f[j], (0, LANE - fox_h)).reshape(1, LANE)
            gain = _gain_row((fox_q_norm_g[j] * Q_PRESCALE, fox_h), (fox_k_norm_g[j], fox_h),
                             (jnp.ones((fox_hd,), F32), fox_h), (xq_norm_g[i], xa_h))
            segs = ((0, fox_nt, True, (0,)), (fox_nt, fox_nt, True, (1, 2)),
                    (2 * fox_nt, fox_nt, False, (3, 4)), (3 * fox_nt, xa_nt, True, (5,)))
            dts = (BF16, F32, BF16, F32, BF16, BF16)
            q16, k32, k16, v32, v16, xq16, lf_p, qa, ka = _proj(
                xp, norm1_g[i], w_main, gain, segs, dts, tm=TM, tn=TN, wf=wf, bf=bf, seq_len=tp,
                n_heads=fox_h)
            outs["fk_p"].append(k32.reshape(mp // page, page, fox_h, fox_hd))
            outs["fv_p"].append(v32.reshape(mp // page, page, fox_h, fox_hd))
            outs["fl_p"].append(lf_p[:, :fox_h].reshape(mp // page, page, fox_h))
            mix_p = _flash(q16.reshape(bp, tp, fox_w), qa.reshape(bp, tp, LANE), k16.reshape(bp, tp, fox_w),
                           ka.reshape(bp, tp, LANE), v16.reshape(bp, tp, fox_w), tq=TQ).reshape(mp, fox_w)

            dts_s = (F32, F32, F32, F32)
            segs_s = ((0, fox_nt, True, (0,)), (fox_nt, fox_nt, True, (1,)),
                      (2 * fox_nt, fox_nt, False, (2,)), (3 * fox_nt, xa_nt, True, (3,)))
            qs, ks, vs, xqs, lf_s = _proj(xs, norm1_g[i], w_main, gain, segs_s, dts_s,
                                          tm=ms, tn=TN, wf=wf, bf=bf)
            outs["fk_s"].append(ks.reshape(bs, ts, fox_h, fox_hd))
            outs["fv_s"].append(vs.reshape(bs, ts, fox_h, fox_hd))
            outs["fl_s"].append(lf_s[:, :fox_h].reshape(bs, ts, fox_h))
            lf_col = lf_s[:, :HEAD_ROWS].reshape(bs, HEAD_ROWS, 1)
            mix_s = _dec_fox(qs.reshape(bs, 1, fox_w), ks.reshape(bs, 1, fox_w), vs.reshape(bs, 1, fox_w),
                             lf_col, ck, cv, clf, page_table, j).reshape(ms, fox_w)
        else:
            w_main = s5_w_in[j].astype(BF16)
            s5_nt = (w_main.shape[1] - xa_w) // TN
            gain = _gain_row((jnp.ones((LANE,), F32), s5_nt * TN // LANE), (xq_norm_g[i], xa_h))
            segs = ((0, s5_nt, False, (0,)), (s5_nt, xa_nt, True, (1,)))
            pk = _s5_pack(s5_a_re[j], s5_a_im[j], s5_log_dt[j], s5_b_re[j], s5_b_im[j],
                          s5_c_re[j], s5_c_im[j], s5_d[j])
            w_glu = s5_w_glu[j].astype(BF16)
            b_glu = s5_b_glu[j].reshape(1, -1)
            u_p, xq16 = _proj(xp, norm1_g[i], w_main, gain, segs, (F32, BF16), tm=TM, tn=TN)
            s5_w = u_p.shape[1]
            mix_p, hr, hi = _s5_prompt(u_p.reshape(bp, tp, s5_w), pk, w_glu, b_glu, lt=S5_LT)
            mix_p = mix_p.reshape(mp, s5_w)
            outs["sr_p"].append(_state_untile(hr, s5_g, s5_n))
            outs["si_p"].append(_state_untile(hi, s5_g, s5_n))
            u_s, xqs = _proj(xs, norm1_g[i], w_main, gain, segs, (F32, F32), tm=ms, tn=TN)
            mix_s, hr, hi = _s5_sample(u_s, state_s5_re[j].reshape(bs, s5_g * s5_n),
                                       state_s5_im[j].reshape(bs, s5_g * s5_n), pk, w_glu, b_glu)
            outs["sr_s"].append(hr.reshape(bs, s5_g, s5_n))
            outs["si_s"].append(hi.reshape(bs, s5_g, s5_n))

        xa_p = _xattn(xq16.reshape(bp, tp, xa_w), mk16.reshape(bp, n_mem, xa_w),
                      mv16.reshape(bp, n_mem, xa_w), tq=TQ).reshape(mp, xa_w)
        xa_s = _dec_xattn(_head_rows(xqs, xa_h), cmk, cmv, i, xa_h)[:, :xa_h].reshape(ms, xa_w)

        xp = _outproj(mix_p, xa_p, wo16, i, xp, tm=TM, tn=d)
        xs = _outproj(mix_s, xa_s, wo16, i, xs, tm=ms, tn=d)
        xp = _ffn(xp, norm2_g[i], wg16, wu16, wd16, i, tm=TM_FFN, tf=TF)
        xs = _ffn(xs, norm2_g[i], wg16, wu16, wd16, i, tm=ms, tf=TF)

    st = lambda k: jnp.stack(outs[k])
    return (xp.reshape(bp, tp, d), xs.reshape(bs, ts, d),
            st("fk_p"), st("fv_p"), st("fl_p"), st("fk_s"), st("fv_s"), st("fl_s"),
            st("sr_p"), st("si_p"), st("sr_s"), st("si_s"), st("mk_p"), st("mv_p"))
```

```python
import functools
import math

import jax
import jax.numpy as jnp
import numpy as np
from jax import lax
from jax.experimental import pallas as pl
from jax.experimental.pallas import tpu as pltpu

F32 = jnp.float32
BF16 = jnp.bfloat16

EPS = 1e-6
NEG_INF = -1e30
LANE = 128
SUBLANE = 8
HEAD_ROWS = 16
VMEM_LIMIT = 52 * 1024 * 1024
VMEM_LIMIT_FFN = 58 * 1024 * 1024
NT_DIMS = (((1,), (1,)), ((), ()))


def _params(sem, vmem=VMEM_LIMIT):
    return pltpu.CompilerParams(dimension_semantics=sem, vmem_limit_bytes=vmem)


def _split3(x):
    hi = x.astype(BF16)
    r1 = x - hi.astype(F32)
    mid = r1.astype(BF16)
    lo = (r1 - mid.astype(F32)).astype(BF16)
    return hi, mid, lo


def _dot_exact01(x, w01):
    r = jnp.dot(jnp.concatenate(_split3(x), axis=0), w01, preferred_element_type=F32)
    return r[0:HEAD_ROWS] + r[HEAD_ROWS:2 * HEAD_ROWS] + r[2 * HEAD_ROWS:3 * HEAD_ROWS]


def _log_sigmoid(x):
    return jnp.minimum(x, 0.0) - jnp.log1p(jnp.exp(-jnp.abs(x)))


def _head_rmsnorm(y, gain_row):
    parts = []
    for c in range(y.shape[-1] // LANE):
        p = y[:, c * LANE:(c + 1) * LANE]
        parts.append(p * lax.rsqrt(jnp.mean(p * p, axis=-1, keepdims=True) + EPS))
    return jnp.concatenate(parts, axis=-1) * gain_row


AUG_LANES = 8
SQRT2 = math.sqrt(2.0)
Q_PRESCALE = 0.125


def _lane_split3(x):
    return jnp.concatenate(_split3(x), axis=-1)


def _lane_sum3(r):
    return r[:, 0:LANE] + r[:, LANE:2 * LANE] + r[:, 2 * LANE:3 * LANE]


def _proj_kernel(*refs, segs, n_out, with_f, cumsum, seq_tiles):
    x_ref, g_ref, w_ref, gain_ref = refs[:4]
    pos = 4
    if with_f:
        wf_ref, bf_ref = refs[4:6]
        pos = 6
        if cumsum:
            wqa_ref, wka_ref, oneq_ref, onek_ref = refs[6:10]
            pos = 10
    out_refs = refs[pos:pos + n_out]
    pos += n_out
    if with_f:
        lf_ref = refs[pos]
        pos += 1
        if cumsum:
            qa_ref, ka_ref = refs[pos:pos + 2]
            pos += 2
    h_scr = refs[pos]
    pos += 1
    if with_f and cumsum:
        tri_scr, carry_scr = refs[pos:pos + 2]

    i = pl.program_id(0)
    j = pl.program_id(1)
    tm = x_ref.shape[0]

    if with_f and cumsum:
        @pl.when((i == 0) & (j == 0))
        def _():
            t = lax.broadcasted_iota(jnp.int32, (tm, tm), 0)
            u = lax.broadcasted_iota(jnp.int32, (tm, tm), 1)
            tri_scr[...] = jnp.where(u <= t, 1.0, 0.0).astype(BF16)

    @pl.when(j == 0)
    def _():
        x = x_ref[...]
        h = x * lax.rsqrt(jnp.mean(x * x, axis=-1, keepdims=True) + EPS) * g_ref[...]
        hb = h.astype(BF16)
        h_scr[...] = hb
        if with_f:
            lf = _log_sigmoid(jnp.dot(hb, wf_ref[...], preferred_element_type=F32) + bf_ref[...])
            lf_ref[...] = lf
            if cumsum:
                @pl.when(i % seq_tiles == 0)
                def _():
                    carry_scr[...] = jnp.zeros_like(carry_scr)
                c = _lane_sum3(jnp.dot(tri_scr[...], _lane_split3(lf), preferred_element_type=F32))
                c = c + carry_scr[...]
                carry_scr[...] = c[tm - 1:tm, :]
                pieces = _lane_split3(c * SQRT2)
                qa_ref[...] = (jnp.dot(pieces, wqa_ref[...], preferred_element_type=F32)
                               + oneq_ref[...]).astype(qa_ref.dtype)
                ka_ref[...] = (jnp.dot(pieces, wka_ref[...], preferred_element_type=F32)
                               + onek_ref[...]).astype(ka_ref.dtype)

    n_sub = 2 if tm % (2 * HEAD_ROWS) == 0 and tm >= 2 * LANE else 1
    rows = tm // n_sub
    for start, n_tiles, norm, outs in segs:
        @pl.when((j >= start) & (j < start + n_tiles))
        def _(norm=norm, outs=outs):
            for r in range(n_sub):
                rs = slice(r * rows, (r + 1) * rows)
                y = jnp.dot(h_scr[rs, :], w_ref[...], preferred_element_type=F32)
                yy = _head_rmsnorm(y, gain_ref[...]) if norm else y
                for o in outs:
                    if len(out_refs[o].shape) == 4:
                        page = out_refs[o].shape[2]
                        for pg in range(rows // page):
                            for hh in range(yy.shape[1] // LANE):
                                out_refs[o][r * (rows // page) + pg, hh] = (
                                    yy[pg * page:(pg + 1) * page, hh * LANE:(hh + 1) * LANE])
                    else:
                        out_refs[o][rs, :] = yy.astype(out_refs[o].dtype)


def _aug_maps(n_heads):
    wqa = np.zeros((3 * LANE, LANE), np.float32)
    wka = np.zeros((3 * LANE, LANE), np.float32)
    oneq = np.zeros((1, LANE), np.float32)
    onek = np.zeros((1, LANE), np.float32)
    for h in range(n_heads):
        for piece in range(3):
            wqa[piece * LANE + h, AUG_LANES * h + piece] = 1.0
            wka[piece * LANE + h, AUG_LANES * h + 3 + piece] = -1.0
            oneq[0, AUG_LANES * h + 3 + piece] = 1.0
            onek[0, AUG_LANES * h + piece] = 1.0
    return jnp.asarray(wqa, BF16), jnp.asarray(wka, BF16), jnp.asarray(oneq), jnp.asarray(onek)


def _proj(x, g, w, gain_row, segs, out_defs, *, tm, tn, layer=None, wf=None, bf=None, seq_len=None,
          n_heads=None):
    m, k = x.shape
    n = w.shape[-1]
    with_f = wf is not None
    cumsum = seq_len is not None
    grid = (m // tm, n // tn)
    const = lambda shape: pl.BlockSpec(shape, lambda i, j: (0,) * len(shape))
    w_spec = (pl.BlockSpec((k, tn), lambda i, j: (0, j)) if layer is None
              else pl.BlockSpec((None, k, tn), lambda i, j: (layer, 0, j)))
    in_specs = [pl.BlockSpec((tm, k), lambda i, j: (i, 0)), const((1, k)), w_spec,
                pl.BlockSpec((1, tn), lambda i, j: (0, j))]
    args = [x, g.reshape(1, k), w, gain_row]
    if with_f:
        in_specs += [const((k, LANE)), const((1, LANE))]
        args += [wf, bf]
        if cumsum:
            maps = _aug_maps(n_heads)
            in_specs += [const(a.shape) for a in maps]
            args += list(maps)
    out_shapes, out_specs = [], []
    for dt, start, n_tiles, page in out_defs:
        if page:
            out_shapes.append(jax.ShapeDtypeStruct((m // page, n_tiles * tn // LANE, page, LANE), dt))
            out_specs.append(pl.BlockSpec(
                (tm // page, tn // LANE, page, LANE),
                lambda i, j, s=start, nt=n_tiles: (i, jnp.clip(j - s, 0, nt - 1), 0, 0)))
        else:
            out_shapes.append(jax.ShapeDtypeStruct((m, n_tiles * tn), dt))
            out_specs.append(pl.BlockSpec(
                (tm, tn), lambda i, j, s=start, nt=n_tiles: (i, jnp.clip(j - s, 0, nt - 1))))
    scratch = [pltpu.VMEM((tm, k), BF16)]
    seq_tiles = 1
    if with_f:
        row_spec = pl.BlockSpec((tm, LANE), lambda i, j: (i, 0))
        out_shapes.append(jax.ShapeDtypeStruct((m, LANE), F32))
        out_specs.append(row_spec)
        if cumsum:
            seq_tiles = seq_len // tm
            out_shapes += [jax.ShapeDtypeStruct((m, LANE), BF16)] * 2
            out_specs += [row_spec] * 2
            scratch += [pltpu.VMEM((tm, tm), BF16), pltpu.VMEM((1, LANE), F32)]
    kern = functools.partial(_proj_kernel, segs=segs, n_out=len(out_defs), with_f=with_f,
                             cumsum=cumsum, seq_tiles=seq_tiles)
    return pl.pallas_call(
        kern, grid=grid, in_specs=in_specs, out_specs=out_specs, out_shape=out_shapes,
        scratch_shapes=scratch, compiler_params=_params(("arbitrary", "arbitrary")),
        name="norm_proj")(*args)


LOG2_SCALE = math.log2(math.e) / SQRT2


def _flash_kernel(qi_ref, kj_ref, q_ref, qa_ref, k_ref, ka_ref, v_ref, o_ref,
                  qaug_scr, m_scr, l_scr, acc_scr):
    hg = pl.program_id(1)
    i = qi_ref[pl.program_id(2)]
    j = kj_ref[pl.program_id(2)]
    tq, tk = q_ref.shape[0], k_ref.shape[0]
    n_hp = q_ref.shape[1] // LANE

    @pl.when(j == 0)
    def _():
        lane = lax.broadcasted_iota(jnp.int32, (tq, LANE), 1)
        for hp in range(n_hp):
            qaug_scr[hp, :, 0:LANE] = q_ref[:, hp * LANE:(hp + 1) * LANE]
            qaug_scr[hp, :, LANE:2 * LANE] = jnp.where(lane // AUG_LANES == hg * n_hp + hp, qa_ref[...],
                                                       jnp.zeros_like(qa_ref))
        m_scr[...] = jnp.full_like(m_scr, NEG_INF)
        l_scr[...] = jnp.zeros_like(l_scr)
        acc_scr[...] = jnp.zeros_like(acc_scr)

    def step(masked):
        for hp in range(n_hp):
            hs = slice(hp * LANE, (hp + 1) * LANE)
            k_aug = jnp.concatenate([k_ref[:, hs], ka_ref[...]], axis=-1)
            x = lax.dot_general(qaug_scr[hp], k_aug, NT_DIMS, preferred_element_type=F32) * LOG2_SCALE
            if masked:
                row = lax.broadcasted_iota(jnp.int32, x.shape, 0)
                col = lax.broadcasted_iota(jnp.int32, x.shape, 1)
                x = jnp.where(col <= row, x, NEG_INF)
            m_old = m_scr[hp]
            m_new = jnp.maximum(m_old, jnp.max(x, axis=-1, keepdims=True))
            alpha = jnp.exp2(m_old - m_new)
            l_part = alpha * l_scr[hp]
            ps = []
            for c in range(tk // LANE):
                p = jnp.exp2(x[:, c * LANE:(c + 1) * LANE] - m_new)
                l_part = l_part + p
                ps.append(p.astype(BF16))
            l_scr[hp] = l_part
            m_scr[hp] = m_new
            acc_scr[hp] = alpha * acc_scr[hp] + jnp.dot(jnp.concatenate(ps, axis=-1), v_ref[:, hs],
                                                        preferred_element_type=F32)

    @pl.when(j < i)
    def _():
        step(False)

    @pl.when(j == i)
    def _():
        step(True)
        for hp in range(n_hp):
            o_ref[:, hp * LANE:(hp + 1) * LANE] = (
                acc_scr[hp] / jnp.sum(l_scr[hp], axis=-1, keepdims=True)).astype(o_ref.dtype)


FLASH_HEADS_PER_STEP = 4


def _flash(q, qa, k, ka, v, *, tq):
    b, t, w = q.shape
    n_hp = FLASH_HEADS_PER_STEP
    hw = n_hp * LANE
    assert w % hw == 0
    nq = t // tq
    pairs = [(i, j) for i in range(nq) for j in range(i + 1)]
    qi = jnp.asarray([p[0] for p in pairs], jnp.int32)
    kj = jnp.asarray([p[1] for p in pairs], jnp.int32)
    q_map = lambda bb, hh, s, qi, kj: (bb, qi[s], hh)
    kv_map = lambda bb, hh, s, qi, kj: (bb, kj[s], hh)
    grid_spec = pltpu.PrefetchScalarGridSpec(
        num_scalar_prefetch=2, grid=(b, w // hw, len(pairs)),
        in_specs=[
            pl.BlockSpec((None, tq, hw), q_map),
            pl.BlockSpec((None, tq, LANE), lambda bb, hh, s, qi, kj: (bb, qi[s], 0)),
            pl.BlockSpec((None, tq, hw), kv_map),
            pl.BlockSpec((None, tq, LANE), lambda bb, hh, s, qi, kj: (bb, kj[s], 0)),
            pl.BlockSpec((None, tq, hw), kv_map),
        ],
        out_specs=pl.BlockSpec((None, tq, hw), q_map),
        scratch_shapes=[pltpu.VMEM((n_hp, tq, 2 * LANE), BF16), pltpu.VMEM((n_hp, tq, LANE), F32),
                        pltpu.VMEM((n_hp, tq, LANE), F32), pltpu.VMEM((n_hp, tq, LANE), F32)])
    return pl.pallas_call(
        _flash_kernel, grid_spec=grid_spec, out_shape=jax.ShapeDtypeStruct((b, t, w), BF16),
        compiler_params=_params(("arbitrary",) * 3), name="fox_prompt_attn")(qi, kj, q, qa, k, ka, v)


def _xattn_kernel(q_ref, k_ref, v_ref, o_ref, *, scale):
    for h in range(q_ref.shape[-1] // LANE):
        sl = slice(h * LANE, (h + 1) * LANE)
        s = lax.dot_general(q_ref[:, sl], k_ref[:, sl], NT_DIMS, preferred_element_type=F32) * scale
        p = jnp.exp(s - jnp.max(s, axis=-1, keepdims=True))
        p = p / jnp.sum(p, axis=-1, keepdims=True)
        o_ref[:, sl] = jnp.dot(p.astype(BF16), v_ref[:, sl],
                               preferred_element_type=F32).astype(o_ref.dtype)


def _xattn(q, k, v, *, tq):
    b, t, w = q.shape
    n_mem = k.shape[1]
    kern = functools.partial(_xattn_kernel, scale=LANE ** -0.5)
    return pl.pallas_call(
        kern, grid=(b, t // tq),
        in_specs=[pl.BlockSpec((None, tq, w), lambda bb, i: (bb, i, 0)),
                  pl.BlockSpec((None, n_mem, w), lambda bb, i: (bb, 0, 0)),
                  pl.BlockSpec((None, n_mem, w), lambda bb, i: (bb, 0, 0))],
        out_specs=pl.BlockSpec((None, tq, w), lambda bb, i: (bb, i, 0)),
        out_shape=jax.ShapeDtypeStruct((b, t, w), BF16),
        compiler_params=_params(("arbitrary", "arbitrary")), name="mem_xattn_prompt")(q, k, v)


def _head_diag(width):
    rows = lax.broadcasted_iota(jnp.int32, (HEAD_ROWS, width), 0)
    lanes = lax.broadcasted_iota(jnp.int32, (HEAD_ROWS, width), 1)
    return (lanes // LANE) == rows


def _dec_fox_kernel(pt_ref, q_ref, kn_ref, vn_ref, lfn_ref, *refs, scale, n_pages, pps):
    kv_refs, lf_refs = refs[:2 * pps], refs[2 * pps:3 * pps]
    o_ref, qbd_scr, m_scr, l_scr, acc_scr, carry_scr = refs[3 * pps:]
    b = pl.program_id(0)
    step = pl.program_id(1)
    n_heads, keys, _ = kv_refs[0].shape
    width = n_heads * LANE
    pair = 2 * LANE
    diag = _head_diag(width)

    @pl.when(step == 0)
    def _():
        qbd = jnp.where(diag, jnp.broadcast_to(q_ref[...], (HEAD_ROWS, width)), 0.0).astype(BF16)
        qbd_scr[...] = qbd
        kn = kn_ref[...].astype(BF16).astype(F32)
        m_scr[...] = jnp.sum(qbd.astype(F32) * kn, axis=-1, keepdims=True) * scale
        l_scr[...] = jnp.ones_like(l_scr)
        acc_scr[...] = jnp.broadcast_to(vn_ref[...].astype(BF16).astype(F32), (HEAD_ROWS, width))
        carry_scr[...] = lfn_ref[...]

    u = lax.broadcasted_iota(jnp.int32, (keys, keys), 0)
    kk = lax.broadcasted_iota(jnp.int32, (keys, keys), 1)
    suffix = jnp.where(u >= kk, 1.0, 0.0).astype(BF16)
    sel_h = lax.broadcasted_iota(jnp.int32, (HEAD_ROWS, n_heads * SUBLANE), 0)
    sel_c = lax.broadcasted_iota(jnp.int32, (HEAD_ROWS, n_heads * SUBLANE), 1)

    scores, incls, lfts = [], [], []
    for i in range(pps):
        k_ref, lf_ref = kv_refs[2 * i], lf_refs[i]
        s = jnp.zeros((HEAD_ROWS, keys), F32)
        for c in range(n_heads // 2):
            kp = jnp.concatenate([k_ref[2 * c], k_ref[2 * c + 1]], axis=-1).astype(BF16)
            s += lax.dot_general(qbd_scr[:, c * pair:(c + 1) * pair], kp, NT_DIMS,
                                 preferred_element_type=F32)
        scores.append(s * scale)
        r = pt_ref[b, n_pages - 1 - (step * pps + i)] % SUBLANE
        sel = jnp.where((sel_c // SUBLANE == sel_h) & (sel_c % SUBLANE == r), 1.0, 0.0).astype(BF16)
        lft = sum(jnp.dot(sel, part, preferred_element_type=F32)
                  for part in _split3(lf_ref[...].reshape(n_heads * SUBLANE, keys)))
        lfts.append(lft)
        incls.append(_dot_exact01(lft, suffix))

    carry = carry_scr[...]
    logits = []
    for i in range(pps):
        logits.append(scores[i] + (carry + incls[i] - lfts[i]))
        carry = carry + incls[i][:, 0:1]
    carry_scr[...] = carry
    logits = jnp.concatenate(logits, axis=-1)

    m_old = m_scr[...]
    m_new = jnp.maximum(m_old, jnp.max(logits, axis=-1, keepdims=True))
    alpha = jnp.exp(m_old - m_new)
    pw = jnp.exp(logits - m_new)
    l_scr[...] = alpha * l_scr[...] + jnp.sum(pw, axis=-1, keepdims=True)
    m_scr[...] = m_new
    pb = pw.astype(BF16)
    for c in range(n_heads // 2):
        vp = jnp.concatenate(
            [jnp.concatenate([kv_refs[2 * i + 1][2 * c], kv_refs[2 * i + 1][2 * c + 1]], axis=-1)
             for i in range(pps)], axis=0).astype(BF16)
        sl = slice(c * pair, (c + 1) * pair)
        acc_scr[:, sl] = alpha * acc_scr[:, sl] + jnp.dot(pb, vp, preferred_element_type=F32)

    @pl.when(step == n_pages // pps - 1)
    def _():
        o_ref[...] = jnp.sum(jnp.where(diag, acc_scr[...] / l_scr[...], 0.0), axis=0, keepdims=True)


DEC_PAGES_PER_STEP = 8


def _dec_fox(q, k_new, v_new, lf_new_col, cache_k, cache_v, cache_lf, page_table, layer):
    b, _, w = q.shape
    n_pages = page_table.shape[1]
    n_heads, page = cache_k.shape[1:3]
    n_pool = cache_lf.shape[1]
    pps = DEC_PAGES_PER_STEP
    assert n_pages % pps == 0 and n_heads % 2 == 0 and n_pool % SUBLANE == 0

    def page_id(bb, p, pt, i):
        return pt[bb, n_pages - 1 - (p * pps + i)]

    row_map = lambda bb, p, pt: (bb, 0, 0)
    row_spec = pl.BlockSpec((None, 1, w), row_map)
    kv_specs, lf_specs = [], []
    for i in range(pps):
        kv_map = lambda bb, p, pt, i=i: (page_id(bb, p, pt, i) + layer * n_pool, 0, 0, 0)
        kv_specs += [pl.BlockSpec((None, n_heads, page, LANE), kv_map)] * 2
        lf_specs.append(pl.BlockSpec((n_heads, SUBLANE, page),
                                     lambda bb, p, pt, i=i: (layer, page_id(bb, p, pt, i) // SUBLANE, 0)))
    kern = functools.partial(_dec_fox_kernel, scale=LANE ** -0.5 / Q_PRESCALE, n_pages=n_pages, pps=pps)
    grid_spec = pltpu.PrefetchScalarGridSpec(
        num_scalar_prefetch=1, grid=(b, n_pages // pps),
        in_specs=[row_spec, row_spec, row_spec, pl.BlockSpec((None, HEAD_ROWS, 1), row_map)]
        + kv_specs + lf_specs,
        out_specs=row_spec,
        scratch_shapes=[pltpu.VMEM((HEAD_ROWS, w), BF16), pltpu.VMEM((HEAD_ROWS, 1), F32),
                        pltpu.VMEM((HEAD_ROWS, 1), F32), pltpu.VMEM((HEAD_ROWS, w), F32),
                        pltpu.VMEM((HEAD_ROWS, 1), F32)])
    return pl.pallas_call(
        kern, grid_spec=grid_spec, out_shape=jax.ShapeDtypeStruct((b, 1, w), F32),
        compiler_params=_params(("arbitrary", "arbitrary")), name="fox_decode_attn")(
            page_table, q, k_new, v_new, lf_new_col, *([cache_k, cache_v] * pps), *([cache_lf] * pps))


def _dec_xattn_kernel(q_ref, k_ref, v_ref, o_ref, *, scale, n_heads):
    for b in range(q_ref.shape[0]):
        s = lax.dot_general(q_ref[b].astype(BF16), k_ref[b].astype(BF16), NT_DIMS,
                            preferred_element_type=F32) * scale
        row = lax.broadcasted_iota(jnp.int32, s.shape, 0)
        col = lax.broadcasted_iota(jnp.int32, s.shape, 1)
        logits = jnp.where(col % n_heads == row, s, NEG_INF)
        p = jnp.exp(logits - jnp.max(logits, axis=-1, keepdims=True))
        p = p / jnp.sum(p, axis=-1, keepdims=True)
        o_ref[b] = jnp.dot(p.astype(BF16), v_ref[b].astype(BF16), preferred_element_type=F32)


DEC_XATTN_SEQS_PER_STEP = 4


def _dec_xattn(q, mem_k, mem_v, layer, n_heads):
    b = q.shape[0]
    n_rows = mem_k.shape[2]
    spb = DEC_XATTN_SEQS_PER_STEP
    assert b % spb == 0
    kern = functools.partial(_dec_xattn_kernel, scale=LANE ** -0.5, n_heads=n_heads)
    head_spec = pl.BlockSpec((spb, HEAD_ROWS, LANE), lambda bb: (bb, 0, 0))
    mem_spec = pl.BlockSpec((None, spb, n_rows, LANE), lambda bb: (layer, bb, 0, 0))
    return pl.pallas_call(
        kern, grid=(b // spb,), in_specs=[head_spec, mem_spec, mem_spec], out_specs=head_spec,
        out_shape=jax.ShapeDtypeStruct((b, HEAD_ROWS, LANE), F32),
        compiler_params=_params(("arbitrary",)), name="mem_xattn_decode")(q, mem_k, mem_v)


def _outproj_kernel(a_ref, xa_ref, w_ref, x_ref, o_ref):
    lhs = jnp.concatenate([a_ref[...].astype(BF16), xa_ref[...].astype(BF16)], axis=-1)
    o_ref[...] = x_ref[...] + jnp.dot(lhs, w_ref[...], preferred_element_type=F32)


def _outproj(a, xa, w, layer, x, *, tm, tn):
    m, d = x.shape
    ka, kx = a.shape[1], xa.shape[1]
    return pl.pallas_call(
        _outproj_kernel, grid=(m // tm, d // tn),
        in_specs=[pl.BlockSpec((tm, ka), lambda i, j: (i, 0)),
                  pl.BlockSpec((tm, kx), lambda i, j: (i, 0)),
                  pl.BlockSpec((None, ka + kx, tn), lambda i, j: (layer, 0, j)),
                  pl.BlockSpec((tm, tn), lambda i, j: (i, j))],
        out_specs=pl.BlockSpec((tm, tn), lambda i, j: (i, j)),
        out_shape=jax.ShapeDtypeStruct((m, d), F32),
        compiler_params=_params(("arbitrary", "arbitrary")), name="out_proj")(a, xa, w, x)


def _ffn_kernel(x_ref, g_ref, wg_ref, wu_ref, wd_ref, o_ref, h_scr):
    f = pl.program_id(1)

    @pl.when(f == 0)
    def _():
        x = x_ref[...]
        h = x * lax.rsqrt(jnp.mean(x * x, axis=-1, keepdims=True) + EPS) * g_ref[...]
        h_scr[...] = h.astype(BF16)
        o_ref[...] = x

    h = h_scr[...]
    gate = jnp.dot(h, wg_ref[...], preferred_element_type=F32)
    up = jnp.dot(h, wu_ref[...], preferred_element_type=F32)
    act = (gate * jax.nn.sigmoid(gate)) * up
    o_ref[...] += jnp.dot(act.astype(BF16), wd_ref[...], preferred_element_type=F32)


def _ffn(x, g, w_gate, w_up, w_down, layer, *, tm, tf):
    m, d = x.shape
    d_ff = w_gate.shape[-1]
    return pl.pallas_call(
        _ffn_kernel, grid=(m // tm, d_ff // tf),
        in_specs=[pl.BlockSpec((tm, d), lambda i, f: (i, 0)),
                  pl.BlockSpec((1, d), lambda i, f: (0, 0)),
                  pl.BlockSpec((None, d, tf), lambda i, f: (layer, 0, f)),
                  pl.BlockSpec((None, d, tf), lambda i, f: (layer, 0, f)),
                  pl.BlockSpec((None, tf, d), lambda i, f: (layer, f, 0))],
        out_specs=pl.BlockSpec((tm, d), lambda i, f: (i, 0)),
        out_shape=jax.ShapeDtypeStruct((m, d), F32),
        scratch_shapes=[pltpu.VMEM((tm, d), BF16)],
        compiler_params=_params(("arbitrary", "arbitrary"), VMEM_LIMIT_FFN), name="swiglu_ffn")(
            x, g.reshape(1, d), w_gate, w_up, w_down)


S5_CHUNK = 2 * LANE


def _gelu_tanh(y):
    return 0.5 * y * (1.0 + jnp.tanh(math.sqrt(2.0 / math.pi) * (y + 0.044715 * (y * y * y))))


def _s5_prompt_kernel(u_ref, bbre_ref, bbim_ref, cre_ref, cimn_ref, are_ref, aim_ref, d_ref,
                      wglu_ref, bglu_ref, mix_ref, hre_ref, him_ref,
                      sre, sim, hre_s, him_s, y_scr, *, lt, n_chunks):
    t = pl.program_id(1)
    tiles_per_chunk = S5_CHUNK // LANE
    chunks_per_row = are_ref.shape[0] // tiles_per_chunk
    chunks_per_tile = S5_CHUNK // (u_ref.shape[1] // n_chunks)

    @pl.when(t == 0)
    def _():
        hre_s[...] = jnp.zeros_like(hre_s)
        him_s[...] = jnp.zeros_like(him_s)

    u = u_ref[...]
    ub = u.astype(BF16)

    def put(dst, q, val):
        j, lc = divmod(q, chunks_per_row)
        for k in range(tiles_per_chunk):
            dst[lc * tiles_per_chunk + k, pl.ds(j, lt, stride=SUBLANE), :] = val[:, k * LANE:(k + 1) * LANE]

    def get(src, q):
        j, lc = divmod(q, chunks_per_row)
        return jnp.concatenate([src[lc * tiles_per_chunk + k, pl.ds(j, lt, stride=SUBLANE), :]
                                for k in range(tiles_per_chunk)], axis=-1)

    for q in range(n_chunks):
        ut = ub[:, (q // 2) * LANE:(q // 2 + 1) * LANE]
        put(sre, q, jnp.dot(ut, bbre_ref[q], preferred_element_type=F32))
        put(sim, q, jnp.dot(ut, bbim_ref[q], preferred_element_type=F32))

    a_re = are_ref[...]
    a_im = aim_ref[...]

    def body(tt, carry):
        h_re, h_im = carry
        r0 = pl.multiple_of(tt * SUBLANE, SUBLANE)
        n_re = a_re * h_re - a_im * h_im + sre[:, pl.ds(r0, SUBLANE), :]
        n_im = a_re * h_im + a_im * h_re + sim[:, pl.ds(r0, SUBLANE), :]
        sre[:, pl.ds(r0, SUBLANE), :] = n_re
        sim[:, pl.ds(r0, SUBLANE), :] = n_im
        return n_re, n_im

    h_re, h_im = lax.fori_loop(0, lt, body, (hre_s[...], him_s[...]), unroll=8)
    hre_s[...] = h_re
    him_s[...] = h_im
    hre_ref[...] = h_re
    him_ref[...] = h_im

    for c in range(n_chunks // chunks_per_tile):
        acc = jnp.zeros((lt, S5_CHUNK), F32)
        for q in range(c * chunks_per_tile, (c + 1) * chunks_per_tile):
            acc += jnp.dot(get(sre, q).astype(BF16), cre_ref[q], preferred_element_type=F32)
            acc += jnp.dot(get(sim, q).astype(BF16), cimn_ref[q], preferred_element_type=F32)
        sl = slice(c * S5_CHUNK, (c + 1) * S5_CHUNK)
        y_scr[:, sl] = _gelu_tanh(acc + d_ref[:, sl] * u[:, sl])

    y = y_scr[...]
    gate = jax.nn.sigmoid(jnp.dot(y.astype(BF16), wglu_ref[...], preferred_element_type=F32)
                          + bglu_ref[...])
    mix_ref[...] = (y * gate).astype(mix_ref.dtype)


def _s5_prompt(u, pk, w_glu, b_glu, *, lt):
    b, t, w = u.shape
    n_chunks = pk["bb_re"].shape[0]
    st_shape = pk["a_re8"].shape
    full = lambda a: pl.BlockSpec(a.shape, lambda bb, tt, nd=a.ndim: (0,) * nd)
    consts = [pk["bb_re"], pk["bb_im"], pk["c_re"], pk["c_imn"], pk["a_re8"], pk["a_im8"], pk["d_row"],
              w_glu, b_glu]
    kern = functools.partial(_s5_prompt_kernel, lt=lt, n_chunks=n_chunks)
    st_spec = pl.BlockSpec((None,) + st_shape, lambda bb, tt: (bb, 0, 0, 0))
    return pl.pallas_call(
        kern, grid=(b, t // lt),
        in_specs=[pl.BlockSpec((None, lt, w), lambda bb, tt: (bb, tt, 0))] + [full(a) for a in consts],
        out_specs=[pl.BlockSpec((None, lt, w), lambda bb, tt: (bb, tt, 0)), st_spec, st_spec],
        out_shape=[jax.ShapeDtypeStruct((b, t, w), BF16),
                   jax.ShapeDtypeStruct((b,) + st_shape, F32),
                   jax.ShapeDtypeStruct((b,) + st_shape, F32)],
        scratch_shapes=[pltpu.VMEM((st_shape[0], lt * SUBLANE, LANE), F32),
                        pltpu.VMEM((st_shape[0], lt * SUBLANE, LANE), F32),
                        pltpu.VMEM(st_shape, F32), pltpu.VMEM(st_shape, F32),
                        pltpu.VMEM((lt, w), F32)],
        compiler_params=_params(("arbitrary", "arbitrary")), name="s5_prompt")(u, *consts)


def _s5_sample_kernel(u_ref, h0re_ref, h0im_ref, bbre_ref, bbim_ref, cre_ref, cimn_ref, are_ref, aim_ref,
                      d_ref, wglu_ref, bglu_ref, mix_ref, hre_ref, him_ref, *, n_chunks):
    u = u_ref[...]
    ub = u.astype(BF16)
    chunks_per_tile = S5_CHUNK // (u.shape[1] // n_chunks)
    ys = []
    for c in range(n_chunks // chunks_per_tile):
        acc = jnp.zeros((u.shape[0], S5_CHUNK), F32)
        for q in range(c * chunks_per_tile, (c + 1) * chunks_per_tile):
            ut = ub[:, (q // 2) * LANE:(q // 2 + 1) * LANE]
            sl = slice(q * S5_CHUNK, (q + 1) * S5_CHUNK)
            a_re, a_im = are_ref[:, sl], aim_ref[:, sl]
            h_re, h_im = h0re_ref[:, sl], h0im_ref[:, sl]
            n_re = a_re * h_re - a_im * h_im + jnp.dot(ut, bbre_ref[q], preferred_element_type=F32)
            n_im = a_re * h_im + a_im * h_re + jnp.dot(ut, bbim_ref[q], preferred_element_type=F32)
            hre_ref[:, sl] = n_re
            him_ref[:, sl] = n_im
            acc += jnp.dot(n_re.astype(BF16), cre_ref[q], preferred_element_type=F32)
            acc += jnp.dot(n_im.astype(BF16), cimn_ref[q], preferred_element_type=F32)
        cs = slice(c * S5_CHUNK, (c + 1) * S5_CHUNK)
        ys.append(_gelu_tanh(acc + d_ref[:, cs] * u[:, cs]))
    y = jnp.concatenate(ys, axis=-1)
    gate = jax.nn.sigmoid(jnp.dot(y.astype(BF16), wglu_ref[...], preferred_element_type=F32)
                          + bglu_ref[...])
    mix_ref[...] = (y * gate).astype(mix_ref.dtype)


def _s5_sample(u, h0_re, h0_im, pk, w_glu, b_glu):
    b, w = u.shape
    n_state = h0_re.shape[1]
    args = [u, h0_re, h0_im, pk["bb_re"], pk["bb_im"], pk["c_re"], pk["c_imn"], pk["a_re1"], pk["a_im1"],
            pk["d_row"], w_glu, b_glu]
    kern = functools.partial(_s5_sample_kernel, n_chunks=pk["bb_re"].shape[0])
    return pl.pallas_call(
        kern,
        out_shape=[jax.ShapeDtypeStruct((b, w), BF16), jax.ShapeDtypeStruct((b, n_state), F32),
                   jax.ShapeDtypeStruct((b, n_state), F32)],
        compiler_params=pltpu.CompilerParams(vmem_limit_bytes=VMEM_LIMIT), name="s5_sample")(*args)


def _state_tiles(a):
    return a.reshape(SUBLANE, -1, LANE).transpose(1, 0, 2)


def _state_untile(h, g, n):
    return h.transpose(0, 2, 1, 3).reshape(h.shape[0], g, n)


def _s5_pack(a_re, a_im, log_dt, b_re, b_im, c_re, c_im, d):
    g, n = a_re.shape
    p = d.shape[1]
    dt = jnp.exp(log_dt)[:, None]
    mag = jnp.exp(a_re * dt)
    ab_re = mag * jnp.cos(a_im * dt)
    ab_im = mag * jnp.sin(a_im * dt)
    num_re, num_im = ab_re - 1.0, ab_im
    den = a_re * a_re + a_im * a_im
    z_re = (num_re * a_re + num_im * a_im) / den
    z_im = (num_im * a_re - num_re * a_im) / den
    bb_re = z_re[..., None] * b_re - z_im[..., None] * b_im
    bb_im = z_re[..., None] * b_im + z_im[..., None] * b_re
    gpc = S5_CHUNK // n
    n_chunks = g // gpc
    eye = jnp.eye(gpc, dtype=F32)
    chunks_per_in_tile = LANE // (gpc * p)
    chunks_per_out_tile = S5_CHUNK // (gpc * p)

    def pack_in(bb):
        blk = jnp.einsum("qgpn,gh->qgphn", bb.transpose(0, 2, 1).reshape(n_chunks, gpc, p, n), eye)
        blk = blk.reshape(n_chunks, gpc * p, S5_CHUNK)
        sel = jax.nn.one_hot(jnp.arange(n_chunks) % chunks_per_in_tile, chunks_per_in_tile, dtype=F32)
        return jnp.einsum("qrc,qs->qsrc", blk, sel).reshape(n_chunks, LANE, S5_CHUNK).astype(BF16)

    def pack_out(cc):
        blk = jnp.einsum("qgnp,gh->qgnhp", cc.transpose(0, 2, 1).reshape(n_chunks, gpc, n, p), eye)
        blk = blk.reshape(n_chunks, S5_CHUNK, gpc * p)
        sel = jax.nn.one_hot(jnp.arange(n_chunks) % chunks_per_out_tile, chunks_per_out_tile, dtype=F32)
        return jnp.einsum("qrc,qs->qrsc", blk, sel).reshape(n_chunks, S5_CHUNK, S5_CHUNK).astype(BF16)

    return {
        "bb_re": pack_in(bb_re), "bb_im": pack_in(bb_im),
        "c_re": pack_out(c_re), "c_imn": pack_out(-c_im),
        "a_re8": _state_tiles(ab_re), "a_im8": _state_tiles(ab_im),
        "a_re1": ab_re.reshape(1, g * n), "a_im1": ab_im.reshape(1, g * n),
        "d_row": d.reshape(1, g * p),
    }


TM = 512
TN = 512
TM_FFN = 1024
TF = 512
TQ = 512
S5_LT = 256


def _gain_row(*pieces):
    return jnp.concatenate([jnp.tile(g, reps) for g, reps in pieces]).reshape(1, -1).astype(F32)


def _head_rows(x, n_heads):
    x = x.reshape(x.shape[0], n_heads, LANE)
    return jnp.pad(x, ((0, 0), (0, HEAD_ROWS - n_heads), (0, 0)))


def kernel(x_prompt, x_sample, cache_fox_k, cache_fox_v, cache_fox_logf, state_s5_re, state_s5_im, cache_mem_k, cache_mem_v, page_table, mem_prompt, norm1_g, w_out, mem_norm_g, w_mem_kv, xq_norm_g, xk_norm_g, norm2_g, w_ffn_gate, w_ffn_up, w_ffn_down, fox_w_in, fox_b_f, fox_q_norm_g, fox_k_norm_g, s5_w_in, s5_a_re, s5_a_im, s5_log_dt, s5_b_re, s5_b_im, s5_c_re, s5_c_im, s5_d, s5_w_glu, s5_b_glu):
    bp, tp, d = x_prompt.shape
    bs, ts, _ = x_sample.shape
    depth = norm1_g.shape[0]
    n_fox, n_pool, page, fox_h, fox_hd = cache_fox_k.shape
    fox_w = fox_h * fox_hd
    n_mem, xa_h, xa_hd = cache_mem_k.shape[2:]
    xa_w = xa_h * xa_hd
    s5_g, s5_n = s5_a_re.shape[1:]
    mp, ms = bp * tp, bs * ts
    assert ts == 1 and fox_hd == LANE and xa_hd == LANE

    xp = x_prompt.reshape(mp, d)
    xs = x_sample.reshape(ms, d)
    mem = mem_prompt.reshape(bp * n_mem, d)
    ck = cache_fox_k.transpose(0, 1, 3, 2, 4).reshape(n_fox * n_pool, fox_h, page, fox_hd)
    cv = cache_fox_v.transpose(0, 1, 3, 2, 4).reshape(n_fox * n_pool, fox_h, page, fox_hd)
    clf = cache_fox_logf.transpose(0, 3, 1, 2).reshape(n_fox * fox_h, n_pool, page)
    cmk = cache_mem_k.reshape(depth, bs, n_mem * xa_h, xa_hd)
    cmv = cache_mem_v.reshape(depth, bs, n_mem * xa_h, xa_hd)

    outs = {k: [] for k in ("fk_p", "fv_p", "fl_p", "fk_s", "fv_s", "fl_s",
                            "sr_p", "si_p", "sr_s", "si_s", "mk_p", "mv_p")}
    fox_nt, xa_nt = fox_w // TN, xa_w // TN

    wo16 = w_out.astype(BF16)
    wkv16 = w_mem_kv.astype(BF16)
    wg16, wu16, wd16 = w_ffn_gate.astype(BF16), w_ffn_up.astype(BF16), w_ffn_down.astype(BF16)

    for i in range(depth):
        j = i // 2

        gain = _gain_row((xk_norm_g[i], xa_h), (jnp.ones((xa_hd,), F32), xa_h))
        segs = ((0, xa_nt, True, (0, 1)), (xa_nt, xa_nt, False, (2, 3)))
        defs = ((F32, 0, xa_nt, 0), (BF16, 0, xa_nt, 0), (F32, xa_nt, xa_nt, 0), (BF16, xa_nt, xa_nt, 0))
        mk32, mk16, mv32, mv16 = _proj(mem, mem_norm_g[i], wkv16, gain, segs, defs,
                                       tm=bp * n_mem, tn=TN, layer=i)
        outs["mk_p"].append(mk32.reshape(bp, n_mem, xa_h, xa_hd))
        outs["mv_p"].append(mv32.reshape(bp, n_mem, xa_h, xa_hd))

        if i % 2 == 0:
            w_in = fox_w_in[j]
            w_main = jnp.concatenate([w_in[:, :3 * fox_w], w_in[:, 3 * fox_w + fox_h:]], axis=1).astype(BF16)
            wf = jnp.pad(w_in[:, 3 * fox_w:3 * fox_w + fox_h], ((0, 0), (0, LANE - fox_h))).astype(BF16)
            bf = jnp.pad(fox_b_f[j], (0, LANE - fox_h)).reshape(1, LANE)
            gain = _gain_row((fox_q_norm_g[j] * Q_PRESCALE, fox_h), (fox_k_norm_g[j], fox_h),
                             (jnp.ones((fox_hd,), F32), fox_h), (xq_norm_g[i], xa_h))
            segs = ((0, fox_nt, True, (0,)), (fox_nt, fox_nt, True, (1, 2)),
                    (2 * fox_nt, fox_nt, False, (3, 4)), (3 * fox_nt, xa_nt, True, (5,)))
            defs = ((BF16, 0, fox_nt, 0), (F32, fox_nt, fox_nt, page), (BF16, fox_nt, fox_nt, 0),
                    (F32, 2 * fox_nt, fox_nt, page), (BF16, 2 * fox_nt, fox_nt, 0),
                    (BF16, 3 * fox_nt, xa_nt, 0))
            q16, k32, k16, v32, v16, xq16, lf_p, qa, ka = _proj(
                xp, norm1_g[i], w_main, gain, segs, defs, tm=TM, tn=TN, wf=wf, bf=bf, seq_len=tp,
                n_heads=fox_h)
            outs["fk_p"].append(k32.transpose(0, 2, 1, 3))
            outs["fv_p"].append(v32.transpose(0, 2, 1, 3))
            outs["fl_p"].append(lf_p[:, :fox_h].reshape(mp // page, page, fox_h))
            mix_p = _flash(q16.reshape(bp, tp, fox_w), qa.reshape(bp, tp, LANE), k16.reshape(bp, tp, fox_w),
                           ka.reshape(bp, tp, LANE), v16.reshape(bp, tp, fox_w), tq=TQ).reshape(mp, fox_w)

            defs_s = ((F32, 0, fox_nt, 0), (F32, fox_nt, fox_nt, 0),
                      (F32, 2 * fox_nt, fox_nt, 0), (F32, 3 * fox_nt, xa_nt, 0))
            segs_s = ((0, fox_nt, True, (0,)), (fox_nt, fox_nt, True, (1,)),
                      (2 * fox_nt, fox_nt, False, (2,)), (3 * fox_nt, xa_nt, True, (3,)))
            qs, ks, vs, xqs, lf_s = _proj(xs, norm1_g[i], w_main, gain, segs_s, defs_s,
                                          tm=ms, tn=TN, wf=wf, bf=bf)
            outs["fk_s"].append(ks.reshape(bs, ts, fox_h, fox_hd))
            outs["fv_s"].append(vs.reshape(bs, ts, fox_h, fox_hd))
            outs["fl_s"].append(lf_s[:, :fox_h].reshape(bs, ts, fox_h))
            lf_col = lf_s[:, :HEAD_ROWS].reshape(bs, HEAD_ROWS, 1)
            mix_s = _dec_fox(qs.reshape(bs, 1, fox_w), ks.reshape(bs, 1, fox_w), vs.reshape(bs, 1, fox_w),
                             lf_col, ck, cv, clf, page_table, j).reshape(ms, fox_w)
        else:
            w_main = s5_w_in[j].astype(BF16)
            s5_nt = (w_main.shape[1] - xa_w) // TN
            gain = _gain_row((jnp.ones((LANE,), F32), s5_nt * TN // LANE), (xq_norm_g[i], xa_h))
            segs = ((0, s5_nt, False, (0,)), (s5_nt, xa_nt, True, (1,)))
            pk = _s5_pack(s5_a_re[j], s5_a_im[j], s5_log_dt[j], s5_b_re[j], s5_b_im[j],
                          s5_c_re[j], s5_c_im[j], s5_d[j])
            w_glu = s5_w_glu[j].astype(BF16)
            b_glu = s5_b_glu[j].reshape(1, -1)
            u_p, xq16 = _proj(xp, norm1_g[i], w_main, gain, segs,
                              ((F32, 0, s5_nt, 0), (BF16, s5_nt, xa_nt, 0)), tm=TM, tn=TN)
            s5_w = u_p.shape[1]
            mix_p, hr, hi = _s5_prompt(u_p.reshape(bp, tp, s5_w), pk, w_glu, b_glu, lt=S5_LT)
            mix_p = mix_p.reshape(mp, s5_w)
            outs["sr_p"].append(_state_untile(hr, s5_g, s5_n))
            outs["si_p"].append(_state_untile(hi, s5_g, s5_n))
            u_s, xqs = _proj(xs, norm1_g[i], w_main, gain, segs,
                             ((F32, 0, s5_nt, 0), (F32, s5_nt, xa_nt, 0)), tm=ms, tn=TN)
            mix_s, hr, hi = _s5_sample(u_s, state_s5_re[j].reshape(bs, s5_g * s5_n),
                                       state_s5_im[j].reshape(bs, s5_g * s5_n), pk, w_glu, b_glu)
            outs["sr_s"].append(hr.reshape(bs, s5_g, s5_n))
            outs["si_s"].append(hi.reshape(bs, s5_g, s5_n))

        xa_p = _xattn(xq16.reshape(bp, tp, xa_w), mk16.reshape(bp, n_mem, xa_w),
                      mv16.reshape(bp, n_mem, xa_w), tq=TQ).reshape(mp, xa_w)
        xa_s = _dec_xattn(_head_rows(xqs, xa_h), cmk, cmv, i, xa_h)[:, :xa_h].reshape(ms, xa_w)

        xp = _outproj(mix_p, xa_p, wo16, i, xp, tm=TM, tn=d)
        xs = _outproj(mix_s, xa_s, wo16, i, xs, tm=ms, tn=d)
        xp = _ffn(xp, norm2_g[i], wg16, wu16, wd16, i, tm=TM_FFN, tf=TF)
        xs = _ffn(xs, norm2_g[i], wg16, wu16, wd16, i, tm=ms, tf=TF)

    st = lambda k: jnp.stack(outs[k])
    return (xp.reshape(bp, tp, d), xs.reshape(bs, ts, d),
            st("fk_p"), st("fv_p"), st("fl_p"), st("fk_s"), st("fv_s"), st("fl_s"),
            st("sr_p"), st("si_p"), st("sr_s"), st("si_s"), st("mk_p"), st("mv_p"))
```

```python
import functools
import math

import jax
import jax.numpy as jnp
import numpy as np
from jax import lax
from jax.experimental import pallas as pl
from jax.experimental.pallas import tpu as pltpu

F32 = jnp.float32
BF16 = jnp.bfloat16

EPS = 1e-6
NEG_INF = -1e30
LANE = 128
SUBLANE = 8
HEAD_ROWS = 16
VMEM_LIMIT = 52 * 1024 * 1024
VMEM_LIMIT_BIG = 58 * 1024 * 1024
NT_DIMS = (((1,), (1,)), ((), ()))


def _params(sem, vmem=VMEM_LIMIT):
    return pltpu.CompilerParams(dimension_semantics=sem, vmem_limit_bytes=vmem)


def _split3(x):
    hi = x.astype(BF16)
    r1 = x - hi.astype(F32)
    mid = r1.astype(BF16)
    lo = (r1 - mid.astype(F32)).astype(BF16)
    return hi, mid, lo


def _dot_exact01(x, w01):
    r = jnp.dot(jnp.concatenate(_split3(x), axis=0), w01, preferred_element_type=F32)
    return r[0:HEAD_ROWS] + r[HEAD_ROWS:2 * HEAD_ROWS] + r[2 * HEAD_ROWS:3 * HEAD_ROWS]


def _log_sigmoid(x):
    return jnp.minimum(x, 0.0) - jnp.log1p(jnp.exp(-jnp.abs(x)))


def _head_rmsnorm(y, gain_row):
    parts = []
    for c in range(y.shape[-1] // LANE):
        p = y[:, c * LANE:(c + 1) * LANE]
        parts.append(p * lax.rsqrt(jnp.mean(p * p, axis=-1, keepdims=True) + EPS))
    return jnp.concatenate(parts, axis=-1) * gain_row


AUG_LANES = 8
SQRT2 = math.sqrt(2.0)
Q_PRESCALE = 0.125
CUMSUM_ROWS = 512


def _lane_split3(x):
    return jnp.concatenate(_split3(x), axis=-1)


def _lane_sum3(r):
    return r[:, 0:LANE] + r[:, LANE:2 * LANE] + r[:, 2 * LANE:3 * LANE]


def _proj_kernel(*refs, segs, n_out, with_f, cumsum, seq_tiles):
    x_ref, g_ref, w_ref, gain_ref = refs[:4]
    pos = 4
    if with_f:
        wf_ref, bf_ref = refs[4:6]
        pos = 6
        if cumsum:
            wqa_ref, wka_ref, oneq_ref, onek_ref = refs[6:10]
            pos = 10
    out_refs = refs[pos:pos + n_out]
    pos += n_out
    if with_f:
        lf_ref = refs[pos]
        pos += 1
        if cumsum:
            qa_ref, ka_ref = refs[pos:pos + 2]
            pos += 2
    h_scr = refs[pos]
    pos += 1
    if with_f and cumsum:
        tri_scr, carry_scr = refs[pos:pos + 2]

    i = pl.program_id(0)
    j = pl.program_id(1)
    tm = x_ref.shape[0]

    if with_f and cumsum:
        ct = tri_scr.shape[0]

        @pl.when((i == 0) & (j == 0))
        def _():
            t = lax.broadcasted_iota(jnp.int32, (ct, ct), 0)
            u = lax.broadcasted_iota(jnp.int32, (ct, ct), 1)
            tri_scr[...] = jnp.where(u <= t, 1.0, 0.0).astype(BF16)

    @pl.when(j == 0)
    def _():
        x = x_ref[...]
        h = x * lax.rsqrt(jnp.mean(x * x, axis=-1, keepdims=True) + EPS) * g_ref[...]
        hb = h.astype(BF16)
        h_scr[...] = hb
        if with_f:
            lf = _log_sigmoid(jnp.dot(hb, wf_ref[...], preferred_element_type=F32) + bf_ref[...])
            lf_ref[...] = lf
            if cumsum:
                @pl.when(i % seq_tiles == 0)
                def _():
                    carry_scr[...] = jnp.zeros_like(carry_scr)
                carry = carry_scr[...]
                for s in range(tm // ct):
                    rs = slice(s * ct, (s + 1) * ct)
                    c = _lane_sum3(jnp.dot(tri_scr[...], _lane_split3(lf[rs]), preferred_element_type=F32))
                    c = c + carry
                    carry = c[ct - 1:ct, :]
                    pieces = _lane_split3(c * SQRT2)
                    qa_ref[rs, :] = (jnp.dot(pieces, wqa_ref[...], preferred_element_type=F32)
                                     + oneq_ref[...]).astype(qa_ref.dtype)
                    ka_ref[rs, :] = (jnp.dot(pieces, wka_ref[...], preferred_element_type=F32)
                                     + onek_ref[...]).astype(ka_ref.dtype)
                carry_scr[...] = carry

    n_sub = 2 if tm % (2 * HEAD_ROWS) == 0 and tm >= 2 * LANE else 1
    rows = tm // n_sub
    for start, n_tiles, norm, outs in segs:
        @pl.when((j >= start) & (j < start + n_tiles))
        def _(norm=norm, outs=outs):
            for r in range(n_sub):
                rs = slice(r * rows, (r + 1) * rows)
                y = jnp.dot(h_scr[rs, :], w_ref[...], preferred_element_type=F32)
                yy = _head_rmsnorm(y, gain_ref[...]) if norm else y
                for o in outs:
                    if len(out_refs[o].shape) == 4:
                        page = out_refs[o].shape[2]
                        for pg in range(rows // page):
                            for hh in range(yy.shape[1] // LANE):
                                out_refs[o][r * (rows // page) + pg, hh] = (
                                    yy[pg * page:(pg + 1) * page, hh * LANE:(hh + 1) * LANE])
                    else:
                        out_refs[o][rs, :] = yy.astype(out_refs[o].dtype)


def _aug_maps(n_heads):
    wqa = np.zeros((3 * LANE, LANE), np.float32)
    wka = np.zeros((3 * LANE, LANE), np.float32)
    oneq = np.zeros((1, LANE), np.float32)
    onek = np.zeros((1, LANE), np.float32)
    for h in range(n_heads):
        for piece in range(3):
            wqa[piece * LANE + h, AUG_LANES * h + piece] = 1.0
            wka[piece * LANE + h, AUG_LANES * h + 3 + piece] = -1.0
            oneq[0, AUG_LANES * h + 3 + piece] = 1.0
            onek[0, AUG_LANES * h + piece] = 1.0
    return jnp.asarray(wqa, BF16), jnp.asarray(wka, BF16), jnp.asarray(oneq), jnp.asarray(onek)


def _proj(x, g, w, gain_row, segs, out_defs, *, tm, tn, layer=None, wf=None, bf=None, seq_len=None,
          n_heads=None):
    m, k = x.shape
    n = w.shape[-1]
    with_f = wf is not None
    cumsum = seq_len is not None
    grid = (m // tm, n // tn)
    const = lambda shape: pl.BlockSpec(shape, lambda i, j: (0,) * len(shape))
    w_spec = (pl.BlockSpec((k, tn), lambda i, j: (0, j)) if layer is None
              else pl.BlockSpec((None, k, tn), lambda i, j: (layer, 0, j)))
    in_specs = [pl.BlockSpec((tm, k), lambda i, j: (i, 0)), const((1, k)), w_spec,
                pl.BlockSpec((1, tn), lambda i, j: (0, j))]
    args = [x, g.reshape(1, k), w, gain_row]
    if with_f:
        in_specs += [const((k, LANE)), const((1, LANE))]
        args += [wf, bf]
        if cumsum:
            maps = _aug_maps(n_heads)
            in_specs += [const(a.shape) for a in maps]
            args += list(maps)
    out_shapes, out_specs = [], []
    for dt, start, n_tiles, page in out_defs:
        if page:
            out_shapes.append(jax.ShapeDtypeStruct((m // page, n_tiles * tn // LANE, page, LANE), dt))
            out_specs.append(pl.BlockSpec(
                (tm // page, tn // LANE, page, LANE),
                lambda i, j, s=start, nt=n_tiles: (i, jnp.clip(j - s, 0, nt - 1), 0, 0)))
        else:
            out_shapes.append(jax.ShapeDtypeStruct((m, n_tiles * tn), dt))
            out_specs.append(pl.BlockSpec(
                (tm, tn), lambda i, j, s=start, nt=n_tiles: (i, jnp.clip(j - s, 0, nt - 1))))
    scratch = [pltpu.VMEM((tm, k), BF16)]
    seq_tiles = 1
    if with_f:
        row_spec = pl.BlockSpec((tm, LANE), lambda i, j: (i, 0))
        out_shapes.append(jax.ShapeDtypeStruct((m, LANE), F32))
        out_specs.append(row_spec)
        if cumsum:
            seq_tiles = seq_len // tm
            ct = math.gcd(tm, CUMSUM_ROWS)
            out_shapes += [jax.ShapeDtypeStruct((m, LANE), BF16)] * 2
            out_specs += [row_spec] * 2
            scratch += [pltpu.VMEM((ct, ct), BF16), pltpu.VMEM((1, LANE), F32)]
    kern = functools.partial(_proj_kernel, segs=segs, n_out=len(out_defs), with_f=with_f,
                             cumsum=cumsum, seq_tiles=seq_tiles)
    return pl.pallas_call(
        kern, grid=grid, in_specs=in_specs, out_specs=out_specs, out_shape=out_shapes,
        scratch_shapes=scratch,
        compiler_params=_params(("arbitrary", "arbitrary"), VMEM_LIMIT_BIG if tm >= 1024 else VMEM_LIMIT),
        name="norm_proj")(*args)


LOG2_SCALE = math.log2(math.e) / SQRT2


def _flash_kernel(qi_ref, kj_ref, q_ref, qa_ref, k_ref, ka_ref, v_ref, o_ref,
                  qaug_scr, m_scr, l_scr, acc_scr):
    hg = pl.program_id(1)
    i = qi_ref[pl.program_id(2)]
    j = kj_ref[pl.program_id(2)]
    tq, tk = q_ref.shape[0], k_ref.shape[0]
    n_hp = q_ref.shape[1] // LANE

    @pl.when(j == 0)
    def _():
        lane = lax.broadcasted_iota(jnp.int32, (tq, LANE), 1)
        for hp in range(n_hp):
            qaug_scr[hp, :, 0:LANE] = q_ref[:, hp * LANE:(hp + 1) * LANE]
            qaug_scr[hp, :, LANE:2 * LANE] = jnp.where(lane // AUG_LANES == hg * n_hp + hp, qa_ref[...],
                                                       jnp.zeros_like(qa_ref))
        m_scr[...] = jnp.full_like(m_scr, NEG_INF)
        l_scr[...] = jnp.zeros_like(l_scr)
        acc_scr[...] = jnp.zeros_like(acc_scr)

    def step(masked):
        for hp in range(n_hp):
            hs = slice(hp * LANE, (hp + 1) * LANE)
            k_aug = jnp.concatenate([k_ref[:, hs], ka_ref[...]], axis=-1)
            x = lax.dot_general(qaug_scr[hp], k_aug, NT_DIMS, preferred_element_type=F32) * LOG2_SCALE
            if masked:
                row = lax.broadcasted_iota(jnp.int32, x.shape, 0)
                col = lax.broadcasted_iota(jnp.int32, x.shape, 1)
                x = jnp.where(col <= row, x, NEG_INF)
            m_old = m_scr[hp]
            m_new = jnp.maximum(m_old, jnp.max(x, axis=-1, keepdims=True))
            alpha = jnp.exp2(m_old - m_new)
            l_part = alpha * l_scr[hp]
            ps = []
            for c in range(tk // LANE):
                p = jnp.exp2(x[:, c * LANE:(c + 1) * LANE] - m_new)
                l_part = l_part + p
                ps.append(p.astype(BF16))
            l_scr[hp] = l_part
            m_scr[hp] = m_new
            acc_scr[hp] = alpha * acc_scr[hp] + jnp.dot(jnp.concatenate(ps, axis=-1), v_ref[:, hs],
                                                        preferred_element_type=F32)

    @pl.when(j < i)
    def _():
        step(False)

    @pl.when(j == i)
    def _():
        step(True)
        for hp in range(n_hp):
            o_ref[:, hp * LANE:(hp + 1) * LANE] = (
                acc_scr[hp] / jnp.sum(l_scr[hp], axis=-1, keepdims=True)).astype(o_ref.dtype)


FLASH_HEADS_PER_STEP = 6


def _flash(q, qa, k, ka, v, *, tq):
    b, t, w = q.shape
    n_hp = FLASH_HEADS_PER_STEP
    hw = n_hp * LANE
    assert w % hw == 0
    nq = t // tq
    pairs = [(i, j) for i in range(nq) for j in range(i + 1)]
    qi = jnp.asarray([p[0] for p in pairs], jnp.int32)
    kj = jnp.asarray([p[1] for p in pairs], jnp.int32)
    q_map = lambda bb, hh, s, qi, kj: (bb, qi[s], hh)
    kv_map = lambda bb, hh, s, qi, kj: (bb, kj[s], hh)
    grid_spec = pltpu.PrefetchScalarGridSpec(
        num_scalar_prefetch=2, grid=(b, w // hw, len(pairs)),
        in_specs=[
            pl.BlockSpec((None, tq, hw), q_map),
            pl.BlockSpec((None, tq, LANE), lambda bb, hh, s, qi, kj: (bb, qi[s], 0)),
            pl.BlockSpec((None, tq, hw), kv_map),
            pl.BlockSpec((None, tq, LANE), lambda bb, hh, s, qi, kj: (bb, kj[s], 0)),
            pl.BlockSpec((None, tq, hw), kv_map),
        ],
        out_specs=pl.BlockSpec((None, tq, hw), q_map),
        scratch_shapes=[pltpu.VMEM((n_hp, tq, 2 * LANE), BF16), pltpu.VMEM((n_hp, tq, LANE), F32),
                        pltpu.VMEM((n_hp, tq, LANE), F32), pltpu.VMEM((n_hp, tq, LANE), F32)])
    return pl.pallas_call(
        _flash_kernel, grid_spec=grid_spec, out_shape=jax.ShapeDtypeStruct((b, t, w), BF16),
        compiler_params=_params(("arbitrary",) * 3), name="fox_prompt_attn")(qi, kj, q, qa, k, ka, v)


def _xattn_kernel(q_ref, k_ref, v_ref, o_ref, *, scale):
    for h in range(q_ref.shape[-1] // LANE):
        sl = slice(h * LANE, (h + 1) * LANE)
        s = lax.dot_general(q_ref[:, sl], k_ref[:, sl], NT_DIMS, preferred_element_type=F32) * scale
        p = jnp.exp(s - jnp.max(s, axis=-1, keepdims=True))
        p = p / jnp.sum(p, axis=-1, keepdims=True)
        o_ref[:, sl] = jnp.dot(p.astype(BF16), v_ref[:, sl],
                               preferred_element_type=F32).astype(o_ref.dtype)


def _xattn(q, k, v, *, tq):
    b, t, w = q.shape
    n_mem = k.shape[1]
    kern = functools.partial(_xattn_kernel, scale=LANE ** -0.5)
    return pl.pallas_call(
        kern, grid=(b, t // tq),
        in_specs=[pl.BlockSpec((None, tq, w), lambda bb, i: (bb, i, 0)),
                  pl.BlockSpec((None, n_mem, w), lambda bb, i: (bb, 0, 0)),
                  pl.BlockSpec((None, n_mem, w), lambda bb, i: (bb, 0, 0))],
        out_specs=pl.BlockSpec((None, tq, w), lambda bb, i: (bb, i, 0)),
        out_shape=jax.ShapeDtypeStruct((b, t, w), BF16),
        compiler_params=_params(("arbitrary", "arbitrary")), name="mem_xattn_prompt")(q, k, v)


def _head_diag(width):
    rows = lax.broadcasted_iota(jnp.int32, (HEAD_ROWS, width), 0)
    lanes = lax.broadcasted_iota(jnp.int32, (HEAD_ROWS, width), 1)
    return (lanes // LANE) == rows


def _dec_fox_kernel(pt_ref, q_ref, kn_ref, vn_ref, lfn_ref, *refs, scale, n_pages, pps):
    kv_refs, lf_refs = refs[:2 * pps], refs[2 * pps:3 * pps]
    o_ref, qbd_scr, m_scr, l_scr, acc_scr, carry_scr = refs[3 * pps:]
    b = pl.program_id(0)
    step = pl.program_id(1)
    n_heads, keys, _ = kv_refs[0].shape
    width = n_heads * LANE
    pair = 2 * LANE
    diag = _head_diag(width)

    @pl.when(step == 0)
    def _():
        qbd = jnp.where(diag, jnp.broadcast_to(q_ref[...], (HEAD_ROWS, width)), 0.0).astype(BF16)
        qbd_scr[...] = qbd
        kn = kn_ref[...].astype(BF16).astype(F32)
        m_scr[...] = jnp.sum(qbd.astype(F32) * kn, axis=-1, keepdims=True) * scale
        l_scr[...] = jnp.ones_like(l_scr)
        acc_scr[...] = jnp.broadcast_to(vn_ref[...].astype(BF16).astype(F32), (HEAD_ROWS, width))
        carry_scr[...] = lfn_ref[...]

    u = lax.broadcasted_iota(jnp.int32, (keys, keys), 0)
    kk = lax.broadcasted_iota(jnp.int32, (keys, keys), 1)
    suffix = jnp.where(u >= kk, 1.0, 0.0).astype(BF16)
    sel_h = lax.broadcasted_iota(jnp.int32, (HEAD_ROWS, n_heads * SUBLANE), 0)
    sel_c = lax.broadcasted_iota(jnp.int32, (HEAD_ROWS, n_heads * SUBLANE), 1)

    scores, incls, lfts = [], [], []
    for i in range(pps):
        k_ref, lf_ref = kv_refs[2 * i], lf_refs[i]
        s = jnp.zeros((HEAD_ROWS, keys), F32)
        for c in range(n_heads // 2):
            kp = jnp.concatenate([k_ref[2 * c], k_ref[2 * c + 1]], axis=-1).astype(BF16)
            s += lax.dot_general(qbd_scr[:, c * pair:(c + 1) * pair], kp, NT_DIMS,
                                 preferred_element_type=F32)
        scores.append(s * scale)
        r = pt_ref[b, n_pages - 1 - (step * pps + i)] % SUBLANE
        sel = jnp.where((sel_c // SUBLANE == sel_h) & (sel_c % SUBLANE == r), 1.0, 0.0).astype(BF16)
        lft = sum(jnp.dot(sel, part, preferred_element_type=F32)
                  for part in _split3(lf_ref[...].reshape(n_heads * SUBLANE, keys)))
        lfts.append(lft)
        incls.append(_dot_exact01(lft, suffix))

    carry = carry_scr[...]
    logits = []
    for i in range(pps):
        logits.append(scores[i] + (carry + incls[i] - lfts[i]))
        carry = carry + incls[i][:, 0:1]
    carry_scr[...] = carry
    logits = jnp.concatenate(logits, axis=-1)

    m_old = m_scr[...]
    m_new = jnp.maximum(m_old, jnp.max(logits, axis=-1, keepdims=True))
    alpha = jnp.exp(m_old - m_new)
    pw = jnp.exp(logits - m_new)
    l_scr[...] = alpha * l_scr[...] + jnp.sum(pw, axis=-1, keepdims=True)
    m_scr[...] = m_new
    pb = pw.astype(BF16)
    for c in range(n_heads // 2):
        vp = jnp.concatenate(
            [jnp.concatenate([kv_refs[2 * i + 1][2 * c], kv_refs[2 * i + 1][2 * c + 1]], axis=-1)
             for i in range(pps)], axis=0).astype(BF16)
        sl = slice(c * pair, (c + 1) * pair)
        acc_scr[:, sl] = alpha * acc_scr[:, sl] + jnp.dot(pb, vp, preferred_element_type=F32)

    @pl.when(step == n_pages // pps - 1)
    def _():
        o_ref[...] = jnp.sum(jnp.where(diag, acc_scr[...] / l_scr[...], 0.0), axis=0, keepdims=True)


DEC_PAGES_PER_STEP = 8


def _dec_fox(q, k_new, v_new, lf_new_col, cache_k, cache_v, cache_lf, page_table, layer):
    b, _, w = q.shape
    n_pages = page_table.shape[1]
    n_heads, page = cache_k.shape[1:3]
    n_pool = cache_lf.shape[1]
    pps = DEC_PAGES_PER_STEP
    assert n_pages % pps == 0 and n_heads % 2 == 0 and n_pool % SUBLANE == 0

    def page_id(bb, p, pt, i):
        return pt[bb, n_pages - 1 - (p * pps + i)]

    row_map = lambda bb, p, pt: (bb, 0, 0)
    row_spec = pl.BlockSpec((None, 1, w), row_map)
    kv_specs, lf_specs = [], []
    for i in range(pps):
        kv_map = lambda bb, p, pt, i=i: (page_id(bb, p, pt, i) + layer * n_pool, 0, 0, 0)
        kv_specs += [pl.BlockSpec((None, n_heads, page, LANE), kv_map)] * 2
        lf_specs.append(pl.BlockSpec((n_heads, SUBLANE, page),
                                     lambda bb, p, pt, i=i: (layer, page_id(bb, p, pt, i) // SUBLANE, 0)))
    kern = functools.partial(_dec_fox_kernel, scale=LANE ** -0.5 / Q_PRESCALE, n_pages=n_pages, pps=pps)
    grid_spec = pltpu.PrefetchScalarGridSpec(
        num_scalar_prefetch=1, grid=(b, n_pages // pps),
        in_specs=[row_spec, row_spec, row_spec, pl.BlockSpec((None, HEAD_ROWS, 1), row_map)]
        + kv_specs + lf_specs,
        out_specs=row_spec,
        scratch_shapes=[pltpu.VMEM((HEAD_ROWS, w), BF16), pltpu.VMEM((HEAD_ROWS, 1), F32),
                        pltpu.VMEM((HEAD_ROWS, 1), F32), pltpu.VMEM((HEAD_ROWS, w), F32),
                        pltpu.VMEM((HEAD_ROWS, 1), F32)])
    return pl.pallas_call(
        kern, grid_spec=grid_spec, out_shape=jax.ShapeDtypeStruct((b, 1, w), F32),
        compiler_params=_params(("arbitrary", "arbitrary")), name="fox_decode_attn")(
            page_table, q, k_new, v_new, lf_new_col, *([cache_k, cache_v] * pps), *([cache_lf] * pps))


def _dec_xattn_kernel(q_ref, k_ref, v_ref, o_ref, *, scale, n_heads):
    for b in range(q_ref.shape[0]):
        s = lax.dot_general(q_ref[b].astype(BF16), k_ref[b].astype(BF16), NT_DIMS,
                            preferred_element_type=F32) * scale
        row = lax.broadcasted_iota(jnp.int32, s.shape, 0)
        col = lax.broadcasted_iota(jnp.int32, s.shape, 1)
        logits = jnp.where(col % n_heads == row, s, NEG_INF)
        p = jnp.exp(logits - jnp.max(logits, axis=-1, keepdims=True))
        p = p / jnp.sum(p, axis=-1, keepdims=True)
        o_ref[b] = jnp.dot(p.astype(BF16), v_ref[b].astype(BF16), preferred_element_type=F32)


DEC_XATTN_SEQS_PER_STEP = 4


def _dec_xattn(q, mem_k, mem_v, layer, n_heads):
    b = q.shape[0]
    n_rows = mem_k.shape[2]
    spb = DEC_XATTN_SEQS_PER_STEP
    assert b % spb == 0
    kern = functools.partial(_dec_xattn_kernel, scale=LANE ** -0.5, n_heads=n_heads)
    head_spec = pl.BlockSpec((spb, HEAD_ROWS, LANE), lambda bb: (bb, 0, 0))
    mem_spec = pl.BlockSpec((None, spb, n_rows, LANE), lambda bb: (layer, bb, 0, 0))
    return pl.pallas_call(
        kern, grid=(b // spb,), in_specs=[head_spec, mem_spec, mem_spec], out_specs=head_spec,
        out_shape=jax.ShapeDtypeStruct((b, HEAD_ROWS, LANE), F32),
        compiler_params=_params(("arbitrary",)), name="mem_xattn_decode")(q, mem_k, mem_v)


def _outproj_kernel(a_ref, xa_ref, w_ref, x_ref, o_ref):
    lhs = jnp.concatenate([a_ref[...].astype(BF16), xa_ref[...].astype(BF16)], axis=-1)
    o_ref[...] = x_ref[...] + jnp.dot(lhs, w_ref[...], preferred_element_type=F32)


def _outproj(a, xa, w, layer, x, *, tm, tn):
    m, d = x.shape
    ka, kx = a.shape[1], xa.shape[1]
    return pl.pallas_call(
        _outproj_kernel, grid=(m // tm, d // tn),
        in_specs=[pl.BlockSpec((tm, ka), lambda i, j: (i, 0)),
                  pl.BlockSpec((tm, kx), lambda i, j: (i, 0)),
                  pl.BlockSpec((None, ka + kx, tn), lambda i, j: (layer, 0, j)),
                  pl.BlockSpec((tm, tn), lambda i, j: (i, j))],
        out_specs=pl.BlockSpec((tm, tn), lambda i, j: (i, j)),
        out_shape=jax.ShapeDtypeStruct((m, d), F32),
        compiler_params=_params(("arbitrary", "arbitrary")), name="out_proj")(a, xa, w, x)


def _ffn_kernel(x_ref, g_ref, wg_ref, wu_ref, wd_ref, o_ref, h_scr):
    f = pl.program_id(1)

    @pl.when(f == 0)
    def _():
        x = x_ref[...]
        h = x * lax.rsqrt(jnp.mean(x * x, axis=-1, keepdims=True) + EPS) * g_ref[...]
        h_scr[...] = h.astype(BF16)
        o_ref[...] = x

    h = h_scr[...]
    gate = jnp.dot(h, wg_ref[...], preferred_element_type=F32)
    up = jnp.dot(h, wu_ref[...], preferred_element_type=F32)
    act = (gate * jax.nn.sigmoid(gate)) * up
    o_ref[...] += jnp.dot(act.astype(BF16), wd_ref[...], preferred_element_type=F32)


def _ffn(x, g, w_gate, w_up, w_down, layer, *, tm, tf):
    m, d = x.shape
    d_ff = w_gate.shape[-1]
    return pl.pallas_call(
        _ffn_kernel, grid=(m // tm, d_ff // tf),
        in_specs=[pl.BlockSpec((tm, d), lambda i, f: (i, 0)),
                  pl.BlockSpec((1, d), lambda i, f: (0, 0)),
                  pl.BlockSpec((None, d, tf), lambda i, f: (layer, 0, f)),
                  pl.BlockSpec((None, d, tf), lambda i, f: (layer, 0, f)),
                  pl.BlockSpec((None, tf, d), lambda i, f: (layer, f, 0))],
        out_specs=pl.BlockSpec((tm, d), lambda i, f: (i, 0)),
        out_shape=jax.ShapeDtypeStruct((m, d), F32),
        scratch_shapes=[pltpu.VMEM((tm, d), BF16)],
        compiler_params=_params(("arbitrary", "arbitrary"), VMEM_LIMIT_BIG), name="swiglu_ffn")(
            x, g.reshape(1, d), w_gate, w_up, w_down)


S5_CHUNK = 2 * LANE


def _gelu_tanh(y):
    return 0.5 * y * (1.0 + jnp.tanh(math.sqrt(2.0 / math.pi) * (y + 0.044715 * (y * y * y))))


def _s5_prompt_kernel(u_ref, bbre_ref, bbim_ref, cre_ref, cimn_ref, are_ref, aim_ref, d_ref,
                      wglu_ref, bglu_ref, mix_ref, hre_ref, him_ref,
                      sre, sim, hre_s, him_s, y_scr, *, lt, n_chunks):
    t = pl.program_id(1)
    tiles_per_chunk = S5_CHUNK // LANE
    chunks_per_row = are_ref.shape[0] // tiles_per_chunk
    chunks_per_tile = S5_CHUNK // (u_ref.shape[1] // n_chunks)

    @pl.when(t == 0)
    def _():
        hre_s[...] = jnp.zeros_like(hre_s)
        him_s[...] = jnp.zeros_like(him_s)

    u = u_ref[...]
    ub = u.astype(BF16)

    def put(dst, q, val):
        j, lc = divmod(q, chunks_per_row)
        for k in range(tiles_per_chunk):
            dst[lc * tiles_per_chunk + k, pl.ds(j, lt, stride=SUBLANE), :] = val[:, k * LANE:(k + 1) * LANE]

    def get(src, q):
        j, lc = divmod(q, chunks_per_row)
        return jnp.concatenate([src[lc * tiles_per_chunk + k, pl.ds(j, lt, stride=SUBLANE), :]
                                for k in range(tiles_per_chunk)], axis=-1)

    for q in range(n_chunks):
        ut = ub[:, (q // 2) * LANE:(q // 2 + 1) * LANE]
        put(sre, q, jnp.dot(ut, bbre_ref[q], preferred_element_type=F32))
        put(sim, q, jnp.dot(ut, bbim_ref[q], preferred_element_type=F32))

    a_re = are_ref[...]
    a_im = aim_ref[...]

    def body(tt, carry):
        h_re, h_im = carry
        r0 = pl.multiple_of(tt * SUBLANE, SUBLANE)
        n_re = a_re * h_re - a_im * h_im + sre[:, pl.ds(r0, SUBLANE), :]
        n_im = a_re * h_im + a_im * h_re + sim[:, pl.ds(r0, SUBLANE), :]
        sre[:, pl.ds(r0, SUBLANE), :] = n_re
        sim[:, pl.ds(r0, SUBLANE), :] = n_im
        return n_re, n_im

    h_re, h_im = lax.fori_loop(0, lt, body, (hre_s[...], him_s[...]), unroll=8)
    hre_s[...] = h_re
    him_s[...] = h_im
    hre_ref[...] = h_re
    him_ref[...] = h_im

    for c in range(n_chunks // chunks_per_tile):
        acc = jnp.zeros((lt, S5_CHUNK), F32)
        for q in range(c * chunks_per_tile, (c + 1) * chunks_per_tile):
            acc += jnp.dot(get(sre, q).astype(BF16), cre_ref[q], preferred_element_type=F32)
            acc += jnp.dot(get(sim, q).astype(BF16), cimn_ref[q], preferred_element_type=F32)
        sl = slice(c * S5_CHUNK, (c + 1) * S5_CHUNK)
        y_scr[:, sl] = _gelu_tanh(acc + d_ref[:, sl] * u[:, sl])

    y = y_scr[...]
    gate = jax.nn.sigmoid(jnp.dot(y.astype(BF16), wglu_ref[...], preferred_element_type=F32)
                          + bglu_ref[...])
    mix_ref[...] = (y * gate).astype(mix_ref.dtype)


def _s5_prompt(u, pk, w_glu, b_glu, *, lt):
    b, t, w = u.shape
    n_chunks = pk["bb_re"].shape[0]
    st_shape = pk["a_re8"].shape
    full = lambda a: pl.BlockSpec(a.shape, lambda bb, tt, nd=a.ndim: (0,) * nd)
    consts = [pk["bb_re"], pk["bb_im"], pk["c_re"], pk["c_imn"], pk["a_re8"], pk["a_im8"], pk["d_row"],
              w_glu, b_glu]
    kern = functools.partial(_s5_prompt_kernel, lt=lt, n_chunks=n_chunks)
    st_spec = pl.BlockSpec((None,) + st_shape, lambda bb, tt: (bb, 0, 0, 0))
    return pl.pallas_call(
        kern, grid=(b, t // lt),
        in_specs=[pl.BlockSpec((None, lt, w), lambda bb, tt: (bb, tt, 0))] + [full(a) for a in consts],
        out_specs=[pl.BlockSpec((None, lt, w), lambda bb, tt: (bb, tt, 0)), st_spec, st_spec],
        out_shape=[jax.ShapeDtypeStruct((b, t, w), BF16),
                   jax.ShapeDtypeStruct((b,) + st_shape, F32),
                   jax.ShapeDtypeStruct((b,) + st_shape, F32)],
        scratch_shapes=[pltpu.VMEM((st_shape[0], lt * SUBLANE, LANE), F32),
                        pltpu.VMEM((st_shape[0], lt * SUBLANE, LANE), F32),
                        pltpu.VMEM(st_shape, F32), pltpu.VMEM(st_shape, F32),
                        pltpu.VMEM((lt, w), F32)],
        compiler_params=_params(("arbitrary", "arbitrary")), name="s5_prompt")(u, *consts)


def _s5_sample_kernel(u_ref, h0re_ref, h0im_ref, bbre_ref, bbim_ref, cre_ref, cimn_ref, are_ref, aim_ref,
                      d_ref, wglu_ref, bglu_ref, mix_ref, hre_ref, him_ref, *, n_chunks):
    u = u_ref[...]
    ub = u.astype(BF16)
    chunks_per_tile = S5_CHUNK // (u.shape[1] // n_chunks)
    ys = []
    for c in range(n_chunks // chunks_per_tile):
        acc = jnp.zeros((u.shape[0], S5_CHUNK), F32)
        for q in range(c * chunks_per_tile, (c + 1) * chunks_per_tile):
            ut = ub[:, (q // 2) * LANE:(q // 2 + 1) * LANE]
            sl = slice(q * S5_CHUNK, (q + 1) * S5_CHUNK)
            a_re, a_im = are_ref[:, sl], aim_ref[:, sl]
            h_re, h_im = h0re_ref[:, sl], h0im_ref[:, sl]
            n_re = a_re * h_re - a_im * h_im + jnp.dot(ut, bbre_ref[q], preferred_element_type=F32)
            n_im = a_re * h_im + a_im * h_re + jnp.dot(ut, bbim_ref[q], preferred_element_type=F32)
            hre_ref[:, sl] = n_re
            him_ref[:, sl] = n_im
            acc += jnp.dot(n_re.astype(BF16), cre_ref[q], preferred_element_type=F32)
            acc += jnp.dot(n_im.astype(BF16), cimn_ref[q], preferred_element_type=F32)
        cs = slice(c * S5_CHUNK, (c + 1) * S5_CHUNK)
        ys.append(_gelu_tanh(acc + d_ref[:, cs] * u[:, cs]))
    y = jnp.concatenate(ys, axis=-1)
    gate = jax.nn.sigmoid(jnp.dot(y.astype(BF16), wglu_ref[...], preferred_element_type=F32)
                          + bglu_ref[...])
    mix_ref[...] = (y * gate).astype(mix_ref.dtype)


def _s5_sample(u, h0_re, h0_im, pk, w_glu, b_glu):
    b, w = u.shape
    n_state = h0_re.shape[1]
    args = [u, h0_re, h0_im, pk["bb_re"], pk["bb_im"], pk["c_re"], pk["c_imn"], pk["a_re1"], pk["a_im1"],
            pk["d_row"], w_glu, b_glu]
    kern = functools.partial(_s5_sample_kernel, n_chunks=pk["bb_re"].shape[0])
    return pl.pallas_call(
        kern,
        out_shape=[jax.ShapeDtypeStruct((b, w), BF16), jax.ShapeDtypeStruct((b, n_state), F32),
                   jax.ShapeDtypeStruct((b, n_state), F32)],
        compiler_params=pltpu.CompilerParams(vmem_limit_bytes=VMEM_LIMIT), name="s5_sample")(*args)


def _state_tiles(a):
    return a.reshape(SUBLANE, -1, LANE).transpose(1, 0, 2)


def _state_untile(h, g, n):
    return h.transpose(0, 2, 1, 3).reshape(h.shape[0], g, n)


def _s5_pack(a_re, a_im, log_dt, b_re, b_im, c_re, c_im, d):
    g, n = a_re.shape
    p = d.shape[1]
    dt = jnp.exp(log_dt)[:, None]
    mag = jnp.exp(a_re * dt)
    ab_re = mag * jnp.cos(a_im * dt)
    ab_im = mag * jnp.sin(a_im * dt)
    num_re, num_im = ab_re - 1.0, ab_im
    den = a_re * a_re + a_im * a_im
    z_re = (num_re * a_re + num_im * a_im) / den
    z_im = (num_im * a_re - num_re * a_im) / den
    bb_re = z_re[..., None] * b_re - z_im[..., None] * b_im
    bb_im = z_re[..., None] * b_im + z_im[..., None] * b_re
    gpc = S5_CHUNK // n
    n_chunks = g // gpc
    eye = jnp.eye(gpc, dtype=F32)
    chunks_per_in_tile = LANE // (gpc * p)
    chunks_per_out_tile = S5_CHUNK // (gpc * p)

    def pack_in(bb):
        blk = jnp.einsum("qgpn,gh->qgphn", bb.transpose(0, 2, 1).reshape(n_chunks, gpc, p, n), eye)
        blk = blk.reshape(n_chunks, gpc * p, S5_CHUNK)
        sel = jax.nn.one_hot(jnp.arange(n_chunks) % chunks_per_in_tile, chunks_per_in_tile, dtype=F32)
        return jnp.einsum("qrc,qs->qsrc", blk, sel).reshape(n_chunks, LANE, S5_CHUNK).astype(BF16)

    def pack_out(cc):
        blk = jnp.einsum("qgnp,gh->qgnhp", cc.transpose(0, 2, 1).reshape(n_chunks, gpc, n, p), eye)
        blk = blk.reshape(n_chunks, S5_CHUNK, gpc * p)
        sel = jax.nn.one_hot(jnp.arange(n_chunks) % chunks_per_out_tile, chunks_per_out_tile, dtype=F32)
        return jnp.einsum("qrc,qs->qrsc", blk, sel).reshape(n_chunks, S5_CHUNK, S5_CHUNK).astype(BF16)

    return {
        "bb_re": pack_in(bb_re), "bb_im": pack_in(bb_im),
        "c_re": pack_out(c_re), "c_imn": pack_out(-c_im),
        "a_re8": _state_tiles(ab_re), "a_im8": _state_tiles(ab_im),
        "a_re1": ab_re.reshape(1, g * n), "a_im1": ab_im.reshape(1, g * n),
        "d_row": d.reshape(1, g * p),
    }


TM = 1024
TN = 512
TM_FFN = 1024
TF = 512
TM_OUT = 512
TQ = 512
S5_LT = 256


def _gain_row(*pieces):
    return jnp.concatenate([jnp.tile(g, reps) for g, reps in pieces]).reshape(1, -1).astype(F32)


def _head_rows(x, n_heads):
    x = x.reshape(x.shape[0], n_heads, LANE)
    return jnp.pad(x, ((0, 0), (0, HEAD_ROWS - n_heads), (0, 0)))


def kernel(x_prompt, x_sample, cache_fox_k, cache_fox_v, cache_fox_logf, state_s5_re, state_s5_im, cache_mem_k, cache_mem_v, page_table, mem_prompt, norm1_g, w_out, mem_norm_g, w_mem_kv, xq_norm_g, xk_norm_g, norm2_g, w_ffn_gate, w_ffn_up, w_ffn_down, fox_w_in, fox_b_f, fox_q_norm_g, fox_k_norm_g, s5_w_in, s5_a_re, s5_a_im, s5_log_dt, s5_b_re, s5_b_im, s5_c_re, s5_c_im, s5_d, s5_w_glu, s5_b_glu):
    bp, tp, d = x_prompt.shape
    bs, ts, _ = x_sample.shape
    depth = norm1_g.shape[0]
    n_fox, n_pool, page, fox_h, fox_hd = cache_fox_k.shape
    fox_w = fox_h * fox_hd
    n_mem, xa_h, xa_hd = cache_mem_k.shape[2:]
    xa_w = xa_h * xa_hd
    s5_g, s5_n = s5_a_re.shape[1:]
    mp, ms = bp * tp, bs * ts
    assert ts == 1 and fox_hd == LANE and xa_hd == LANE

    xp = x_prompt.reshape(mp, d)
    xs = x_sample.reshape(ms, d)
    mem = mem_prompt.reshape(bp * n_mem, d)
    ck = cache_fox_k.transpose(0, 1, 3, 2, 4).reshape(n_fox * n_pool, fox_h, page, fox_hd)
    cv = cache_fox_v.transpose(0, 1, 3, 2, 4).reshape(n_fox * n_pool, fox_h, page, fox_hd)
    clf = cache_fox_logf.transpose(0, 3, 1, 2).reshape(n_fox * fox_h, n_pool, page)
    cmk = cache_mem_k.reshape(depth, bs, n_mem * xa_h, xa_hd)
    cmv = cache_mem_v.reshape(depth, bs, n_mem * xa_h, xa_hd)

    outs = {k: [] for k in ("fk_p", "fv_p", "fl_p", "fk_s", "fv_s", "fl_s",
                            "sr_p", "si_p", "sr_s", "si_s", "mk_p", "mv_p")}
    fox_nt, xa_nt = fox_w // TN, xa_w // TN

    wo16 = w_out.astype(BF16)
    wkv16 = w_mem_kv.astype(BF16)
    wg16, wu16, wd16 = w_ffn_gate.astype(BF16), w_ffn_up.astype(BF16), w_ffn_down.astype(BF16)

    for i in range(depth):
        j = i // 2

        gain = _gain_row((xk_norm_g[i], xa_h), (jnp.ones((xa_hd,), F32), xa_h))
        segs = ((0, xa_nt, True, (0, 1)), (xa_nt, xa_nt, False, (2, 3)))
        defs = ((F32, 0, xa_nt, 0), (BF16, 0, xa_nt, 0), (F32, xa_nt, xa_nt, 0), (BF16, xa_nt, xa_nt, 0))
        mk32, mk16, mv32, mv16 = _proj(mem, mem_norm_g[i], wkv16, gain, segs, defs,
                                       tm=bp * n_mem, tn=TN, layer=i)
        outs["mk_p"].append(mk32.reshape(bp, n_mem, xa_h, xa_hd))
        outs["mv_p"].append(mv32.reshape(bp, n_mem, xa_h, xa_hd))

        if i % 2 == 0:
            w_in = fox_w_in[j]
            w_main = jnp.concatenate([w_in[:, :3 * fox_w], w_in[:, 3 * fox_w + fox_h:]], axis=1).astype(BF16)
            wf = jnp.pad(w_in[:, 3 * fox_w:3 * fox_w + fox_h], ((0, 0), (0, LANE - fox_h))).astype(BF16)
            bf = jnp.pad(fox_b_f[j], (0, LANE - fox_h)).reshape(1, LANE)
            gain = _gain_row((fox_q_norm_g[j] * Q_PRESCALE, fox_h), (fox_k_norm_g[j], fox_h),
                             (jnp.ones((fox_hd,), F32), fox_h), (xq_norm_g[i], xa_h))
            segs = ((0, fox_nt, True, (0,)), (fox_nt, fox_nt, True, (1, 2)),
                    (2 * fox_nt, fox_nt, False, (3, 4)), (3 * fox_nt, xa_nt, True, (5,)))
            defs = ((BF16, 0, fox_nt, 0), (F32, fox_nt, fox_nt, page), (BF16, fox_nt, fox_nt, 0),
                    (F32, 2 * fox_nt, fox_nt, page), (BF16, 2 * fox_nt, fox_nt, 0),
                    (BF16, 3 * fox_nt, xa_nt, 0))
            q16, k32, k16, v32, v16, xq16, lf_p, qa, ka = _proj(
                xp, norm1_g[i], w_main, gain, segs, defs, tm=TM, tn=TN, wf=wf, bf=bf, seq_len=tp,
                n_heads=fox_h)
            outs["fk_p"].append(k32.transpose(0, 2, 1, 3))
            outs["fv_p"].append(v32.transpose(0, 2, 1, 3))
            outs["fl_p"].append(lf_p[:, :fox_h].reshape(mp // page, page, fox_h))
            mix_p = _flash(q16.reshape(bp, tp, fox_w), qa.reshape(bp, tp, LANE), k16.reshape(bp, tp, fox_w),
                           ka.reshape(bp, tp, LANE), v16.reshape(bp, tp, fox_w), tq=TQ).reshape(mp, fox_w)

            defs_s = ((F32, 0, fox_nt, 0), (F32, fox_nt, fox_nt, 0),
                      (F32, 2 * fox_nt, fox_nt, 0), (F32, 3 * fox_nt, xa_nt, 0))
            segs_s = ((0, fox_nt, True, (0,)), (fox_nt, fox_nt, True, (1,)),
                      (2 * fox_nt, fox_nt, False, (2,)), (3 * fox_nt, xa_nt, True, (3,)))
            qs, ks, vs, xqs, lf_s = _proj(xs, norm1_g[i], w_main, gain, segs_s, defs_s,
                                          tm=ms, tn=TN, wf=wf, bf=bf)
            outs["fk_s"].append(ks.reshape(bs, ts, fox_h, fox_hd))
            outs["fv_s"].append(vs.reshape(bs, ts, fox_h, fox_hd))
            outs["fl_s"].append(lf_s[:, :fox_h].reshape(bs, ts, fox_h))
            lf_col = lf_s[:, :HEAD_ROWS].reshape(bs, HEAD_ROWS, 1)
            mix_s = _dec_fox(qs.reshape(bs, 1, fox_w), ks.reshape(bs, 1, fox_w), vs.reshape(bs, 1, fox_w),
                             lf_col, ck, cv, clf, page_table, j).reshape(ms, fox_w)
        else:
            w_main = s5_w_in[j].astype(BF16)
            s5_nt = (w_main.shape[1] - xa_w) // TN
            gain = _gain_row((jnp.ones((LANE,), F32), s5_nt * TN // LANE), (xq_norm_g[i], xa_h))
            segs = ((0, s5_nt, False, (0,)), (s5_nt, xa_nt, True, (1,)))
            pk = _s5_pack(s5_a_re[j], s5_a_im[j], s5_log_dt[j], s5_b_re[j], s5_b_im[j],
                          s5_c_re[j], s5_c_im[j], s5_d[j])
            w_glu = s5_w_glu[j].astype(BF16)
            b_glu = s5_b_glu[j].reshape(1, -1)
            u_p, xq16 = _proj(xp, norm1_g[i], w_main, gain, segs,
                              ((F32, 0, s5_nt, 0), (BF16, s5_nt, xa_nt, 0)), tm=TM, tn=TN)
            s5_w = u_p.shape[1]
            mix_p, hr, hi = _s5_prompt(u_p.reshape(bp, tp, s5_w), pk, w_glu, b_glu, lt=S5_LT)
            mix_p = mix_p.reshape(mp, s5_w)
            outs["sr_p"].append(_state_untile(hr, s5_g, s5_n))
            outs["si_p"].append(_state_untile(hi, s5_g, s5_n))
            u_s, xqs = _proj(xs, norm1_g[i], w_main, gain, segs,
                             ((F32, 0, s5_nt, 0), (F32, s5_nt, xa_nt, 0)), tm=ms, tn=TN)
            mix_s, hr, hi = _s5_sample(u_s, state_s5_re[j].reshape(bs, s5_g * s5_n),
                                       state_s5_im[j].reshape(bs, s5_g * s5_n), pk, w_glu, b_glu)
            outs["sr_s"].append(hr.reshape(bs, s5_g, s5_n))
            outs["si_s"].append(hi.reshape(bs, s5_g, s5_n))

        xa_p = _xattn(xq16.reshape(bp, tp, xa_w), mk16.reshape(bp, n_mem, xa_w),
                      mv16.reshape(bp, n_mem, xa_w), tq=TQ).reshape(mp, xa_w)
        xa_s = _dec_xattn(_head_rows(xqs, xa_h), cmk, cmv, i, xa_h)[:, :xa_h].reshape(ms, xa_w)

        xp = _outproj(mix_p, xa_p, wo16, i, xp, tm=TM_OUT, tn=d)
        xs = _outproj(mix_s, xa_s, wo16, i, xs, tm=ms, tn=d)
        xp = _ffn(xp, norm2_g[i], wg16, wu16, wd16, i, tm=TM_FFN, tf=TF)
        xs = _ffn(xs, norm2_g[i], wg16, wu16, wd16, i, tm=ms, tf=TF)

    st = lambda k: jnp.stack(outs[k])
    return (xp.reshape(bp, tp, d), xs.reshape(bs, ts, d),
            st("fk_p"), st("fv_p"), st("fl_p"), st("fk_s"), st("fv_s"), st("fl_s"),
            st("sr_p"), st("si_p"), st("sr_s"), st("si_s"), st("mk_p"), st("mv_p"))
```

```python
import functools
import math

import jax
import jax.numpy as jnp
import numpy as np
from jax import lax
from jax.experimental import pallas as pl
from jax.experimental.pallas import tpu as pltpu

F32 = jnp.float32
BF16 = jnp.bfloat16

EPS = 1e-6
NEG_INF = -1e30
LANE = 128
SUBLANE = 8
HEAD_ROWS = 16
VMEM_LIMIT = 52 * 1024 * 1024
VMEM_LIMIT_BIG = 58 * 1024 * 1024
NT_DIMS = (((1,), (1,)), ((), ()))


def _params(sem, vmem=VMEM_LIMIT):
    return pltpu.CompilerParams(dimension_semantics=sem, vmem_limit_bytes=vmem)


def _split3(x):
    hi = x.astype(BF16)
    r1 = x - hi.astype(F32)
    mid = r1.astype(BF16)
    lo = (r1 - mid.astype(F32)).astype(BF16)
    return hi, mid, lo


def _dot_exact01(x, w01):
    r = jnp.dot(jnp.concatenate(_split3(x), axis=0), w01, preferred_element_type=F32)
    return r[0:HEAD_ROWS] + r[HEAD_ROWS:2 * HEAD_ROWS] + r[2 * HEAD_ROWS:3 * HEAD_ROWS]


def _log_sigmoid(x):
    return jnp.minimum(x, 0.0) - jnp.log1p(jnp.exp(-jnp.abs(x)))


def _head_rmsnorm(y, gain_row):
    parts = []
    for c in range(y.shape[-1] // LANE):
        p = y[:, c * LANE:(c + 1) * LANE]
        parts.append(p * lax.rsqrt(jnp.mean(p * p, axis=-1, keepdims=True) + EPS))
    return jnp.concatenate(parts, axis=-1) * gain_row


AUG_LANES = 8
SQRT2 = math.sqrt(2.0)
Q_PRESCALE = 0.125
CUMSUM_ROWS = 512


def _lane_split3(x):
    return jnp.concatenate(_split3(x), axis=-1)


def _lane_sum3(r):
    return r[:, 0:LANE] + r[:, LANE:2 * LANE] + r[:, 2 * LANE:3 * LANE]


def _proj_kernel(*refs, segs, n_out, with_f, cumsum, seq_tiles):
    x_ref, g_ref, w_ref, gain_ref = refs[:4]
    pos = 4
    if with_f:
        wf_ref, bf_ref = refs[4:6]
        pos = 6
        if cumsum:
            wqa_ref, wka_ref, oneq_ref, onek_ref = refs[6:10]
            pos = 10
    out_refs = refs[pos:pos + n_out]
    pos += n_out
    if with_f:
        lf_ref = refs[pos]
        pos += 1
        if cumsum:
            qa_ref, ka_ref = refs[pos:pos + 2]
            pos += 2
    h_scr = refs[pos]
    pos += 1
    if with_f and cumsum:
        tri_scr, carry_scr = refs[pos:pos + 2]

    i = pl.program_id(0)
    j = pl.program_id(1)
    tm = x_ref.shape[0]

    if with_f and cumsum:
        ct = tri_scr.shape[0]

        @pl.when((i == 0) & (j == 0))
        def _():
            t = lax.broadcasted_iota(jnp.int32, (ct, ct), 0)
            u = lax.broadcasted_iota(jnp.int32, (ct, ct), 1)
            tri_scr[...] = jnp.where(u <= t, 1.0, 0.0).astype(BF16)

    @pl.when(j == 0)
    def _():
        x = x_ref[...]
        h = x * lax.rsqrt(jnp.mean(x * x, axis=-1, keepdims=True) + EPS) * g_ref[...]
        hb = h.astype(BF16)
        h_scr[...] = hb
        if with_f:
            lf = _log_sigmoid(jnp.dot(hb, wf_ref[...], preferred_element_type=F32) + bf_ref[...])
            lf_ref[...] = lf
            if cumsum:
                @pl.when(i % seq_tiles == 0)
                def _():
                    carry_scr[...] = jnp.zeros_like(carry_scr)
                carry = carry_scr[...]
                for s in range(tm // ct):
                    rs = slice(s * ct, (s + 1) * ct)
                    c = _lane_sum3(jnp.dot(tri_scr[...], _lane_split3(lf[rs]), preferred_element_type=F32))
                    c = c + carry
                    carry = c[ct - 1:ct, :]
                    pieces = _lane_split3(c * SQRT2)
                    qa_ref[rs, :] = (jnp.dot(pieces, wqa_ref[...], preferred_element_type=F32)
                                     + oneq_ref[...]).astype(qa_ref.dtype)
                    ka_ref[rs, :] = (jnp.dot(pieces, wka_ref[...], preferred_element_type=F32)
                                     + onek_ref[...]).astype(ka_ref.dtype)
                carry_scr[...] = carry

    n_sub = 2 if tm % (2 * HEAD_ROWS) == 0 and tm >= 2 * LANE else 1
    rows = tm // n_sub
    for start, n_tiles, norm, outs in segs:
        @pl.when((j >= start) & (j < start + n_tiles))
        def _(norm=norm, outs=outs):
            for r in range(n_sub):
                rs = slice(r * rows, (r + 1) * rows)
                y = jnp.dot(h_scr[rs, :], w_ref[...], preferred_element_type=F32)
                yy = _head_rmsnorm(y, gain_ref[...]) if norm else y
                for o in outs:
                    if len(out_refs[o].shape) == 4:
                        page = out_refs[o].shape[2]
                        for pg in range(rows // page):
                            for hh in range(yy.shape[1] // LANE):
                                out_refs[o][r * (rows // page) + pg, hh] = (
                                    yy[pg * page:(pg + 1) * page, hh * LANE:(hh + 1) * LANE])
                    else:
                        out_refs[o][rs, :] = yy.astype(out_refs[o].dtype)


def _aug_maps(n_heads):
    wqa = np.zeros((3 * LANE, LANE), np.float32)
    wka = np.zeros((3 * LANE, LANE), np.float32)
    oneq = np.zeros((1, LANE), np.float32)
    onek = np.zeros((1, LANE), np.float32)
    for h in range(n_heads):
        for piece in range(3):
            wqa[piece * LANE + h, AUG_LANES * h + piece] = 1.0
            wka[piece * LANE + h, AUG_LANES * h + 3 + piece] = -1.0
            oneq[0, AUG_LANES * h + 3 + piece] = 1.0
            onek[0, AUG_LANES * h + piece] = 1.0
    return jnp.asarray(wqa, BF16), jnp.asarray(wka, BF16), jnp.asarray(oneq), jnp.asarray(onek)


def _proj(x, g, w, gain_row, segs, out_defs, *, tm, tn, layer=None, wf=None, bf=None, seq_len=None,
          n_heads=None):
    m, k = x.shape
    n = w.shape[-1]
    with_f = wf is not None
    cumsum = seq_len is not None
    grid = (m // tm, n // tn)
    const = lambda shape: pl.BlockSpec(shape, lambda i, j: (0,) * len(shape))
    w_spec = (pl.BlockSpec((k, tn), lambda i, j: (0, j)) if layer is None
              else pl.BlockSpec((None, k, tn), lambda i, j: (layer, 0, j)))
    in_specs = [pl.BlockSpec((tm, k), lambda i, j: (i, 0)), const((1, k)), w_spec,
                pl.BlockSpec((1, tn), lambda i, j: (0, j))]
    args = [x, g.reshape(1, k), w, gain_row]
    if with_f:
        in_specs += [const((k, LANE)), const((1, LANE))]
        args += [wf, bf]
        if cumsum:
            maps = _aug_maps(n_heads)
            in_specs += [const(a.shape) for a in maps]
            args += list(maps)
    out_shapes, out_specs = [], []
    for dt, start, n_tiles, page in out_defs:
        if page:
            out_shapes.append(jax.ShapeDtypeStruct((m // page, n_tiles * tn // LANE, page, LANE), dt))
            out_specs.append(pl.BlockSpec(
                (tm // page, tn // LANE, page, LANE),
                lambda i, j, s=start, nt=n_tiles: (i, jnp.clip(j - s, 0, nt - 1), 0, 0)))
        else:
            out_shapes.append(jax.ShapeDtypeStruct((m, n_tiles * tn), dt))
            out_specs.append(pl.BlockSpec(
                (tm, tn), lambda i, j, s=start, nt=n_tiles: (i, jnp.clip(j - s, 0, nt - 1))))
    scratch = [pltpu.VMEM((tm, k), BF16)]
    seq_tiles = 1
    if with_f:
        row_spec = pl.BlockSpec((tm, LANE), lambda i, j: (i, 0))
        out_shapes.append(jax.ShapeDtypeStruct((m, LANE), F32))
        out_specs.append(row_spec)
        if cumsum:
            seq_tiles = seq_len // tm
            ct = math.gcd(tm, CUMSUM_ROWS)
            out_shapes += [jax.ShapeDtypeStruct((m, LANE), BF16)] * 2
            out_specs += [row_spec] * 2
            scratch += [pltpu.VMEM((ct, ct), BF16), pltpu.VMEM((1, LANE), F32)]
    kern = functools.partial(_proj_kernel, segs=segs, n_out=len(out_defs), with_f=with_f,
                             cumsum=cumsum, seq_tiles=seq_tiles)
    return pl.pallas_call(
        kern, grid=grid, in_specs=in_specs, out_specs=out_specs, out_shape=out_shapes,
        scratch_shapes=scratch,
        compiler_params=_params(("arbitrary", "arbitrary"), VMEM_LIMIT_BIG if tm >= 1024 else VMEM_LIMIT),
        name="norm_proj")(*args)


LOG2_SCALE = math.log2(math.e) / SQRT2


def _flash_kernel(qi_ref, kj_ref, q_ref, qa_ref, k_ref, ka_ref, v_ref, o_ref,
                  qaug_scr, m_scr, l_scr, acc_scr):
    hg = pl.program_id(1)
    i = qi_ref[pl.program_id(2)]
    j = kj_ref[pl.program_id(2)]
    tq, tk = q_ref.shape[0], k_ref.shape[0]
    n_hp = q_ref.shape[1] // LANE

    @pl.when(j == 0)
    def _():
        lane = lax.broadcasted_iota(jnp.int32, (tq, LANE), 1)
        for hp in range(n_hp):
            qaug_scr[hp, :, 0:LANE] = q_ref[:, hp * LANE:(hp + 1) * LANE]
            qaug_scr[hp, :, LANE:2 * LANE] = jnp.where(lane // AUG_LANES == hg * n_hp + hp, qa_ref[...],
                                                       jnp.zeros_like(qa_ref))
        m_scr[...] = jnp.full_like(m_scr, NEG_INF)
        l_scr[...] = jnp.zeros_like(l_scr)
        acc_scr[...] = jnp.zeros_like(acc_scr)

    def step(masked):
        for hp in range(n_hp):
            hs = slice(hp * LANE, (hp + 1) * LANE)
            k_aug = jnp.concatenate([k_ref[:, hs], ka_ref[...]], axis=-1)
            x = lax.dot_general(qaug_scr[hp], k_aug, NT_DIMS, preferred_element_type=F32) * LOG2_SCALE
            if masked:
                row = lax.broadcasted_iota(jnp.int32, x.shape, 0)
                col = lax.broadcasted_iota(jnp.int32, x.shape, 1)
                x = jnp.where(col <= row, x, NEG_INF)
            m_old = m_scr[hp]
            m_new = jnp.maximum(m_old, jnp.max(x, axis=-1, keepdims=True))
            alpha = jnp.exp2(m_old - m_new)
            l_part = alpha * l_scr[hp]
            ps = []
            for c in range(tk // LANE):
                p = jnp.exp2(x[:, c * LANE:(c + 1) * LANE] - m_new)
                l_part = l_part + p
                ps.append(p.astype(BF16))
            l_scr[hp] = l_part
            m_scr[hp] = m_new
            acc_scr[hp] = alpha * acc_scr[hp] + jnp.dot(jnp.concatenate(ps, axis=-1), v_ref[:, hs],
                                                        preferred_element_type=F32)

    @pl.when(j < i)
    def _():
        step(False)

    @pl.when(j == i)
    def _():
        step(True)
        for hp in range(n_hp):
            o_ref[:, hp * LANE:(hp + 1) * LANE] = (
                acc_scr[hp] / jnp.sum(l_scr[hp], axis=-1, keepdims=True)).astype(o_ref.dtype)


FLASH_HEADS_PER_STEP = 12


def _flash(q, qa, k, ka, v, *, tq):
    b, t, w = q.shape
    n_hp = FLASH_HEADS_PER_STEP
    hw = n_hp * LANE
    assert w % hw == 0
    nq = t // tq
    pairs = [(i, j) for i in range(nq) for j in range(i + 1)]
    qi = jnp.asarray([p[0] for p in pairs], jnp.int32)
    kj = jnp.asarray([p[1] for p in pairs], jnp.int32)
    q_map = lambda bb, hh, s, qi, kj: (bb, qi[s], hh)
    kv_map = lambda bb, hh, s, qi, kj: (bb, kj[s], hh)
    grid_spec = pltpu.PrefetchScalarGridSpec(
        num_scalar_prefetch=2, grid=(b, w // hw, len(pairs)),
        in_specs=[
            pl.BlockSpec((None, tq, hw), q_map),
            pl.BlockSpec((None, tq, LANE), lambda bb, hh, s, qi, kj: (bb, qi[s], 0)),
            pl.BlockSpec((None, tq, hw), kv_map),
            pl.BlockSpec((None, tq, LANE), lambda bb, hh, s, qi, kj: (bb, kj[s], 0)),
            pl.BlockSpec((None, tq, hw), kv_map),
        ],
        out_specs=pl.BlockSpec((None, tq, hw), q_map),
        scratch_shapes=[pltpu.VMEM((n_hp, tq, 2 * LANE), BF16), pltpu.VMEM((n_hp, tq, LANE), F32),
                        pltpu.VMEM((n_hp, tq, LANE), F32), pltpu.VMEM((n_hp, tq, LANE), F32)])
    return pl.pallas_call(
        _flash_kernel, grid_spec=grid_spec, out_shape=jax.ShapeDtypeStruct((b, t, w), BF16),
        compiler_params=_params(("arbitrary",) * 3), name="fox_prompt_attn")(qi, kj, q, qa, k, ka, v)


def _xattn_kernel(q_ref, k_ref, v_ref, o_ref, *, scale):
    for h in range(q_ref.shape[-1] // LANE):
        sl = slice(h * LANE, (h + 1) * LANE)
        s = lax.dot_general(q_ref[:, sl], k_ref[:, sl], NT_DIMS, preferred_element_type=F32) * scale
        p = jnp.exp(s - jnp.max(s, axis=-1, keepdims=True))
        p = p / jnp.sum(p, axis=-1, keepdims=True)
        o_ref[:, sl] = jnp.dot(p.astype(BF16), v_ref[:, sl],
                               preferred_element_type=F32).astype(o_ref.dtype)


def _xattn(q, k, v, *, tq):
    b, t, w = q.shape
    n_mem = k.shape[1]
    kern = functools.partial(_xattn_kernel, scale=LANE ** -0.5)
    return pl.pallas_call(
        kern, grid=(b, t // tq),
        in_specs=[pl.BlockSpec((None, tq, w), lambda bb, i: (bb, i, 0)),
                  pl.BlockSpec((None, n_mem, w), lambda bb, i: (bb, 0, 0)),
                  pl.BlockSpec((None, n_mem, w), lambda bb, i: (bb, 0, 0))],
        out_specs=pl.BlockSpec((None, tq, w), lambda bb, i: (bb, i, 0)),
        out_shape=jax.ShapeDtypeStruct((b, t, w), BF16),
        compiler_params=_params(("arbitrary", "arbitrary")), name="mem_xattn_prompt")(q, k, v)


def _head_diag(width):
    rows = lax.broadcasted_iota(jnp.int32, (HEAD_ROWS, width), 0)
    lanes = lax.broadcasted_iota(jnp.int32, (HEAD_ROWS, width), 1)
    return (lanes // LANE) == rows


def _dec_fox_kernel(pt_ref, q_ref, kn_ref, vn_ref, lfn_ref, *refs, scale, n_pages, pps):
    kv_refs, lf_refs = refs[:2 * pps], refs[2 * pps:3 * pps]
    o_ref, qbd_scr, m_scr, l_scr, acc_scr, carry_scr = refs[3 * pps:]
    b = pl.program_id(0)
    step = pl.program_id(1)
    n_heads, keys, _ = kv_refs[0].shape
    width = n_heads * LANE
    pair = 2 * LANE
    diag = _head_diag(width)

    @pl.when(step == 0)
    def _():
        qbd = jnp.where(diag, jnp.broadcast_to(q_ref[...], (HEAD_ROWS, width)), 0.0).astype(BF16)
        qbd_scr[...] = qbd
        kn = kn_ref[...].astype(BF16).astype(F32)
        m_scr[...] = jnp.sum(qbd.astype(F32) * kn, axis=-1, keepdims=True) * scale
        l_scr[...] = jnp.ones_like(l_scr)
        acc_scr[...] = jnp.broadcast_to(vn_ref[...].astype(BF16).astype(F32), (HEAD_ROWS, width))
        carry_scr[...] = lfn_ref[...]

    u = lax.broadcasted_iota(jnp.int32, (keys, keys), 0)
    kk = lax.broadcasted_iota(jnp.int32, (keys, keys), 1)
    suffix = jnp.where(u >= kk, 1.0, 0.0).astype(BF16)
    sel_h = lax.broadcasted_iota(jnp.int32, (HEAD_ROWS, n_heads * SUBLANE), 0)
    sel_c = lax.broadcasted_iota(jnp.int32, (HEAD_ROWS, n_heads * SUBLANE), 1)

    scores, incls, lfts = [], [], []
    for i in range(pps):
        k_ref, lf_ref = kv_refs[2 * i], lf_refs[i]
        s = jnp.zeros((HEAD_ROWS, keys), F32)
        for c in range(n_heads // 2):
            kp = jnp.concatenate([k_ref[2 * c], k_ref[2 * c + 1]], axis=-1).astype(BF16)
            s += lax.dot_general(qbd_scr[:, c * pair:(c + 1) * pair], kp, NT_DIMS,
                                 preferred_element_type=F32)
        scores.append(s * scale)
        r = pt_ref[b, n_pages - 1 - (step * pps + i)] % SUBLANE
        sel = jnp.where((sel_c // SUBLANE == sel_h) & (sel_c % SUBLANE == r), 1.0, 0.0).astype(BF16)
        lft = sum(jnp.dot(sel, part, preferred_element_type=F32)
                  for part in _split3(lf_ref[...].reshape(n_heads * SUBLANE, keys)))
        lfts.append(lft)
        incls.append(_dot_exact01(lft, suffix))

    carry = carry_scr[...]
    logits = []
    for i in range(pps):
        logits.append(scores[i] + (carry + incls[i] - lfts[i]))
        carry = carry + incls[i][:, 0:1]
    carry_scr[...] = carry
    logits = jnp.concatenate(logits, axis=-1)

    m_old = m_scr[...]
    m_new = jnp.maximum(m_old, jnp.max(logits, axis=-1, keepdims=True))
    alpha = jnp.exp(m_old - m_new)
    pw = jnp.exp(logits - m_new)
    l_scr[...] = alpha * l_scr[...] + jnp.sum(pw, axis=-1, keepdims=True)
    m_scr[...] = m_new
    pb = pw.astype(BF16)
    for c in range(n_heads // 2):
        vp = jnp.concatenate(
            [jnp.concatenate([kv_refs[2 * i + 1][2 * c], kv_refs[2 * i + 1][2 * c + 1]], axis=-1)
             for i in range(pps)], axis=0).astype(BF16)
        sl = slice(c * pair, (c + 1) * pair)
        acc_scr[:, sl] = alpha * acc_scr[:, sl] + jnp.dot(pb, vp, preferred_element_type=F32)

    @pl.when(step == n_pages // pps - 1)
    def _():
        o_ref[...] = jnp.sum(jnp.where(diag, acc_scr[...] / l_scr[...], 0.0), axis=0, keepdims=True)


DEC_PAGES_PER_STEP = 8


def _dec_fox(q, k_new, v_new, lf_new_col, cache_k, cache_v, cache_lf, page_table, layer):
    b, _, w = q.shape
    n_pages = page_table.shape[1]
    n_heads, page = cache_k.shape[1:3]
    n_pool = cache_lf.shape[1]
    pps = DEC_PAGES_PER_STEP
    assert n_pages % pps == 0 and n_heads % 2 == 0 and n_pool % SUBLANE == 0

    def page_id(bb, p, pt, i):
        return pt[bb, n_pages - 1 - (p * pps + i)]

    row_map = lambda bb, p, pt: (bb, 0, 0)
    row_spec = pl.BlockSpec((None, 1, w), row_map)
    kv_specs, lf_specs = [], []
    for i in range(pps):
        kv_map = lambda bb, p, pt, i=i: (page_id(bb, p, pt, i) + layer * n_pool, 0, 0, 0)
        kv_specs += [pl.BlockSpec((None, n_heads, page, LANE), kv_map)] * 2
        lf_specs.append(pl.BlockSpec((n_heads, SUBLANE, page),
                                     lambda bb, p, pt, i=i: (layer, page_id(bb, p, pt, i) // SUBLANE, 0)))
    kern = functools.partial(_dec_fox_kernel, scale=LANE ** -0.5 / Q_PRESCALE, n_pages=n_pages, pps=pps)
    grid_spec = pltpu.PrefetchScalarGridSpec(
        num_scalar_prefetch=1, grid=(b, n_pages // pps),
        in_specs=[row_spec, row_spec, row_spec, pl.BlockSpec((None, HEAD_ROWS, 1), row_map)]
        + kv_specs + lf_specs,
        out_specs=row_spec,
        scratch_shapes=[pltpu.VMEM((HEAD_ROWS, w), BF16), pltpu.VMEM((HEAD_ROWS, 1), F32),
                        pltpu.VMEM((HEAD_ROWS, 1), F32), pltpu.VMEM((HEAD_ROWS, w), F32),
                        pltpu.VMEM((HEAD_ROWS, 1), F32)])
    return pl.pallas_call(
        kern, grid_spec=grid_spec, out_shape=jax.ShapeDtypeStruct((b, 1, w), F32),
        compiler_params=_params(("arbitrary", "arbitrary")), name="fox_decode_attn")(
            page_table, q, k_new, v_new, lf_new_col, *([cache_k, cache_v] * pps), *([cache_lf] * pps))


def _dec_xattn_kernel(q_ref, k_ref, v_ref, o_ref, *, scale, n_heads):
    for b in range(q_ref.shape[0]):
        s = lax.dot_general(q_ref[b].astype(BF16), k_ref[b].astype(BF16), NT_DIMS,
                            preferred_element_type=F32) * scale
        row = lax.broadcasted_iota(jnp.int32, s.shape, 0)
        col = lax.broadcasted_iota(jnp.int32, s.shape, 1)
        logits = jnp.where(col % n_heads == row, s, NEG_INF)
        p = jnp.exp(logits - jnp.max(logits, axis=-1, keepdims=True))
        p = p / jnp.sum(p, axis=-1, keepdims=True)
        o_ref[b] = jnp.dot(p.astype(BF16), v_ref[b].astype(BF16), preferred_element_type=F32)


DEC_XATTN_SEQS_PER_STEP = 4


def _dec_xattn(q, mem_k, mem_v, layer, n_heads):
    b = q.shape[0]
    n_rows = mem_k.shape[2]
    spb = DEC_XATTN_SEQS_PER_STEP
    assert b % spb == 0
    kern = functools.partial(_dec_xattn_kernel, scale=LANE ** -0.5, n_heads=n_heads)
    head_spec = pl.BlockSpec((spb, HEAD_ROWS, LANE), lambda bb: (bb, 0, 0))
    mem_spec = pl.BlockSpec((None, spb, n_rows, LANE), lambda bb: (layer, bb, 0, 0))
    return pl.pallas_call(
        kern, grid=(b // spb,), in_specs=[head_spec, mem_spec, mem_spec], out_specs=head_spec,
        out_shape=jax.ShapeDtypeStruct((b, HEAD_ROWS, LANE), F32),
        compiler_params=_params(("arbitrary",)), name="mem_xattn_decode")(q, mem_k, mem_v)


def _outproj_kernel(a_ref, xa_ref, w_ref, x_ref, o_ref):
    lhs = jnp.concatenate([a_ref[...].astype(BF16), xa_ref[...].astype(BF16)], axis=-1)
    o_ref[...] = x_ref[...] + jnp.dot(lhs, w_ref[...], preferred_element_type=F32)


def _outproj(a, xa, w, layer, x, *, tm, tn):
    m, d = x.shape
    ka, kx = a.shape[1], xa.shape[1]
    return pl.pallas_call(
        _outproj_kernel, grid=(m // tm, d // tn),
        in_specs=[pl.BlockSpec((tm, ka), lambda i, j: (i, 0)),
                  pl.BlockSpec((tm, kx), lambda i, j: (i, 0)),
                  pl.BlockSpec((None, ka + kx, tn), lambda i, j: (layer, 0, j)),
                  pl.BlockSpec((tm, tn), lambda i, j: (i, j))],
        out_specs=pl.BlockSpec((tm, tn), lambda i, j: (i, j)),
        out_shape=jax.ShapeDtypeStruct((m, d), F32),
        compiler_params=_params(("arbitrary", "arbitrary")), name="out_proj")(a, xa, w, x)


def _ffn_kernel(x_ref, g_ref, wg_ref, wu_ref, wd_ref, o_ref, h_scr):
    f = pl.program_id(1)

    @pl.when(f == 0)
    def _():
        x = x_ref[...]
        h = x * lax.rsqrt(jnp.mean(x * x, axis=-1, keepdims=True) + EPS) * g_ref[...]
        h_scr[...] = h.astype(BF16)
        o_ref[...] = x

    h = h_scr[...]
    gate = jnp.dot(h, wg_ref[...], preferred_element_type=F32)
    up = jnp.dot(h, wu_ref[...], preferred_element_type=F32)
    act = (gate * jax.nn.sigmoid(gate)) * up
    o_ref[...] += jnp.dot(act.astype(BF16), wd_ref[...], preferred_element_type=F32)


def _ffn(x, g, w_gate, w_up, w_down, layer, *, tm, tf):
    m, d = x.shape
    d_ff = w_gate.shape[-1]
    return pl.pallas_call(
        _ffn_kernel, grid=(m // tm, d_ff // tf),
        in_specs=[pl.BlockSpec((tm, d), lambda i, f: (i, 0)),
                  pl.BlockSpec((1, d), lambda i, f: (0, 0)),
                  pl.BlockSpec((None, d, tf), lambda i, f: (layer, 0, f)),
                  pl.BlockSpec((None, d, tf), lambda i, f: (layer, 0, f)),
                  pl.BlockSpec((None, tf, d), lambda i, f: (layer, f, 0))],
        out_specs=pl.BlockSpec((tm, d), lambda i, f: (i, 0)),
        out_shape=jax.ShapeDtypeStruct((m, d), F32),
        scratch_shapes=[pltpu.VMEM((tm, d), BF16)],
        compiler_params=_params(("arbitrary", "arbitrary"), VMEM_LIMIT_BIG), name="swiglu_ffn")(
            x, g.reshape(1, d), w_gate, w_up, w_down)


S5_CHUNK = 2 * LANE


def _gelu_tanh(y):
    return 0.5 * y * (1.0 + jnp.tanh(math.sqrt(2.0 / math.pi) * (y + 0.044715 * (y * y * y))))


def _s5_prompt_kernel(u_ref, bbre_ref, bbim_ref, cre_ref, cimn_ref, are_ref, aim_ref, d_ref,
                      wglu_ref, bglu_ref, mix_ref, hre_ref, him_ref,
                      sre, sim, hre_s, him_s, y_scr, *, lt, n_chunks):
    t = pl.program_id(1)
    tiles_per_chunk = S5_CHUNK // LANE
    chunks_per_row = are_ref.shape[0] // tiles_per_chunk
    chunks_per_tile = S5_CHUNK // (u_ref.shape[1] // n_chunks)

    @pl.when(t == 0)
    def _():
        hre_s[...] = jnp.zeros_like(hre_s)
        him_s[...] = jnp.zeros_like(him_s)

    u = u_ref[...]
    ub = u.astype(BF16)

    def put(dst, q, val):
        j, lc = divmod(q, chunks_per_row)
        for k in range(tiles_per_chunk):
            dst[lc * tiles_per_chunk + k, pl.ds(j, lt, stride=SUBLANE), :] = val[:, k * LANE:(k + 1) * LANE]

    def get(src, q):
        j, lc = divmod(q, chunks_per_row)
        return jnp.concatenate([src[lc * tiles_per_chunk + k, pl.ds(j, lt, stride=SUBLANE), :]
                                for k in range(tiles_per_chunk)], axis=-1)

    for q in range(n_chunks):
        ut = ub[:, (q // 2) * LANE:(q // 2 + 1) * LANE]
        put(sre, q, jnp.dot(ut, bbre_ref[q], preferred_element_type=F32))
        put(sim, q, jnp.dot(ut, bbim_ref[q], preferred_element_type=F32))

    a_re = are_ref[...]
    a_im = aim_ref[...]

    def body(tt, carry):
        h_re, h_im = carry
        r0 = pl.multiple_of(tt * SUBLANE, SUBLANE)
        n_re = a_re * h_re - a_im * h_im + sre[:, pl.ds(r0, SUBLANE), :]
        n_im = a_re * h_im + a_im * h_re + sim[:, pl.ds(r0, SUBLANE), :]
        sre[:, pl.ds(r0, SUBLANE), :] = n_re
        sim[:, pl.ds(r0, SUBLANE), :] = n_im
        return n_re, n_im

    h_re, h_im = lax.fori_loop(0, lt, body, (hre_s[...], him_s[...]), unroll=8)
    hre_s[...] = h_re
    him_s[...] = h_im
    hre_ref[...] = h_re
    him_ref[...] = h_im

    for c in range(n_chunks // chunks_per_tile):
        acc = jnp.zeros((lt, S5_CHUNK), F32)
        for q in range(c * chunks_per_tile, (c + 1) * chunks_per_tile):
            acc += jnp.dot(get(sre, q).astype(BF16), cre_ref[q], preferred_element_type=F32)
            acc += jnp.dot(get(sim, q).astype(BF16), cimn_ref[q], preferred_element_type=F32)
        sl = slice(c * S5_CHUNK, (c + 1) * S5_CHUNK)
        y_scr[:, sl] = _gelu_tanh(acc + d_ref[:, sl] * u[:, sl])

    y = y_scr[...]
    gate = jax.nn.sigmoid(jnp.dot(y.astype(BF16), wglu_ref[...], preferred_element_type=F32)
                          + bglu_ref[...])
    mix_ref[...] = (y * gate).astype(mix_ref.dtype)


def _s5_prompt(u, pk, w_glu, b_glu, *, lt):
    b, t, w = u.shape
    n_chunks = pk["bb_re"].shape[0]
    st_shape = pk["a_re8"].shape
    full = lambda a: pl.BlockSpec(a.shape, lambda bb, tt, nd=a.ndim: (0,) * nd)
    consts = [pk["bb_re"], pk["bb_im"], pk["c_re"], pk["c_imn"], pk["a_re8"], pk["a_im8"], pk["d_row"],
              w_glu, b_glu]
    kern = functools.partial(_s5_prompt_kernel, lt=lt, n_chunks=n_chunks)
    st_spec = pl.BlockSpec((None,) + st_shape, lambda bb, tt: (bb, 0, 0, 0))
    return pl.pallas_call(
        kern, grid=(b, t // lt),
        in_specs=[pl.BlockSpec((None, lt, w), lambda bb, tt: (bb, tt, 0))] + [full(a) for a in consts],
        out_specs=[pl.BlockSpec((None, lt, w), lambda bb, tt: (bb, tt, 0)), st_spec, st_spec],
        out_shape=[jax.ShapeDtypeStruct((b, t, w), BF16),
                   jax.ShapeDtypeStruct((b,) + st_shape, F32),
                   jax.ShapeDtypeStruct((b,) + st_shape, F32)],
        scratch_shapes=[pltpu.VMEM((st_shape[0], lt * SUBLANE, LANE), F32),
                        pltpu.VMEM((st_shape[0], lt * SUBLANE, LANE), F32),
                        pltpu.VMEM(st_shape, F32), pltpu.VMEM(st_shape, F32),
                        pltpu.VMEM((lt, w), F32)],
        compiler_params=_params(("arbitrary", "arbitrary")), name="s5_prompt")(u, *consts)


def _s5_sample_kernel(u_ref, h0re_ref, h0im_ref, bbre_ref, bbim_ref, cre_ref, cimn_ref, are_ref, aim_ref,
                      d_ref, wglu_ref, bglu_ref, mix_ref, hre_ref, him_ref, *, n_chunks):
    u = u_ref[...]
    ub = u.astype(BF16)
    chunks_per_tile = S5_CHUNK // (u.shape[1] // n_chunks)
    ys = []
    for c in range(n_chunks // chunks_per_tile):
        acc = jnp.zeros((u.shape[0], S5_CHUNK), F32)
        for q in range(c * chunks_per_tile, (c + 1) * chunks_per_tile):
            ut = ub[:, (q // 2) * LANE:(q // 2 + 1) * LANE]
            sl = slice(q * S5_CHUNK, (q + 1) * S5_CHUNK)
            a_re, a_im = are_ref[:, sl], aim_ref[:, sl]
            h_re, h_im = h0re_ref[:, sl], h0im_ref[:, sl]
            n_re = a_re * h_re - a_im * h_im + jnp.dot(ut, bbre_ref[q], preferred_element_type=F32)
            n_im = a_re * h_im + a_im * h_re + jnp.dot(ut, bbim_ref[q], preferred_element_type=F32)
            hre_ref[:, sl] = n_re
            him_ref[:, sl] = n_im
            acc += jnp.dot(n_re.astype(BF16), cre_ref[q], preferred_element_type=F32)
            acc += jnp.dot(n_im.astype(BF16), cimn_ref[q], preferred_element_type=F32)
        cs = slice(c * S5_CHUNK, (c + 1) * S5_CHUNK)
        ys.append(_gelu_tanh(acc + d_ref[:, cs] * u[:, cs]))
    y = jnp.concatenate(ys, axis=-1)
    gate = jax.nn.sigmoid(jnp.dot(y.astype(BF16), wglu_ref[...], preferred_element_type=F32)
                          + bglu_ref[...])
    mix_ref[...] = (y * gate).astype(mix_ref.dtype)


def _s5_sample(u, h0_re, h0_im, pk, w_glu, b_glu):
    b, w = u.shape
    n_state = h0_re.shape[1]
    args = [u, h0_re, h0_im, pk["bb_re"], pk["bb_im"], pk["c_re"], pk["c_imn"], pk["a_re1"], pk["a_im1"],
            pk["d_row"], w_glu, b_glu]
    kern = functools.partial(_s5_sample_kernel, n_chunks=pk["bb_re"].shape[0])
    return pl.pallas_call(
        kern,
        out_shape=[jax.ShapeDtypeStruct((b, w), BF16), jax.ShapeDtypeStruct((b, n_state), F32),
                   jax.ShapeDtypeStruct((b, n_state), F32)],
        compiler_params=pltpu.CompilerParams(vmem_limit_bytes=VMEM_LIMIT), name="s5_sample")(*args)


def _state_tiles(a):
    return a.reshape(SUBLANE, -1, LANE).transpose(1, 0, 2)


def _state_untile(h, g, n):
    return h.transpose(0, 2, 1, 3).reshape(h.shape[0], g, n)


def _s5_disc_kernel(are_ref, aim_ref, ldt_ref, bre_ref, bim_ref, abre_ref, abim_ref, bbre_ref, bbim_ref):
    a_re, a_im = are_ref[...], aim_ref[...]
    dt = jnp.exp(ldt_ref[...])
    mag = jnp.exp(a_re * dt)
    ab_re = mag * jnp.cos(a_im * dt)
    ab_im = mag * jnp.sin(a_im * dt)
    num_re, num_im = ab_re - 1.0, ab_im
    den = a_re * a_re + a_im * a_im
    z_re = (num_re * a_re + num_im * a_im) / den
    z_im = (num_im * a_re - num_re * a_im) / den
    abre_ref[...] = ab_re
    abim_ref[...] = ab_im
    b_re, b_im = bre_ref[...], bim_ref[...]
    bbre_ref[...] = z_re * b_re - z_im * b_im
    bbim_ref[...] = z_re * b_im + z_im * b_re


def _s5_pack(a_re, a_im, log_dt, b_re, b_im, c_re, c_im, d):
    g, n = a_re.shape
    p = d.shape[1]
    ab_re, ab_im, bb_re, bb_im = pl.pallas_call(
        _s5_disc_kernel,
        out_shape=[jax.ShapeDtypeStruct((g, 1, n), F32)] * 2 + [jax.ShapeDtypeStruct((g, p, n), F32)] * 2,
        name="s5_discretise")(a_re.reshape(g, 1, n), a_im.reshape(g, 1, n), log_dt.reshape(g, 1, 1),
                              b_re.transpose(0, 2, 1), b_im.transpose(0, 2, 1))
    ab_re, ab_im = ab_re.reshape(g, n), ab_im.reshape(g, n)
    gpc = S5_CHUNK // n
    n_chunks = g // gpc
    eye = jnp.eye(gpc, dtype=F32)
    chunks_per_in_tile = LANE // (gpc * p)
    chunks_per_out_tile = S5_CHUNK // (gpc * p)

    def pack_in(bb):
        blk = jnp.einsum("qgpn,gh->qgphn", bb.reshape(n_chunks, gpc, p, n), eye)
        blk = blk.reshape(n_chunks, gpc * p, S5_CHUNK)
        sel = jax.nn.one_hot(jnp.arange(n_chunks) % chunks_per_in_tile, chunks_per_in_tile, dtype=F32)
        return jnp.einsum("qrc,qs->qsrc", blk, sel).reshape(n_chunks, LANE, S5_CHUNK).astype(BF16)

    def pack_out(cc):
        blk = jnp.einsum("qgnp,gh->qgnhp", cc.transpose(0, 2, 1).reshape(n_chunks, gpc, n, p), eye)
        blk = blk.reshape(n_chunks, S5_CHUNK, gpc * p)
        sel = jax.nn.one_hot(jnp.arange(n_chunks) % chunks_per_out_tile, chunks_per_out_tile, dtype=F32)
        return jnp.einsum("qrc,qs->qrsc", blk, sel).reshape(n_chunks, S5_CHUNK, S5_CHUNK).astype(BF16)

    return {
        "bb_re": pack_in(bb_re), "bb_im": pack_in(bb_im),
        "c_re": pack_out(c_re), "c_imn": pack_out(-c_im),
        "a_re8": _state_tiles(ab_re), "a_im8": _state_tiles(ab_im),
        "a_re1": ab_re.reshape(1, g * n), "a_im1": ab_im.reshape(1, g * n),
        "d_row": d.reshape(1, g * p),
    }


TM = 1024
TN = 512
TM_FFN = 1024
TF = 512
TM_OUT = 512
TQ = 512
S5_LT = 256


def _gain_row(*pieces):
    return jnp.concatenate([jnp.tile(g, reps) for g, reps in pieces]).reshape(1, -1).astype(F32)


def _head_rows(x, n_heads):
    x = x.reshape(x.shape[0], n_heads, LANE)
    return jnp.pad(x, ((0, 0), (0, HEAD_ROWS - n_heads), (0, 0)))


def kernel(x_prompt, x_sample, cache_fox_k, cache_fox_v, cache_fox_logf, state_s5_re, state_s5_im, cache_mem_k, cache_mem_v, page_table, mem_prompt, norm1_g, w_out, mem_norm_g, w_mem_kv, xq_norm_g, xk_norm_g, norm2_g, w_ffn_gate, w_ffn_up, w_ffn_down, fox_w_in, fox_b_f, fox_q_norm_g, fox_k_norm_g, s5_w_in, s5_a_re, s5_a_im, s5_log_dt, s5_b_re, s5_b_im, s5_c_re, s5_c_im, s5_d, s5_w_glu, s5_b_glu):
    bp, tp, d = x_prompt.shape
    bs, ts, _ = x_sample.shape
    depth = norm1_g.shape[0]
    n_fox, n_pool, page, fox_h, fox_hd = cache_fox_k.shape
    fox_w = fox_h * fox_hd
    n_mem, xa_h, xa_hd = cache_mem_k.shape[2:]
    xa_w = xa_h * xa_hd
    s5_g, s5_n = s5_a_re.shape[1:]
    mp, ms = bp * tp, bs * ts
    assert ts == 1 and fox_hd == LANE and xa_hd == LANE

    xp = x_prompt.reshape(mp, d)
    xs = x_sample.reshape(ms, d)
    mem = mem_prompt.reshape(bp * n_mem, d)
    ck = cache_fox_k.transpose(0, 1, 3, 2, 4).reshape(n_fox * n_pool, fox_h, page, fox_hd)
    cv = cache_fox_v.transpose(0, 1, 3, 2, 4).reshape(n_fox * n_pool, fox_h, page, fox_hd)
    clf = cache_fox_logf.transpose(0, 3, 1, 2).reshape(n_fox * fox_h, n_pool, page)
    cmk = cache_mem_k.reshape(depth, bs, n_mem * xa_h, xa_hd)
    cmv = cache_mem_v.reshape(depth, bs, n_mem * xa_h, xa_hd)

    outs = {k: [] for k in ("fk_p", "fv_p", "fl_p", "fk_s", "fv_s", "fl_s",
                            "sr_p", "si_p", "sr_s", "si_s", "mk_p", "mv_p")}
    fox_nt, xa_nt = fox_w // TN, xa_w // TN

    wo16 = w_out.astype(BF16)
    wkv16 = w_mem_kv.astype(BF16)
    wg16, wu16, wd16 = w_ffn_gate.astype(BF16), w_ffn_up.astype(BF16), w_ffn_down.astype(BF16)

    for i in range(depth):
        j = i // 2

        gain = _gain_row((xk_norm_g[i], xa_h), (jnp.ones((xa_hd,), F32), xa_h))
        segs = ((0, xa_nt, True, (0, 1)), (xa_nt, xa_nt, False, (2, 3)))
        defs = ((F32, 0, xa_nt, 0), (BF16, 0, xa_nt, 0), (F32, xa_nt, xa_nt, 0), (BF16, xa_nt, xa_nt, 0))
        mk32, mk16, mv32, mv16 = _proj(mem, mem_norm_g[i], wkv16, gain, segs, defs,
                                       tm=bp * n_mem, tn=TN, layer=i)
        outs["mk_p"].append(mk32.reshape(bp, n_mem, xa_h, xa_hd))
        outs["mv_p"].append(mv32.reshape(bp, n_mem, xa_h, xa_hd))

        if i % 2 == 0:
            w_in = fox_w_in[j]
            w_main = jnp.concatenate([w_in[:, :3 * fox_w], w_in[:, 3 * fox_w + fox_h:]], axis=1).astype(BF16)
            wf = jnp.pad(w_in[:, 3 * fox_w:3 * fox_w + fox_h], ((0, 0), (0, LANE - fox_h))).astype(BF16)
            bf = jnp.pad(fox_b_f[j], (0, LANE - fox_h)).reshape(1, LANE)
            gain = _gain_row((fox_q_norm_g[j] * Q_PRESCALE, fox_h), (fox_k_norm_g[j], fox_h),
                             (jnp.ones((fox_hd,), F32), fox_h), (xq_norm_g[i], xa_h))
            segs = ((0, fox_nt, True, (0,)), (fox_nt, fox_nt, True, (1, 2)),
                    (2 * fox_nt, fox_nt, False, (3, 4)), (3 * fox_nt, xa_nt, True, (5,)))
            defs = ((BF16, 0, fox_nt, 0), (F32, fox_nt, fox_nt, page), (BF16, fox_nt, fox_nt, 0),
                    (F32, 2 * fox_nt, fox_nt, page), (BF16, 2 * fox_nt, fox_nt, 0),
                    (BF16, 3 * fox_nt, xa_nt, 0))
            q16, k32, k16, v32, v16, xq16, lf_p, qa, ka = _proj(
                xp, norm1_g[i], w_main, gain, segs, defs, tm=TM, tn=TN, wf=wf, bf=bf, seq_len=tp,
                n_heads=fox_h)
            outs["fk_p"].append(k32.transpose(0, 2, 1, 3))
            outs["fv_p"].append(v32.transpose(0, 2, 1, 3))
            outs["fl_p"].append(lf_p[:, :fox_h].reshape(mp // page, page, fox_h))
            mix_p = _flash(q16.reshape(bp, tp, fox_w), qa.reshape(bp, tp, LANE), k16.reshape(bp, tp, fox_w),
                           ka.reshape(bp, tp, LANE), v16.reshape(bp, tp, fox_w), tq=TQ).reshape(mp, fox_w)

            defs_s = ((F32, 0, fox_nt, 0), (F32, fox_nt, fox_nt, 0),
                      (F32, 2 * fox_nt, fox_nt, 0), (F32, 3 * fox_nt, xa_nt, 0))
            segs_s = ((0, fox_nt, True, (0,)), (fox_nt, fox_nt, True, (1,)),
                      (2 * fox_nt, fox_nt, False, (2,)), (3 * fox_nt, xa_nt, True, (3,)))
            qs, ks, vs, xqs, lf_s = _proj(xs, norm1_g[i], w_main, gain, segs_s, defs_s,
                                          tm=ms, tn=TN, wf=wf, bf=bf)
            outs["fk_s"].append(ks.reshape(bs, ts, fox_h, fox_hd))
            outs["fv_s"].append(vs.reshape(bs, ts, fox_h, fox_hd))
            outs["fl_s"].append(lf_s[:, :fox_h].reshape(bs, ts, fox_h))
            lf_col = lf_s[:, :HEAD_ROWS].reshape(bs, HEAD_ROWS, 1)
            mix_s = _dec_fox(qs.reshape(bs, 1, fox_w), ks.reshape(bs, 1, fox_w), vs.reshape(bs, 1, fox_w),
                             lf_col, ck, cv, clf, page_table, j).reshape(ms, fox_w)
        else:
            w_main = s5_w_in[j].astype(BF16)
            s5_nt = (w_main.shape[1] - xa_w) // TN
            gain = _gain_row((jnp.ones((LANE,), F32), s5_nt * TN // LANE), (xq_norm_g[i], xa_h))
            segs = ((0, s5_nt, False, (0,)), (s5_nt, xa_nt, True, (1,)))
            pk = _s5_pack(s5_a_re[j], s5_a_im[j], s5_log_dt[j], s5_b_re[j], s5_b_im[j],
                          s5_c_re[j], s5_c_im[j], s5_d[j])
            w_glu = s5_w_glu[j].astype(BF16)
            b_glu = s5_b_glu[j].reshape(1, -1)
            u_p, xq16 = _proj(xp, norm1_g[i], w_main, gain, segs,
                              ((F32, 0, s5_nt, 0), (BF16, s5_nt, xa_nt, 0)), tm=TM, tn=TN)
            s5_w = u_p.shape[1]
            mix_p, hr, hi = _s5_prompt(u_p.reshape(bp, tp, s5_w), pk, w_glu, b_glu, lt=S5_LT)
            mix_p = mix_p.reshape(mp, s5_w)
            outs["sr_p"].append(_state_untile(hr, s5_g, s5_n))
            outs["si_p"].append(_state_untile(hi, s5_g, s5_n))
            u_s, xqs = _proj(xs, norm1_g[i], w_main, gain, segs,
                             ((F32, 0, s5_nt, 0), (F32, s5_nt, xa_nt, 0)), tm=ms, tn=TN)
            mix_s, hr, hi = _s5_sample(u_s, state_s5_re[j].reshape(bs, s5_g * s5_n),
                                       state_s5_im[j].reshape(bs, s5_g * s5_n), pk, w_glu, b_glu)
            outs["sr_s"].append(hr.reshape(bs, s5_g, s5_n))
            outs["si_s"].append(hi.reshape(bs, s5_g, s5_n))

        xa_p = _xattn(xq16.reshape(bp, tp, xa_w), mk16.reshape(bp, n_mem, xa_w),
                      mv16.reshape(bp, n_mem, xa_w), tq=TQ).reshape(mp, xa_w)
        xa_s = _dec_xattn(_head_rows(xqs, xa_h), cmk, cmv, i, xa_h)[:, :xa_h].reshape(ms, xa_w)

        xp = _outproj(mix_p, xa_p, wo16, i, xp, tm=TM_OUT, tn=d)
        xs = _outproj(mix_s, xa_s, wo16, i, xs, tm=ms, tn=d)
        xp = _ffn(xp, norm2_g[i], wg16, wu16, wd16, i, tm=TM_FFN, tf=TF)
        xs = _ffn(xs, norm2_g[i], wg16, wu16, wd16, i, tm=ms, tf=TF)

    st = lambda k: jnp.stack(outs[k])
    return (xp.reshape(bp, tp, d), xs.reshape(bs, ts, d),
            st("fk_p"), st("fv_p"), st("fl_p"), st("fk_s"), st("fv_s"), st("fl_s"),
            st("sr_p"), st("si_p"), st("sr_s"), st("si_s"), st("mk_p"), st("mv_p"))
```

```python
import functools
import math

import jax
import jax.numpy as jnp
import numpy as np
from jax import lax
from jax.experimental import pallas as pl
from jax.experimental.pallas import tpu as pltpu

F32 = jnp.float32
BF16 = jnp.bfloat16

EPS = 1e-6
NEG_INF = -1e30
LANE = 128
SUBLANE = 8
HEAD_ROWS = 16
VMEM_LIMIT = 52 * 1024 * 1024
VMEM_LIMIT_BIG = 58 * 1024 * 1024
NT_DIMS = (((1,), (1,)), ((), ()))


def _params(sem, vmem=VMEM_LIMIT):
    return pltpu.CompilerParams(dimension_semantics=sem, vmem_limit_bytes=vmem)


def _split3(x):
    hi = x.astype(BF16)
    r1 = x - hi.astype(F32)
    mid = r1.astype(BF16)
    lo = (r1 - mid.astype(F32)).astype(BF16)
    return hi, mid, lo


def _dot_exact01(x, w01):
    r = jnp.dot(jnp.concatenate(_split3(x), axis=0), w01, preferred_element_type=F32)
    return r[0:HEAD_ROWS] + r[HEAD_ROWS:2 * HEAD_ROWS] + r[2 * HEAD_ROWS:3 * HEAD_ROWS]


def _log_sigmoid(x):
    return jnp.minimum(x, 0.0) - jnp.log1p(jnp.exp(-jnp.abs(x)))


def _head_rmsnorm(y, gain_row):
    parts = []
    for c in range(y.shape[-1] // LANE):
        p = y[:, c * LANE:(c + 1) * LANE]
        parts.append(p * lax.rsqrt(jnp.mean(p * p, axis=-1, keepdims=True) + EPS))
    return jnp.concatenate(parts, axis=-1) * gain_row


AUG_LANES = 8
SQRT2 = math.sqrt(2.0)
Q_PRESCALE = 0.125
CUMSUM_ROWS = 512


def _lane_split3(x):
    return jnp.concatenate(_split3(x), axis=-1)


def _lane_sum3(r):
    return r[:, 0:LANE] + r[:, LANE:2 * LANE] + r[:, 2 * LANE:3 * LANE]


def _proj_kernel(*refs, segs, n_out, with_f, cumsum, seq_tiles):
    x_ref, g_ref, w_ref, gain_ref = refs[:4]
    pos = 4
    if with_f:
        wf_ref, bf_ref = refs[4:6]
        pos = 6
        if cumsum:
            wqa_ref, wka_ref, oneq_ref, onek_ref = refs[6:10]
            pos = 10
    out_refs = refs[pos:pos + n_out]
    pos += n_out
    if with_f:
        lf_ref = refs[pos]
        pos += 1
        if cumsum:
            qa_ref, ka_ref = refs[pos:pos + 2]
            pos += 2
    h_scr = refs[pos]
    pos += 1
    if with_f and cumsum:
        tri_scr, carry_scr = refs[pos:pos + 2]

    i = pl.program_id(0)
    j = pl.program_id(1)
    tm = x_ref.shape[0]

    if with_f and cumsum:
        ct = tri_scr.shape[0]

        @pl.when((i == 0) & (j == 0))
        def _():
            t = lax.broadcasted_iota(jnp.int32, (ct, ct), 0)
            u = lax.broadcasted_iota(jnp.int32, (ct, ct), 1)
            tri_scr[...] = jnp.where(u <= t, 1.0, 0.0).astype(BF16)

    @pl.when(j == 0)
    def _():
        x = x_ref[...]
        h = x * lax.rsqrt(jnp.mean(x * x, axis=-1, keepdims=True) + EPS) * g_ref[...]
        hb = h.astype(BF16)
        h_scr[...] = hb
        if with_f:
            lf = _log_sigmoid(jnp.dot(hb, wf_ref[...], preferred_element_type=F32) + bf_ref[...])
            lf_ref[...] = lf
            if cumsum:
                @pl.when(i % seq_tiles == 0)
                def _():
                    carry_scr[...] = jnp.zeros_like(carry_scr)
                carry = carry_scr[...]
                for s in range(tm // ct):
                    rs = slice(s * ct, (s + 1) * ct)
                    c = _lane_sum3(jnp.dot(tri_scr[...], _lane_split3(lf[rs]), preferred_element_type=F32))
                    c = c + carry
                    carry = c[ct - 1:ct, :]
                    pieces = _lane_split3(c * SQRT2)
                    qa_ref[rs, :] = (jnp.dot(pieces, wqa_ref[...], preferred_element_type=F32)
                                     + oneq_ref[...]).astype(qa_ref.dtype)
                    ka_ref[rs, :] = (jnp.dot(pieces, wka_ref[...], preferred_element_type=F32)
                                     + onek_ref[...]).astype(ka_ref.dtype)
                carry_scr[...] = carry

    n_sub = 2 if tm % (2 * HEAD_ROWS) == 0 and tm >= 2 * LANE else 1
    rows = tm // n_sub
    for start, n_tiles, norm, outs in segs:
        @pl.when((j >= start) & (j < start + n_tiles))
        def _(norm=norm, outs=outs):
            for r in range(n_sub):
                rs = slice(r * rows, (r + 1) * rows)
                y = jnp.dot(h_scr[rs, :], w_ref[...], preferred_element_type=F32)
                yy = _head_rmsnorm(y, gain_ref[...]) if norm else y
                for o in outs:
                    if len(out_refs[o].shape) == 4:
                        page = out_refs[o].shape[2]
                        for pg in range(rows // page):
                            for hh in range(yy.shape[1] // LANE):
                                out_refs[o][r * (rows // page) + pg, hh] = (
                                    yy[pg * page:(pg + 1) * page, hh * LANE:(hh + 1) * LANE])
                    else:
                        out_refs[o][rs, :] = yy.astype(out_refs[o].dtype)


def _aug_maps(n_heads):
    wqa = np.zeros((3 * LANE, LANE), np.float32)
    wka = np.zeros((3 * LANE, LANE), np.float32)
    oneq = np.zeros((1, LANE), np.float32)
    onek = np.zeros((1, LANE), np.float32)
    for h in range(n_heads):
        for piece in range(3):
            wqa[piece * LANE + h, AUG_LANES * h + piece] = 1.0
            wka[piece * LANE + h, AUG_LANES * h + 3 + piece] = -1.0
            oneq[0, AUG_LANES * h + 3 + piece] = 1.0
            onek[0, AUG_LANES * h + piece] = 1.0
    return jnp.asarray(wqa, BF16), jnp.asarray(wka, BF16), jnp.asarray(oneq), jnp.asarray(onek)


def _proj(x, g, w, gain_row, segs, out_defs, *, tm, tn, layer=None, wf=None, bf=None, seq_len=None,
          n_heads=None):
    m, k = x.shape
    n = w.shape[-1]
    with_f = wf is not None
    cumsum = seq_len is not None
    grid = (m // tm, n // tn)
    const = lambda shape: pl.BlockSpec(shape, lambda i, j: (0,) * len(shape))
    w_spec = (pl.BlockSpec((k, tn), lambda i, j: (0, j)) if layer is None
              else pl.BlockSpec((None, k, tn), lambda i, j: (layer, 0, j)))
    in_specs = [pl.BlockSpec((tm, k), lambda i, j: (i, 0)), const((1, k)), w_spec,
                pl.BlockSpec((1, tn), lambda i, j: (0, j))]
    args = [x, g.reshape(1, k), w, gain_row]
    if with_f:
        in_specs += [const((k, LANE)), const((1, LANE))]
        args += [wf, bf]
        if cumsum:
            maps = _aug_maps(n_heads)
            in_specs += [const(a.shape) for a in maps]
            args += list(maps)
    out_shapes, out_specs = [], []
    for dt, start, n_tiles, page in out_defs:
        if page:
            out_shapes.append(jax.ShapeDtypeStruct((m // page, n_tiles * tn // LANE, page, LANE), dt))
            out_specs.append(pl.BlockSpec(
                (tm // page, tn // LANE, page, LANE),
                lambda i, j, s=start, nt=n_tiles: (i, jnp.clip(j - s, 0, nt - 1), 0, 0)))
        else:
            out_shapes.append(jax.ShapeDtypeStruct((m, n_tiles * tn), dt))
            out_specs.append(pl.BlockSpec(
                (tm, tn), lambda i, j, s=start, nt=n_tiles: (i, jnp.clip(j - s, 0, nt - 1))))
    scratch = [pltpu.VMEM((tm, k), BF16)]
    seq_tiles = 1
    if with_f:
        row_spec = pl.BlockSpec((tm, LANE), lambda i, j: (i, 0))
        out_shapes.append(jax.ShapeDtypeStruct((m, LANE), F32))
        out_specs.append(row_spec)
        if cumsum:
            seq_tiles = seq_len // tm
            ct = math.gcd(tm, CUMSUM_ROWS)
            out_shapes += [jax.ShapeDtypeStruct((m, LANE), BF16)] * 2
            out_specs += [row_spec] * 2
            scratch += [pltpu.VMEM((ct, ct), BF16), pltpu.VMEM((1, LANE), F32)]
    kern = functools.partial(_proj_kernel, segs=segs, n_out=len(out_defs), with_f=with_f,
                             cumsum=cumsum, seq_tiles=seq_tiles)
    return pl.pallas_call(
        kern, grid=grid, in_specs=in_specs, out_specs=out_specs, out_shape=out_shapes,
        scratch_shapes=scratch,
        compiler_params=_params(("arbitrary", "arbitrary"), VMEM_LIMIT_BIG if tm >= 1024 else VMEM_LIMIT),
        name="norm_proj")(*args)


LOG2_SCALE = math.log2(math.e) / SQRT2


def _flash_kernel(qi_ref, kj_ref, q_ref, qa_ref, k_ref, ka_ref, v_ref, o_ref,
                  qaug_scr, m_scr, l_scr, acc_scr):
    hg = pl.program_id(1)
    i = qi_ref[pl.program_id(2)]
    j = kj_ref[pl.program_id(2)]
    tq, tk = q_ref.shape[0], k_ref.shape[0]
    n_hp = q_ref.shape[1] // LANE

    @pl.when(j == 0)
    def _():
        lane = lax.broadcasted_iota(jnp.int32, (tq, LANE), 1)
        for hp in range(n_hp):
            qaug_scr[hp, :, 0:LANE] = q_ref[:, hp * LANE:(hp + 1) * LANE]
            qaug_scr[hp, :, LANE:2 * LANE] = jnp.where(lane // AUG_LANES == hg * n_hp + hp, qa_ref[...],
                                                       jnp.zeros_like(qa_ref))
        m_scr[...] = jnp.full_like(m_scr, NEG_INF)
        l_scr[...] = jnp.zeros_like(l_scr)
        acc_scr[...] = jnp.zeros_like(acc_scr)

    def step(masked):
        for hp in range(n_hp):
            hs = slice(hp * LANE, (hp + 1) * LANE)
            k_aug = jnp.concatenate([k_ref[:, hs], ka_ref[...]], axis=-1)
            x = lax.dot_general(qaug_scr[hp], k_aug, NT_DIMS, preferred_element_type=F32) * LOG2_SCALE
            if masked:
                row = lax.broadcasted_iota(jnp.int32, x.shape, 0)
                col = lax.broadcasted_iota(jnp.int32, x.shape, 1)
                x = jnp.where(col <= row, x, NEG_INF)
            m_old = m_scr[hp]
            m_new = jnp.maximum(m_old, jnp.max(x, axis=-1, keepdims=True))
            alpha = jnp.exp2(m_old - m_new)
            l_part = alpha * l_scr[hp]
            ps = []
            for c in range(tk // LANE):
                p = jnp.exp2(x[:, c * LANE:(c + 1) * LANE] - m_new)
                l_part = l_part + p
                ps.append(p.astype(BF16))
            l_scr[hp] = l_part
            m_scr[hp] = m_new
            acc_scr[hp] = alpha * acc_scr[hp] + jnp.dot(jnp.concatenate(ps, axis=-1), v_ref[:, hs],
                                                        preferred_element_type=F32)

    @pl.when(j < i)
    def _():
        step(False)

    @pl.when(j == i)
    def _():
        step(True)
        for hp in range(n_hp):
            o_ref[:, hp * LANE:(hp + 1) * LANE] = (
                acc_scr[hp] / jnp.sum(l_scr[hp], axis=-1, keepdims=True)).astype(o_ref.dtype)


FLASH_HEADS_PER_STEP = 12


def _flash(q, qa, k, ka, v, *, tq):
    b, t, w = q.shape
    n_hp = FLASH_HEADS_PER_STEP
    hw = n_hp * LANE
    assert w % hw == 0
    nq = t // tq
    pairs = [(i, j) for i in range(nq) for j in range(i + 1)]
    qi = jnp.asarray([p[0] for p in pairs], jnp.int32)
    kj = jnp.asarray([p[1] for p in pairs], jnp.int32)
    q_map = lambda bb, hh, s, qi, kj: (bb, qi[s], hh)
    kv_map = lambda bb, hh, s, qi, kj: (bb, kj[s], hh)
    grid_spec = pltpu.PrefetchScalarGridSpec(
        num_scalar_prefetch=2, grid=(b, w // hw, len(pairs)),
        in_specs=[
            pl.BlockSpec((None, tq, hw), q_map),
            pl.BlockSpec((None, tq, LANE), lambda bb, hh, s, qi, kj: (bb, qi[s], 0)),
            pl.BlockSpec((None, tq, hw), kv_map),
            pl.BlockSpec((None, tq, LANE), lambda bb, hh, s, qi, kj: (bb, kj[s], 0)),
            pl.BlockSpec((None, tq, hw), kv_map),
        ],
        out_specs=pl.BlockSpec((None, tq, hw), q_map),
        scratch_shapes=[pltpu.VMEM((n_hp, tq, 2 * LANE), BF16), pltpu.VMEM((n_hp, tq, LANE), F32),
                        pltpu.VMEM((n_hp, tq, LANE), F32), pltpu.VMEM((n_hp, tq, LANE), F32)])
    return pl.pallas_call(
        _flash_kernel, grid_spec=grid_spec, out_shape=jax.ShapeDtypeStruct((b, t, w), BF16),
        compiler_params=_params(("arbitrary",) * 3), name="fox_prompt_attn")(qi, kj, q, qa, k, ka, v)


def _xattn_kernel(q_ref, k_ref, v_ref, o_ref, *, scale):
    for h in range(q_ref.shape[-1] // LANE):
        sl = slice(h * LANE, (h + 1) * LANE)
        s = lax.dot_general(q_ref[:, sl], k_ref[:, sl], NT_DIMS, preferred_element_type=F32) * scale
        p = jnp.exp(s - jnp.max(s, axis=-1, keepdims=True))
        p = p / jnp.sum(p, axis=-1, keepdims=True)
        o_ref[:, sl] = jnp.dot(p.astype(BF16), v_ref[:, sl],
                               preferred_element_type=F32).astype(o_ref.dtype)


def _xattn(q, k, v, *, tq):
    b, t, w = q.shape
    n_mem = k.shape[1]
    kern = functools.partial(_xattn_kernel, scale=LANE ** -0.5)
    return pl.pallas_call(
        kern, grid=(b, t // tq),
        in_specs=[pl.BlockSpec((None, tq, w), lambda bb, i: (bb, i, 0)),
                  pl.BlockSpec((None, n_mem, w), lambda bb, i: (bb, 0, 0)),
                  pl.BlockSpec((None, n_mem, w), lambda bb, i: (bb, 0, 0))],
        out_specs=pl.BlockSpec((None, tq, w), lambda bb, i: (bb, i, 0)),
        out_shape=jax.ShapeDtypeStruct((b, t, w), BF16),
        compiler_params=_params(("arbitrary", "arbitrary")), name="mem_xattn_prompt")(q, k, v)


def _head_diag(width):
    rows = lax.broadcasted_iota(jnp.int32, (HEAD_ROWS, width), 0)
    lanes = lax.broadcasted_iota(jnp.int32, (HEAD_ROWS, width), 1)
    return (lanes // LANE) == rows


def _dec_fox_kernel(pt_ref, q_ref, kn_ref, vn_ref, lfn_ref, *refs, scale, n_pages, pps):
    kv_refs, lf_refs = refs[:2 * pps], refs[2 * pps:3 * pps]
    o_ref, qbd_scr, m_scr, l_scr, acc_scr, carry_scr = refs[3 * pps:]
    b = pl.program_id(0)
    step = pl.program_id(1)
    n_heads, keys, _ = kv_refs[0].shape
    width = n_heads * LANE
    pair = 2 * LANE
    diag = _head_diag(width)

    @pl.when(step == 0)
    def _():
        qbd = jnp.where(diag, jnp.broadcast_to(q_ref[...], (HEAD_ROWS, width)), 0.0).astype(BF16)
        qbd_scr[...] = qbd
        kn = kn_ref[...].astype(BF16).astype(F32)
        m_scr[...] = jnp.sum(qbd.astype(F32) * kn, axis=-1, keepdims=True) * scale
        l_scr[...] = jnp.ones_like(l_scr)
        acc_scr[...] = jnp.broadcast_to(vn_ref[...].astype(BF16).astype(F32), (HEAD_ROWS, width))
        carry_scr[...] = lfn_ref[...]

    u = lax.broadcasted_iota(jnp.int32, (keys, keys), 0)
    kk = lax.broadcasted_iota(jnp.int32, (keys, keys), 1)
    suffix = jnp.where(u >= kk, 1.0, 0.0).astype(BF16)
    sel_h = lax.broadcasted_iota(jnp.int32, (HEAD_ROWS, n_heads * SUBLANE), 0)
    sel_c = lax.broadcasted_iota(jnp.int32, (HEAD_ROWS, n_heads * SUBLANE), 1)

    scores, incls, lfts = [], [], []
    for i in range(pps):
        k_ref, lf_ref = kv_refs[2 * i], lf_refs[i]
        s = jnp.zeros((HEAD_ROWS, keys), F32)
        for c in range(n_heads // 2):
            kp = jnp.concatenate([k_ref[2 * c], k_ref[2 * c + 1]], axis=-1).astype(BF16)
            s += lax.dot_general(qbd_scr[:, c * pair:(c + 1) * pair], kp, NT_DIMS,
                                 preferred_element_type=F32)
        scores.append(s * scale)
        r = pt_ref[b, n_pages - 1 - (step * pps + i)] % SUBLANE
        sel = jnp.where((sel_c // SUBLANE == sel_h) & (sel_c % SUBLANE == r), 1.0, 0.0).astype(BF16)
        lft = sum(jnp.dot(sel, part, preferred_element_type=F32)
                  for part in _split3(lf_ref[...].reshape(n_heads * SUBLANE, keys)))
        lfts.append(lft)
        incls.append(_dot_exact01(lft, suffix))

    carry = carry_scr[...]
    logits = []
    for i in range(pps):
        logits.append(scores[i] + (carry + incls[i] - lfts[i]))
        carry = carry + incls[i][:, 0:1]
    carry_scr[...] = carry
    logits = jnp.concatenate(logits, axis=-1)

    m_old = m_scr[...]
    m_new = jnp.maximum(m_old, jnp.max(logits, axis=-1, keepdims=True))
    alpha = jnp.exp(m_old - m_new)
    pw = jnp.exp(logits - m_new)
    l_scr[...] = alpha * l_scr[...] + jnp.sum(pw, axis=-1, keepdims=True)
    m_scr[...] = m_new
    pb = pw.astype(BF16)
    for c in range(n_heads // 2):
        vp = jnp.concatenate(
            [jnp.concatenate([kv_refs[2 * i + 1][2 * c], kv_refs[2 * i + 1][2 * c + 1]], axis=-1)
             for i in range(pps)], axis=0).astype(BF16)
        sl = slice(c * pair, (c + 1) * pair)
        acc_scr[:, sl] = alpha * acc_scr[:, sl] + jnp.dot(pb, vp, preferred_element_type=F32)

    @pl.when(step == n_pages // pps - 1)
    def _():
        o_ref[...] = jnp.sum(jnp.where(diag, acc_scr[...] / l_scr[...], 0.0), axis=0, keepdims=True)


DEC_PAGES_PER_STEP = 16


def _dec_fox(q, k_new, v_new, lf_new_col, cache_k, cache_v, cache_lf, page_table, layer):
    b, _, w = q.shape
    n_pages = page_table.shape[1]
    n_heads, page = cache_k.shape[1:3]
    n_pool = cache_lf.shape[1]
    pps = DEC_PAGES_PER_STEP
    assert n_pages % pps == 0 and n_heads % 2 == 0 and n_pool % SUBLANE == 0

    def page_id(bb, p, pt, i):
        return pt[bb, n_pages - 1 - (p * pps + i)]

    row_map = lambda bb, p, pt: (bb, 0, 0)
    row_spec = pl.BlockSpec((None, 1, w), row_map)
    kv_specs, lf_specs = [], []
    for i in range(pps):
        kv_map = lambda bb, p, pt, i=i: (page_id(bb, p, pt, i) + layer * n_pool, 0, 0, 0)
        kv_specs += [pl.BlockSpec((None, n_heads, page, LANE), kv_map)] * 2
        lf_specs.append(pl.BlockSpec((n_heads, SUBLANE, page),
                                     lambda bb, p, pt, i=i: (layer, page_id(bb, p, pt, i) // SUBLANE, 0)))
    kern = functools.partial(_dec_fox_kernel, scale=LANE ** -0.5 / Q_PRESCALE, n_pages=n_pages, pps=pps)
    grid_spec = pltpu.PrefetchScalarGridSpec(
        num_scalar_prefetch=1, grid=(b, n_pages // pps),
        in_specs=[row_spec, row_spec, row_spec, pl.BlockSpec((None, HEAD_ROWS, 1), row_map)]
        + kv_specs + lf_specs,
        out_specs=row_spec,
        scratch_shapes=[pltpu.VMEM((HEAD_ROWS, w), BF16), pltpu.VMEM((HEAD_ROWS, 1), F32),
                        pltpu.VMEM((HEAD_ROWS, 1), F32), pltpu.VMEM((HEAD_ROWS, w), F32),
                        pltpu.VMEM((HEAD_ROWS, 1), F32)])
    return pl.pallas_call(
        kern, grid_spec=grid_spec, out_shape=jax.ShapeDtypeStruct((b, 1, w), F32),
        compiler_params=_params(("arbitrary", "arbitrary"), VMEM_LIMIT_BIG), name="fox_decode_attn")(
            page_table, q, k_new, v_new, lf_new_col, *([cache_k, cache_v] * pps), *([cache_lf] * pps))


def _dec_xattn_kernel(q_ref, k_ref, v_ref, o_ref, *, scale, n_heads):
    for b in range(q_ref.shape[0]):
        s = lax.dot_general(q_ref[b].astype(BF16), k_ref[b].astype(BF16), NT_DIMS,
                            preferred_element_type=F32) * scale
        row = lax.broadcasted_iota(jnp.int32, s.shape, 0)
        col = lax.broadcasted_iota(jnp.int32, s.shape, 1)
        logits = jnp.where(col % n_heads == row, s, NEG_INF)
        p = jnp.exp(logits - jnp.max(logits, axis=-1, keepdims=True))
        p = p / jnp.sum(p, axis=-1, keepdims=True)
        o_ref[b] = jnp.dot(p.astype(BF16), v_ref[b].astype(BF16), preferred_element_type=F32)


DEC_XATTN_SEQS_PER_STEP = 4


def _dec_xattn(q, mem_k, mem_v, layer, n_heads):
    b = q.shape[0]
    n_rows = mem_k.shape[2]
    spb = DEC_XATTN_SEQS_PER_STEP
    assert b % spb == 0
    kern = functools.partial(_dec_xattn_kernel, scale=LANE ** -0.5, n_heads=n_heads)
    head_spec = pl.BlockSpec((spb, HEAD_ROWS, LANE), lambda bb: (bb, 0, 0))
    mem_spec = pl.BlockSpec((None, spb, n_rows, LANE), lambda bb: (layer, bb, 0, 0))
    return pl.pallas_call(
        kern, grid=(b // spb,), in_specs=[head_spec, mem_spec, mem_spec], out_specs=head_spec,
        out_shape=jax.ShapeDtypeStruct((b, HEAD_ROWS, LANE), F32),
        compiler_params=_params(("arbitrary",)), name="mem_xattn_decode")(q, mem_k, mem_v)


def _outproj_kernel(a_ref, xa_ref, w_ref, x_ref, o_ref):
    lhs = jnp.concatenate([a_ref[...].astype(BF16), xa_ref[...].astype(BF16)], axis=-1)
    o_ref[...] = x_ref[...] + jnp.dot(lhs, w_ref[...], preferred_element_type=F32)


def _outproj(a, xa, w, layer, x, *, tm, tn):
    m, d = x.shape
    ka, kx = a.shape[1], xa.shape[1]
    return pl.pallas_call(
        _outproj_kernel, grid=(m // tm, d // tn),
        in_specs=[pl.BlockSpec((tm, ka), lambda i, j: (i, 0)),
                  pl.BlockSpec((tm, kx), lambda i, j: (i, 0)),
                  pl.BlockSpec((None, ka + kx, tn), lambda i, j: (layer, 0, j)),
                  pl.BlockSpec((tm, tn), lambda i, j: (i, j))],
        out_specs=pl.BlockSpec((tm, tn), lambda i, j: (i, j)),
        out_shape=jax.ShapeDtypeStruct((m, d), F32),
        compiler_params=_params(("arbitrary", "arbitrary")), name="out_proj")(a, xa, w, x)


def _ffn_kernel(x_ref, g_ref, wg_ref, wu_ref, wd_ref, o_ref, h_scr):
    f = pl.program_id(1)

    @pl.when(f == 0)
    def _():
        x = x_ref[...]
        h = x * lax.rsqrt(jnp.mean(x * x, axis=-1, keepdims=True) + EPS) * g_ref[...]
        h_scr[...] = h.astype(BF16)
        o_ref[...] = x

    h = h_scr[...]
    gate = jnp.dot(h, wg_ref[...], preferred_element_type=F32)
    up = jnp.dot(h, wu_ref[...], preferred_element_type=F32)
    act = (gate * jax.nn.sigmoid(gate)) * up
    o_ref[...] += jnp.dot(act.astype(BF16), wd_ref[...], preferred_element_type=F32)


def _ffn(x, g, w_gate, w_up, w_down, layer, *, tm, tf):
    m, d = x.shape
    d_ff = w_gate.shape[-1]
    return pl.pallas_call(
        _ffn_kernel, grid=(m // tm, d_ff // tf),
        in_specs=[pl.BlockSpec((tm, d), lambda i, f: (i, 0)),
                  pl.BlockSpec((1, d), lambda i, f: (0, 0)),
                  pl.BlockSpec((None, d, tf), lambda i, f: (layer, 0, f)),
                  pl.BlockSpec((None, d, tf), lambda i, f: (layer, 0, f)),
                  pl.BlockSpec((None, tf, d), lambda i, f: (layer, f, 0))],
        out_specs=pl.BlockSpec((tm, d), lambda i, f: (i, 0)),
        out_shape=jax.ShapeDtypeStruct((m, d), F32),
        scratch_shapes=[pltpu.VMEM((tm, d), BF16)],
        compiler_params=_params(("arbitrary", "arbitrary"), VMEM_LIMIT_BIG), name="swiglu_ffn")(
            x, g.reshape(1, d), w_gate, w_up, w_down)


S5_CHUNK = 2 * LANE


def _gelu_tanh(y):
    return 0.5 * y * (1.0 + jnp.tanh(math.sqrt(2.0 / math.pi) * (y + 0.044715 * (y * y * y))))


def _s5_prompt_kernel(u_ref, bbre_ref, bbim_ref, cre_ref, cimn_ref, are_ref, aim_ref, d_ref,
                      wglu_ref, bglu_ref, mix_ref, hre_ref, him_ref,
                      sre, sim, hre_s, him_s, y_scr, *, lt, n_chunks):
    t = pl.program_id(1)
    tiles_per_chunk = S5_CHUNK // LANE
    chunks_per_row = are_ref.shape[0] // tiles_per_chunk
    chunks_per_tile = S5_CHUNK // (u_ref.shape[1] // n_chunks)

    @pl.when(t == 0)
    def _():
        hre_s[...] = jnp.zeros_like(hre_s)
        him_s[...] = jnp.zeros_like(him_s)

    u = u_ref[...]
    ub = u.astype(BF16)

    def put(dst, q, val):
        j, lc = divmod(q, chunks_per_row)
        for k in range(tiles_per_chunk):
            dst[lc * tiles_per_chunk + k, pl.ds(j, lt, stride=SUBLANE), :] = val[:, k * LANE:(k + 1) * LANE]

    def get(src, q):
        j, lc = divmod(q, chunks_per_row)
        return jnp.concatenate([src[lc * tiles_per_chunk + k, pl.ds(j, lt, stride=SUBLANE), :]
                                for k in range(tiles_per_chunk)], axis=-1)

    for q in range(n_chunks):
        ut = ub[:, (q // 2) * LANE:(q // 2 + 1) * LANE]
        put(sre, q, jnp.dot(ut, bbre_ref[q], preferred_element_type=F32))
        put(sim, q, jnp.dot(ut, bbim_ref[q], preferred_element_type=F32))

    a_re = are_ref[...]
    a_im = aim_ref[...]

    def body(tt, carry):
        h_re, h_im = carry
        r0 = pl.multiple_of(tt * SUBLANE, SUBLANE)
        n_re = a_re * h_re - a_im * h_im + sre[:, pl.ds(r0, SUBLANE), :]
        n_im = a_re * h_im + a_im * h_re + sim[:, pl.ds(r0, SUBLANE), :]
        sre[:, pl.ds(r0, SUBLANE), :] = n_re
        sim[:, pl.ds(r0, SUBLANE), :] = n_im
        return n_re, n_im

    h_re, h_im = lax.fori_loop(0, lt, body, (hre_s[...], him_s[...]), unroll=8)
    hre_s[...] = h_re
    him_s[...] = h_im
    hre_ref[...] = h_re
    him_ref[...] = h_im

    for c in range(n_chunks // chunks_per_tile):
        acc = jnp.zeros((lt, S5_CHUNK), F32)
        for q in range(c * chunks_per_tile, (c + 1) * chunks_per_tile):
            acc += jnp.dot(get(sre, q).astype(BF16), cre_ref[q], preferred_element_type=F32)
            acc += jnp.dot(get(sim, q).astype(BF16), cimn_ref[q], preferred_element_type=F32)
        sl = slice(c * S5_CHUNK, (c + 1) * S5_CHUNK)
        y_scr[:, sl] = _gelu_tanh(acc + d_ref[:, sl] * u[:, sl])

    y = y_scr[...]
    gate = jax.nn.sigmoid(jnp.dot(y.astype(BF16), wglu_ref[...], preferred_element_type=F32)
                          + bglu_ref[...])
    mix_ref[...] = (y * gate).astype(mix_ref.dtype)


def _s5_prompt(u, pk, w_glu, b_glu, *, lt):
    b, t, w = u.shape
    n_chunks = pk["bb_re"].shape[0]
    st_shape = pk["a_re8"].shape
    full = lambda a: pl.BlockSpec(a.shape, lambda bb, tt, nd=a.ndim: (0,) * nd)
    consts = [pk["bb_re"], pk["bb_im"], pk["c_re"], pk["c_imn"], pk["a_re8"], pk["a_im8"], pk["d_row"],
              w_glu, b_glu]
    kern = functools.partial(_s5_prompt_kernel, lt=lt, n_chunks=n_chunks)
    st_spec = pl.BlockSpec((None,) + st_shape, lambda bb, tt: (bb, 0, 0, 0))
    return pl.pallas_call(
        kern, grid=(b, t // lt),
        in_specs=[pl.BlockSpec((None, lt, w), lambda bb, tt: (bb, tt, 0))] + [full(a) for a in consts],
        out_specs=[pl.BlockSpec((None, lt, w), lambda bb, tt: (bb, tt, 0)), st_spec, st_spec],
        out_shape=[jax.ShapeDtypeStruct((b, t, w), BF16),
                   jax.ShapeDtypeStruct((b,) + st_shape, F32),
                   jax.ShapeDtypeStruct((b,) + st_shape, F32)],
        scratch_shapes=[pltpu.VMEM((st_shape[0], lt * SUBLANE, LANE), F32),
                        pltpu.VMEM((st_shape[0], lt * SUBLANE, LANE), F32),
                        pltpu.VMEM(st_shape, F32), pltpu.VMEM(st_shape, F32),
                        pltpu.VMEM((lt, w), F32)],
        compiler_params=_params(("arbitrary", "arbitrary")), name="s5_prompt")(u, *consts)


def _s5_sample_kernel(u_ref, h0re_ref, h0im_ref, bbre_ref, bbim_ref, cre_ref, cimn_ref, are_ref, aim_ref,
                      d_ref, wglu_ref, bglu_ref, mix_ref, hre_ref, him_ref, *, n_chunks):
    u = u_ref[...]
    ub = u.astype(BF16)
    chunks_per_tile = S5_CHUNK // (u.shape[1] // n_chunks)
    ys = []
    for c in range(n_chunks // chunks_per_tile):
        acc = jnp.zeros((u.shape[0], S5_CHUNK), F32)
        for q in range(c * chunks_per_tile, (c + 1) * chunks_per_tile):
            ut = ub[:, (q // 2) * LANE:(q // 2 + 1) * LANE]
            sl = slice(q * S5_CHUNK, (q + 1) * S5_CHUNK)
            a_re, a_im = are_ref[:, sl], aim_ref[:, sl]
            h_re, h_im = h0re_ref[:, sl], h0im_ref[:, sl]
            n_re = a_re * h_re - a_im * h_im + jnp.dot(ut, bbre_ref[q], preferred_element_type=F32)
            n_im = a_re * h_im + a_im * h_re + jnp.dot(ut, bbim_ref[q], preferred_element_type=F32)
            hre_ref[:, sl] = n_re
            him_ref[:, sl] = n_im
            acc += jnp.dot(n_re.astype(BF16), cre_ref[q], preferred_element_type=F32)
            acc += jnp.dot(n_im.astype(BF16), cimn_ref[q], preferred_element_type=F32)
        cs = slice(c * S5_CHUNK, (c + 1) * S5_CHUNK)
        ys.append(_gelu_tanh(acc + d_ref[:, cs] * u[:, cs]))
    y = jnp.concatenate(ys, axis=-1)
    gate = jax.nn.sigmoid(jnp.dot(y.astype(BF16), wglu_ref[...], preferred_element_type=F32)
                          + bglu_ref[...])
    mix_ref[...] = (y * gate).astype(mix_ref.dtype)


def _s5_sample(u, h0_re, h0_im, pk, w_glu, b_glu):
    b, w = u.shape
    n_state = h0_re.shape[1]
    args = [u, h0_re, h0_im, pk["bb_re"], pk["bb_im"], pk["c_re"], pk["c_imn"], pk["a_re1"], pk["a_im1"],
            pk["d_row"], w_glu, b_glu]
    kern = functools.partial(_s5_sample_kernel, n_chunks=pk["bb_re"].shape[0])
    return pl.pallas_call(
        kern,
        out_shape=[jax.ShapeDtypeStruct((b, w), BF16), jax.ShapeDtypeStruct((b, n_state), F32),
                   jax.ShapeDtypeStruct((b, n_state), F32)],
        compiler_params=pltpu.CompilerParams(vmem_limit_bytes=VMEM_LIMIT), name="s5_sample")(*args)


def _state_tiles(a):
    return a.reshape(SUBLANE, -1, LANE).transpose(1, 0, 2)


def _state_untile(h, g, n):
    return h.transpose(0, 2, 1, 3).reshape(h.shape[0], g, n)


def _s5_disc_kernel(are_ref, aim_ref, ldt_ref, bre_ref, bim_ref, abre_ref, abim_ref, bbre_ref, bbim_ref):
    a_re, a_im = are_ref[...], aim_ref[...]
    dt = jnp.exp(ldt_ref[...])
    mag = jnp.exp(a_re * dt)
    ab_re = mag * jnp.cos(a_im * dt)
    ab_im = mag * jnp.sin(a_im * dt)
    num_re, num_im = ab_re - 1.0, ab_im
    den = a_re * a_re + a_im * a_im
    z_re = (num_re * a_re + num_im * a_im) / den
    z_im = (num_im * a_re - num_re * a_im) / den
    abre_ref[...] = ab_re
    abim_ref[...] = ab_im
    b_re, b_im = bre_ref[...], bim_ref[...]
    bbre_ref[...] = z_re * b_re - z_im * b_im
    bbim_ref[...] = z_re * b_im + z_im * b_re


def _s5_pack(a_re, a_im, log_dt, b_re, b_im, c_re, c_im, d):
    g, n = a_re.shape
    p = d.shape[1]
    ab_re, ab_im, bb_re, bb_im = pl.pallas_call(
        _s5_disc_kernel,
        out_shape=[jax.ShapeDtypeStruct((g, 1, n), F32)] * 2 + [jax.ShapeDtypeStruct((g, p, n), F32)] * 2,
        name="s5_discretise")(a_re.reshape(g, 1, n), a_im.reshape(g, 1, n), log_dt.reshape(g, 1, 1),
                              b_re.transpose(0, 2, 1), b_im.transpose(0, 2, 1))
    ab_re, ab_im = ab_re.reshape(g, n), ab_im.reshape(g, n)
    gpc = S5_CHUNK // n
    n_chunks = g // gpc
    eye = jnp.eye(gpc, dtype=F32)
    chunks_per_in_tile = LANE // (gpc * p)
    chunks_per_out_tile = S5_CHUNK // (gpc * p)

    def pack_in(bb):
        blk = jnp.einsum("qgpn,gh->qgphn", bb.reshape(n_chunks, gpc, p, n), eye)
        blk = blk.reshape(n_chunks, gpc * p, S5_CHUNK)
        sel = jax.nn.one_hot(jnp.arange(n_chunks) % chunks_per_in_tile, chunks_per_in_tile, dtype=F32)
        return jnp.einsum("qrc,qs->qsrc", blk, sel).reshape(n_chunks, LANE, S5_CHUNK).astype(BF16)

    def pack_out(cc):
        blk = jnp.einsum("qgnp,gh->qgnhp", cc.transpose(0, 2, 1).reshape(n_chunks, gpc, n, p), eye)
        blk = blk.reshape(n_chunks, S5_CHUNK, gpc * p)
        sel = jax.nn.one_hot(jnp.arange(n_chunks) % chunks_per_out_tile, chunks_per_out_tile, dtype=F32)
        return jnp.einsum("qrc,qs->qrsc", blk, sel).reshape(n_chunks, S5_CHUNK, S5_CHUNK).astype(BF16)

    return {
        "bb_re": pack_in(bb_re), "bb_im": pack_in(bb_im),
        "c_re": pack_out(c_re), "c_imn": pack_out(-c_im),
        "a_re8": _state_tiles(ab_re), "a_im8": _state_tiles(ab_im),
        "a_re1": ab_re.reshape(1, g * n), "a_im1": ab_im.reshape(1, g * n),
        "d_row": d.reshape(1, g * p),
    }


TM = 1024
TN = 512
TM_FFN = 1024
TF = 512
TM_OUT = 512
TQ = 512
S5_LT = 256


def _gain_row(*pieces):
    return jnp.concatenate([jnp.tile(g, reps) for g, reps in pieces]).reshape(1, -1).astype(F32)


def _head_rows(x, n_heads):
    x = x.reshape(x.shape[0], n_heads, LANE)
    return jnp.pad(x, ((0, 0), (0, HEAD_ROWS - n_heads), (0, 0)))


def kernel(x_prompt, x_sample, cache_fox_k, cache_fox_v, cache_fox_logf, state_s5_re, state_s5_im, cache_mem_k, cache_mem_v, page_table, mem_prompt, norm1_g, w_out, mem_norm_g, w_mem_kv, xq_norm_g, xk_norm_g, norm2_g, w_ffn_gate, w_ffn_up, w_ffn_down, fox_w_in, fox_b_f, fox_q_norm_g, fox_k_norm_g, s5_w_in, s5_a_re, s5_a_im, s5_log_dt, s5_b_re, s5_b_im, s5_c_re, s5_c_im, s5_d, s5_w_glu, s5_b_glu):
    bp, tp, d = x_prompt.shape
    bs, ts, _ = x_sample.shape
    depth = norm1_g.shape[0]
    n_fox, n_pool, page, fox_h, fox_hd = cache_fox_k.shape
    fox_w = fox_h * fox_hd
    n_mem, xa_h, xa_hd = cache_mem_k.shape[2:]
    xa_w = xa_h * xa_hd
    s5_g, s5_n = s5_a_re.shape[1:]
    mp, ms = bp * tp, bs * ts
    assert ts == 1 and fox_hd == LANE and xa_hd == LANE

    xp = x_prompt.reshape(mp, d)
    xs = x_sample.reshape(ms, d)
    mem = mem_prompt.reshape(bp * n_mem, d)
    ck = cache_fox_k.transpose(0, 1, 3, 2, 4).reshape(n_fox * n_pool, fox_h, page, fox_hd)
    cv = cache_fox_v.transpose(0, 1, 3, 2, 4).reshape(n_fox * n_pool, fox_h, page, fox_hd)
    clf = cache_fox_logf.transpose(0, 3, 1, 2).reshape(n_fox * fox_h, n_pool, page)
    cmk = cache_mem_k.reshape(depth, bs, n_mem * xa_h, xa_hd)
    cmv = cache_mem_v.reshape(depth, bs, n_mem * xa_h, xa_hd)

    outs = {k: [] for k in ("fk_p", "fv_p", "fl_p", "fk_s", "fv_s", "fl_s",
                            "sr_p", "si_p", "sr_s", "si_s", "mk_p", "mv_p")}
    fox_nt, xa_nt = fox_w // TN, xa_w // TN

    wo16 = w_out.astype(BF16)
    wkv16 = w_mem_kv.astype(BF16)
    wg16, wu16, wd16 = w_ffn_gate.astype(BF16), w_ffn_up.astype(BF16), w_ffn_down.astype(BF16)

    for i in range(depth):
        j = i // 2

        gain = _gain_row((xk_norm_g[i], xa_h), (jnp.ones((xa_hd,), F32), xa_h))
        segs = ((0, xa_nt, True, (0, 1)), (xa_nt, xa_nt, False, (2, 3)))
        defs = ((F32, 0, xa_nt, 0), (BF16, 0, xa_nt, 0), (F32, xa_nt, xa_nt, 0), (BF16, xa_nt, xa_nt, 0))
        mk32, mk16, mv32, mv16 = _proj(mem, mem_norm_g[i], wkv16, gain, segs, defs,
                                       tm=bp * n_mem, tn=TN, layer=i)
        outs["mk_p"].append(mk32.reshape(bp, n_mem, xa_h, xa_hd))
        outs["mv_p"].append(mv32.reshape(bp, n_mem, xa_h, xa_hd))

        if i % 2 == 0:
            w_in = fox_w_in[j]
            w_main = jnp.concatenate([w_in[:, :3 * fox_w], w_in[:, 3 * fox_w + fox_h:]], axis=1).astype(BF16)
            wf = jnp.pad(w_in[:, 3 * fox_w:3 * fox_w + fox_h], ((0, 0), (0, LANE - fox_h))).astype(BF16)
            bf = jnp.pad(fox_b_f[j], (0, LANE - fox_h)).reshape(1, LANE)
            gain = _gain_row((fox_q_norm_g[j] * Q_PRESCALE, fox_h), (fox_k_norm_g[j], fox_h),
                             (jnp.ones((fox_hd,), F32), fox_h), (xq_norm_g[i], xa_h))
            segs = ((0, fox_nt, True, (0,)), (fox_nt, fox_nt, True, (1, 2)),
                    (2 * fox_nt, fox_nt, False, (3, 4)), (3 * fox_nt, xa_nt, True, (5,)))
            defs = ((BF16, 0, fox_nt, 0), (F32, fox_nt, fox_nt, page), (BF16, fox_nt, fox_nt, 0),
                    (F32, 2 * fox_nt, fox_nt, page), (BF16, 2 * fox_nt, fox_nt, 0),
                    (BF16, 3 * fox_nt, xa_nt, 0))
            q16, k32, k16, v32, v16, xq16, lf_p, qa, ka = _proj(
                xp, norm1_g[i], w_main, gain, segs, defs, tm=TM, tn=TN, wf=wf, bf=bf, seq_len=tp,
                n_heads=fox_h)
            outs["fk_p"].append(k32.transpose(0, 2, 1, 3))
            outs["fv_p"].append(v32.transpose(0, 2, 1, 3))
            outs["fl_p"].append(lf_p[:, :fox_h].reshape(mp // page, page, fox_h))
            mix_p = _flash(q16.reshape(bp, tp, fox_w), qa.reshape(bp, tp, LANE), k16.reshape(bp, tp, fox_w),
                           ka.reshape(bp, tp, LANE), v16.reshape(bp, tp, fox_w), tq=TQ).reshape(mp, fox_w)

            defs_s = ((F32, 0, fox_nt, 0), (F32, fox_nt, fox_nt, 0),
                      (F32, 2 * fox_nt, fox_nt, 0), (F32, 3 * fox_nt, xa_nt, 0))
            segs_s = ((0, fox_nt, True, (0,)), (fox_nt, fox_nt, True, (1,)),
                      (2 * fox_nt, fox_nt, False, (2,)), (3 * fox_nt, xa_nt, True, (3,)))
            qs, ks, vs, xqs, lf_s = _proj(xs, norm1_g[i], w_main, gain, segs_s, defs_s,
                                          tm=ms, tn=TN, wf=wf, bf=bf)
            outs["fk_s"].append(ks.reshape(bs, ts, fox_h, fox_hd))
            outs["fv_s"].append(vs.reshape(bs, ts, fox_h, fox_hd))
            outs["fl_s"].append(lf_s[:, :fox_h].reshape(bs, ts, fox_h))
            lf_col = lf_s[:, :HEAD_ROWS].reshape(bs, HEAD_ROWS, 1)
            mix_s = _dec_fox(qs.reshape(bs, 1, fox_w), ks.reshape(bs, 1, fox_w), vs.reshape(bs, 1, fox_w),
                             lf_col, ck, cv, clf, page_table, j).reshape(ms, fox_w)
        else:
            w_main = s5_w_in[j].astype(BF16)
            s5_nt = (w_main.shape[1] - xa_w) // TN
            gain = _gain_row((jnp.ones((LANE,), F32), s5_nt * TN // LANE), (xq_norm_g[i], xa_h))
            segs = ((0, s5_nt, False, (0,)), (s5_nt, xa_nt, True, (1,)))
            pk = _s5_pack(s5_a_re[j], s5_a_im[j], s5_log_dt[j], s5_b_re[j], s5_b_im[j],
                          s5_c_re[j], s5_c_im[j], s5_d[j])
            w_glu = s5_w_glu[j].astype(BF16)
            b_glu = s5_b_glu[j].reshape(1, -1)
            u_p, xq16 = _proj(xp, norm1_g[i], w_main, gain, segs,
                              ((F32, 0, s5_nt, 0), (BF16, s5_nt, xa_nt, 0)), tm=TM, tn=TN)
            s5_w = u_p.shape[1]
            mix_p, hr, hi = _s5_prompt(u_p.reshape(bp, tp, s5_w), pk, w_glu, b_glu, lt=S5_LT)
            mix_p = mix_p.reshape(mp, s5_w)
            outs["sr_p"].append(_state_untile(hr, s5_g, s5_n))
            outs["si_p"].append(_state_untile(hi, s5_g, s5_n))
            u_s, xqs = _proj(xs, norm1_g[i], w_main, gain, segs,
                             ((F32, 0, s5_nt, 0), (F32, s5_nt, xa_nt, 0)), tm=ms, tn=TN)
            mix_s, hr, hi = _s5_sample(u_s, state_s5_re[j].reshape(bs, s5_g * s5_n),
                                       state_s5_im[j].reshape(bs, s5_g * s5_n), pk, w_glu, b_glu)
            outs["sr_s"].append(hr.reshape(bs, s5_g, s5_n))
            outs["si_s"].append(hi.reshape(bs, s5_g, s5_n))

        xa_p = _xattn(xq16.reshape(bp, tp, xa_w), mk16.reshape(bp, n_mem, xa_w),
                      mv16.reshape(bp, n_mem, xa_w), tq=TQ).reshape(mp, xa_w)
        xa_s = _dec_xattn(_head_rows(xqs, xa_h), cmk, cmv, i, xa_h)[:, :xa_h].reshape(ms, xa_w)

        xp = _outproj(mix_p, xa_p, wo16, i, xp, tm=TM_OUT, tn=d)
        xs = _outproj(mix_s, xa_s, wo16, i, xs, tm=ms, tn=d)
        xp = _ffn(xp, norm2_g[i], wg16, wu16, wd16, i, tm=TM_FFN, tf=TF)
        xs = _ffn(xs, norm2_g[i], wg16, wu16, wd16, i, tm=ms, tf=TF)

    st = lambda k: jnp.stack(outs[k])
    return (xp.reshape(bp, tp, d), xs.reshape(bs, ts, d),
            st("fk_p"), st("fv_p"), st("fl_p"), st("fk_s"), st("fv_s"), st("fl_s"),
            st("sr_p"), st("si_p"), st("sr_s"), st("si_s"), st("mk_p"), st("mv_p"))
```

```python
import functools
import math

import jax
import jax.numpy as jnp
import numpy as np
from jax import lax
from jax.experimental import pallas as pl
from jax.experimental.pallas import tpu as pltpu

F32 = jnp.float32
BF16 = jnp.bfloat16

EPS = 1e-6
NEG_INF = -1e30
LANE = 128
SUBLANE = 8
HEAD_ROWS = 16
VMEM_LIMIT = 52 * 1024 * 1024
VMEM_LIMIT_BIG = 58 * 1024 * 1024
NT_DIMS = (((1,), (1,)), ((), ()))


def _params(sem, vmem=VMEM_LIMIT):
    return pltpu.CompilerParams(dimension_semantics=sem, vmem_limit_bytes=vmem)


def _split3(x):
    hi = x.astype(BF16)
    r1 = x - hi.astype(F32)
    mid = r1.astype(BF16)
    lo = (r1 - mid.astype(F32)).astype(BF16)
    return hi, mid, lo


def _dot_exact01(x, w01):
    r = jnp.dot(jnp.concatenate(_split3(x), axis=0), w01, preferred_element_type=F32)
    return r[0:HEAD_ROWS] + r[HEAD_ROWS:2 * HEAD_ROWS] + r[2 * HEAD_ROWS:3 * HEAD_ROWS]


def _log_sigmoid(x):
    return jnp.minimum(x, 0.0) - jnp.log1p(jnp.exp(-jnp.abs(x)))


def _head_rmsnorm(y, gain_row):
    parts = []
    for c in range(y.shape[-1] // LANE):
        p = y[:, c * LANE:(c + 1) * LANE]
        parts.append(p * lax.rsqrt(jnp.mean(p * p, axis=-1, keepdims=True) + EPS))
    return jnp.concatenate(parts, axis=-1) * gain_row


AUG_LANES = 8
SQRT2 = math.sqrt(2.0)
Q_PRESCALE = 0.125
CUMSUM_ROWS = 512


def _lane_split3(x):
    return jnp.concatenate(_split3(x), axis=-1)


def _lane_sum3(r):
    return r[:, 0:LANE] + r[:, LANE:2 * LANE] + r[:, 2 * LANE:3 * LANE]


def _proj_kernel(*refs, segs, n_out, with_f, cumsum, seq_tiles):
    x_ref, g_ref, w_ref, gain_ref = refs[:4]
    pos = 4
    if with_f:
        wf_ref, bf_ref = refs[4:6]
        pos = 6
        if cumsum:
            wqa_ref, wka_ref, oneq_ref, onek_ref = refs[6:10]
            pos = 10
    out_refs = refs[pos:pos + n_out]
    pos += n_out
    if with_f:
        lf_ref = refs[pos]
        pos += 1
        if cumsum:
            qa_ref, ka_ref = refs[pos:pos + 2]
            pos += 2
    h_scr = refs[pos]
    pos += 1
    if with_f and cumsum:
        tri_scr, carry_scr = refs[pos:pos + 2]

    i = pl.program_id(0)
    j = pl.program_id(1)
    tm = x_ref.shape[0]

    if with_f and cumsum:
        ct = tri_scr.shape[0]

        @pl.when((i == 0) & (j == 0))
        def _():
            t = lax.broadcasted_iota(jnp.int32, (ct, ct), 0)
            u = lax.broadcasted_iota(jnp.int32, (ct, ct), 1)
            tri_scr[...] = jnp.where(u <= t, 1.0, 0.0).astype(BF16)

    @pl.when(j == 0)
    def _():
        x = x_ref[...]
        h = x * lax.rsqrt(jnp.mean(x * x, axis=-1, keepdims=True) + EPS) * g_ref[...]
        hb = h.astype(BF16)
        h_scr[...] = hb
        if with_f:
            lf = _log_sigmoid(jnp.dot(hb, wf_ref[...], preferred_element_type=F32) + bf_ref[...])
            lf_ref[...] = lf
            if cumsum:
                @pl.when(i % seq_tiles == 0)
                def _():
                    carry_scr[...] = jnp.zeros_like(carry_scr)
                carry = carry_scr[...]
                for s in range(tm // ct):
                    rs = slice(s * ct, (s + 1) * ct)
                    c = _lane_sum3(jnp.dot(tri_scr[...], _lane_split3(lf[rs]), preferred_element_type=F32))
                    c = c + carry
                    carry = c[ct - 1:ct, :]
                    pieces = _lane_split3(c * SQRT2)
                    qa_ref[rs, :] = (jnp.dot(pieces, wqa_ref[...], preferred_element_type=F32)
                                     + oneq_ref[...]).astype(qa_ref.dtype)
                    ka_ref[rs, :] = (jnp.dot(pieces, wka_ref[...], preferred_element_type=F32)
                                     + onek_ref[...]).astype(ka_ref.dtype)
                carry_scr[...] = carry

    n_sub = 2 if tm % (2 * HEAD_ROWS) == 0 and tm >= 2 * LANE else 1
    rows = tm // n_sub
    for start, n_tiles, norm, outs in segs:
        @pl.when((j >= start) & (j < start + n_tiles))
        def _(norm=norm, outs=outs):
            for r in range(n_sub):
                rs = slice(r * rows, (r + 1) * rows)
                y = jnp.dot(h_scr[rs, :], w_ref[...], preferred_element_type=F32)
                yy = _head_rmsnorm(y, gain_ref[...]) if norm else y
                for o in outs:
                    if len(out_refs[o].shape) == 4:
                        page = out_refs[o].shape[2]
                        for pg in range(rows // page):
                            for hh in range(yy.shape[1] // LANE):
                                out_refs[o][r * (rows // page) + pg, hh] = (
                                    yy[pg * page:(pg + 1) * page, hh * LANE:(hh + 1) * LANE])
                    else:
                        out_refs[o][rs, :] = yy.astype(out_refs[o].dtype)


def _aug_maps(n_heads):
    wqa = np.zeros((3 * LANE, LANE), np.float32)
    wka = np.zeros((3 * LANE, LANE), np.float32)
    oneq = np.zeros((1, LANE), np.float32)
    onek = np.zeros((1, LANE), np.float32)
    for h in range(n_heads):
        for piece in range(3):
            wqa[piece * LANE + h, AUG_LANES * h + piece] = 1.0
            wka[piece * LANE + h, AUG_LANES * h + 3 + piece] = -1.0
            oneq[0, AUG_LANES * h + 3 + piece] = 1.0
            onek[0, AUG_LANES * h + piece] = 1.0
    return jnp.asarray(wqa, BF16), jnp.asarray(wka, BF16), jnp.asarray(oneq), jnp.asarray(onek)


def _proj(x, g, w, gain_row, segs, out_defs, *, tm, tn, layer=None, wf=None, bf=None, seq_len=None,
          n_heads=None):
    m, k = x.shape
    n = w.shape[-1]
    with_f = wf is not None
    cumsum = seq_len is not None
    grid = (m // tm, n // tn)
    const = lambda shape: pl.BlockSpec(shape, lambda i, j: (0,) * len(shape))
    w_spec = (pl.BlockSpec((k, tn), lambda i, j: (0, j)) if layer is None
              else pl.BlockSpec((None, k, tn), lambda i, j: (layer, 0, j)))
    in_specs = [pl.BlockSpec((tm, k), lambda i, j: (i, 0)), const((1, k)), w_spec,
                pl.BlockSpec((1, tn), lambda i, j: (0, j))]
    args = [x, g.reshape(1, k), w, gain_row]
    if with_f:
        in_specs += [const((k, LANE)), const((1, LANE))]
        args += [wf, bf]
        if cumsum:
            maps = _aug_maps(n_heads)
            in_specs += [const(a.shape) for a in maps]
            args += list(maps)
    out_shapes, out_specs = [], []
    for dt, start, n_tiles, page in out_defs:
        if page:
            out_shapes.append(jax.ShapeDtypeStruct((m // page, n_tiles * tn // LANE, page, LANE), dt))
            out_specs.append(pl.BlockSpec(
                (tm // page, tn // LANE, page, LANE),
                lambda i, j, s=start, nt=n_tiles: (i, jnp.clip(j - s, 0, nt - 1), 0, 0)))
        else:
            out_shapes.append(jax.ShapeDtypeStruct((m, n_tiles * tn), dt))
            out_specs.append(pl.BlockSpec(
                (tm, tn), lambda i, j, s=start, nt=n_tiles: (i, jnp.clip(j - s, 0, nt - 1))))
    scratch = [pltpu.VMEM((tm, k), BF16)]
    seq_tiles = 1
    if with_f:
        row_spec = pl.BlockSpec((tm, LANE), lambda i, j: (i, 0))
        out_shapes.append(jax.ShapeDtypeStruct((m, LANE), F32))
        out_specs.append(row_spec)
        if cumsum:
            seq_tiles = seq_len // tm
            ct = math.gcd(tm, CUMSUM_ROWS)
            out_shapes += [jax.ShapeDtypeStruct((m, LANE), BF16)] * 2
            out_specs += [row_spec] * 2
            scratch += [pltpu.VMEM((ct, ct), BF16), pltpu.VMEM((1, LANE), F32)]
    kern = functools.partial(_proj_kernel, segs=segs, n_out=len(out_defs), with_f=with_f,
                             cumsum=cumsum, seq_tiles=seq_tiles)
    return pl.pallas_call(
        kern, grid=grid, in_specs=in_specs, out_specs=out_specs, out_shape=out_shapes,
        scratch_shapes=scratch,
        compiler_params=_params(("arbitrary", "arbitrary"), VMEM_LIMIT_BIG if tm >= 1024 else VMEM_LIMIT),
        name="norm_proj")(*args)


LOG2_SCALE = math.log2(math.e) / SQRT2


def _flash_kernel(qi_ref, kj_ref, q_ref, qa_ref, k_ref, ka_ref, v_ref, o_ref,
                  qaug_scr, m_scr, l_scr, acc_scr):
    hg = pl.program_id(1)
    i = qi_ref[pl.program_id(2)]
    j = kj_ref[pl.program_id(2)]
    tq, tk = q_ref.shape[0], k_ref.shape[0]
    n_hp = q_ref.shape[1] // LANE

    @pl.when(j == 0)
    def _():
        lane = lax.broadcasted_iota(jnp.int32, (tq, LANE), 1)
        for hp in range(n_hp):
            qaug_scr[hp, :, 0:LANE] = q_ref[:, hp * LANE:(hp + 1) * LANE]
            qaug_scr[hp, :, LANE:2 * LANE] = jnp.where(lane // AUG_LANES == hg * n_hp + hp, qa_ref[...],
                                                       jnp.zeros_like(qa_ref))
        m_scr[...] = jnp.full_like(m_scr, NEG_INF)
        l_scr[...] = jnp.zeros_like(l_scr)
        acc_scr[...] = jnp.zeros_like(acc_scr)

    def step(masked):
        for hp in range(n_hp):
            hs = slice(hp * LANE, (hp + 1) * LANE)
            k_aug = jnp.concatenate([k_ref[:, hs], ka_ref[...]], axis=-1)
            x = lax.dot_general(qaug_scr[hp], k_aug, NT_DIMS, preferred_element_type=F32) * LOG2_SCALE
            if masked:
                row = lax.broadcasted_iota(jnp.int32, x.shape, 0)
                col = lax.broadcasted_iota(jnp.int32, x.shape, 1)
                x = jnp.where(col <= row, x, NEG_INF)
            m_old = m_scr[hp]
            m_new = jnp.maximum(m_old, jnp.max(x, axis=-1, keepdims=True))
            alpha = jnp.exp2(m_old - m_new)
            l_part = alpha * l_scr[hp]
            ps = []
            for c in range(tk // LANE):
                p = jnp.exp2(x[:, c * LANE:(c + 1) * LANE] - m_new)
                l_part = l_part + p
                ps.append(p.astype(BF16))
            l_scr[hp] = l_part
            m_scr[hp] = m_new
            acc_scr[hp] = alpha * acc_scr[hp] + jnp.dot(jnp.concatenate(ps, axis=-1), v_ref[:, hs],
                                                        preferred_element_type=F32)

    @pl.when(j < i)
    def _():
        step(False)

    @pl.when(j == i)
    def _():
        step(True)
        for hp in range(n_hp):
            o_ref[:, hp * LANE:(hp + 1) * LANE] = (
                acc_scr[hp] / jnp.sum(l_scr[hp], axis=-1, keepdims=True)).astype(o_ref.dtype)


FLASH_HEADS_PER_STEP = 12


def _flash(q, qa, k, ka, v, *, tq):
    b, t, w = q.shape
    n_hp = FLASH_HEADS_PER_STEP
    hw = n_hp * LANE
    assert w % hw == 0
    nq = t // tq
    pairs = [(i, j) for i in range(nq) for j in range(i + 1)]
    qi = jnp.asarray([p[0] for p in pairs], jnp.int32)
    kj = jnp.asarray([p[1] for p in pairs], jnp.int32)
    q_map = lambda bb, hh, s, qi, kj: (bb, qi[s], hh)
    kv_map = lambda bb, hh, s, qi, kj: (bb, kj[s], hh)
    grid_spec = pltpu.PrefetchScalarGridSpec(
        num_scalar_prefetch=2, grid=(b, w // hw, len(pairs)),
        in_specs=[
            pl.BlockSpec((None, tq, hw), q_map),
            pl.BlockSpec((None, tq, LANE), lambda bb, hh, s, qi, kj: (bb, qi[s], 0)),
            pl.BlockSpec((None, tq, hw), kv_map),
            pl.BlockSpec((None, tq, LANE), lambda bb, hh, s, qi, kj: (bb, kj[s], 0)),
            pl.BlockSpec((None, tq, hw), kv_map),
        ],
        out_specs=pl.BlockSpec((None, tq, hw), q_map),
        scratch_shapes=[pltpu.VMEM((n_hp, tq, 2 * LANE), BF16), pltpu.VMEM((n_hp, tq, LANE), F32),
                        pltpu.VMEM((n_hp, tq, LANE), F32), pltpu.VMEM((n_hp, tq, LANE), F32)])
    return pl.pallas_call(
        _flash_kernel, grid_spec=grid_spec, out_shape=jax.ShapeDtypeStruct((b, t, w), BF16),
        compiler_params=_params(("arbitrary",) * 3), name="fox_prompt_attn")(qi, kj, q, qa, k, ka, v)


def _xattn_kernel(q_ref, k_ref, v_ref, o_ref, *, scale):
    for h in range(q_ref.shape[-1] // LANE):
        sl = slice(h * LANE, (h + 1) * LANE)
        s = lax.dot_general(q_ref[:, sl], k_ref[:, sl], NT_DIMS, preferred_element_type=F32) * scale
        p = jnp.exp(s - jnp.max(s, axis=-1, keepdims=True))
        p = p / jnp.sum(p, axis=-1, keepdims=True)
        o_ref[:, sl] = jnp.dot(p.astype(BF16), v_ref[:, sl],
                               preferred_element_type=F32).astype(o_ref.dtype)


def _xattn(q, k, v, *, tq):
    b, t, w = q.shape
    n_mem = k.shape[1]
    kern = functools.partial(_xattn_kernel, scale=LANE ** -0.5)
    return pl.pallas_call(
        kern, grid=(b, t // tq),
        in_specs=[pl.BlockSpec((None, tq, w), lambda bb, i: (bb, i, 0)),
                  pl.BlockSpec((None, n_mem, w), lambda bb, i: (bb, 0, 0)),
                  pl.BlockSpec((None, n_mem, w), lambda bb, i: (bb, 0, 0))],
        out_specs=pl.BlockSpec((None, tq, w), lambda bb, i: (bb, i, 0)),
        out_shape=jax.ShapeDtypeStruct((b, t, w), BF16),
        compiler_params=_params(("arbitrary", "arbitrary")), name="mem_xattn_prompt")(q, k, v)


def _head_diag(width):
    rows = lax.broadcasted_iota(jnp.int32, (HEAD_ROWS, width), 0)
    lanes = lax.broadcasted_iota(jnp.int32, (HEAD_ROWS, width), 1)
    return (lanes // LANE) == rows


def _dec_fox_kernel(pt_ref, q_ref, kn_ref, vn_ref, lfn_ref, *refs, scale, n_pages, pps):
    kv_refs, lf_refs = refs[:2 * pps], refs[2 * pps:3 * pps]
    o_ref, qbd_scr, m_scr, l_scr, acc_scr, carry_scr = refs[3 * pps:]
    b = pl.program_id(0)
    step = pl.program_id(1)
    n_heads, keys, _ = kv_refs[0].shape
    width = n_heads * LANE
    pair = 2 * LANE
    diag = _head_diag(width)

    @pl.when(step == 0)
    def _():
        qbd = jnp.where(diag, jnp.broadcast_to(q_ref[...], (HEAD_ROWS, width)), 0.0).astype(BF16)
        qbd_scr[...] = qbd
        kn = kn_ref[...].astype(BF16).astype(F32)
        m_scr[...] = jnp.sum(qbd.astype(F32) * kn, axis=-1, keepdims=True) * scale
        l_scr[...] = jnp.ones_like(l_scr)
        acc_scr[...] = jnp.broadcast_to(vn_ref[...].astype(BF16).astype(F32), (HEAD_ROWS, width))
        carry_scr[...] = lfn_ref[...]

    u = lax.broadcasted_iota(jnp.int32, (keys, keys), 0)
    kk = lax.broadcasted_iota(jnp.int32, (keys, keys), 1)
    suffix = jnp.where(u >= kk, 1.0, 0.0).astype(BF16)
    sel_h = lax.broadcasted_iota(jnp.int32, (HEAD_ROWS, n_heads * SUBLANE), 0)
    sel_c = lax.broadcasted_iota(jnp.int32, (HEAD_ROWS, n_heads * SUBLANE), 1)

    scores, incls, lfts = [], [], []
    for i in range(pps):
        k_ref, lf_ref = kv_refs[2 * i], lf_refs[i]
        s = jnp.zeros((HEAD_ROWS, keys), F32)
        for c in range(n_heads // 2):
            kp = jnp.concatenate([k_ref[2 * c], k_ref[2 * c + 1]], axis=-1).astype(BF16)
            s += lax.dot_general(qbd_scr[:, c * pair:(c + 1) * pair], kp, NT_DIMS,
                                 preferred_element_type=F32)
        scores.append(s * scale)
        r = pt_ref[b, n_pages - 1 - (step * pps + i)] % SUBLANE
        sel = jnp.where((sel_c // SUBLANE == sel_h) & (sel_c % SUBLANE == r), 1.0, 0.0).astype(BF16)
        lft = sum(jnp.dot(sel, part, preferred_element_type=F32)
                  for part in _split3(lf_ref[...].reshape(n_heads * SUBLANE, keys)))
        lfts.append(lft)
        incls.append(_dot_exact01(lft, suffix))

    carry = carry_scr[...]
    logits = []
    for i in range(pps):
        logits.append(scores[i] + (carry + incls[i] - lfts[i]))
        carry = carry + incls[i][:, 0:1]
    carry_scr[...] = carry
    logits = jnp.concatenate(logits, axis=-1)

    m_old = m_scr[...]
    m_new = jnp.maximum(m_old, jnp.max(logits, axis=-1, keepdims=True))
    alpha = jnp.exp(m_old - m_new)
    pw = jnp.exp(logits - m_new)
    l_scr[...] = alpha * l_scr[...] + jnp.sum(pw, axis=-1, keepdims=True)
    m_scr[...] = m_new
    pb = pw.astype(BF16)
    for c in range(n_heads // 2):
        vp = jnp.concatenate(
            [jnp.concatenate([kv_refs[2 * i + 1][2 * c], kv_refs[2 * i + 1][2 * c + 1]], axis=-1)
             for i in range(pps)], axis=0).astype(BF16)
        sl = slice(c * pair, (c + 1) * pair)
        acc_scr[:, sl] = alpha * acc_scr[:, sl] + jnp.dot(pb, vp, preferred_element_type=F32)

    @pl.when(step == n_pages // pps - 1)
    def _():
        o_ref[...] = jnp.sum(jnp.where(diag, acc_scr[...] / l_scr[...], 0.0), axis=0, keepdims=True)


DEC_PAGES_PER_STEP = 16


def _dec_fox(q, k_new, v_new, lf_new_col, cache_k, cache_v, cache_lf, page_table, layer):
    b, _, w = q.shape
    n_pages = page_table.shape[1]
    n_heads, page = cache_k.shape[1:3]
    n_pool = cache_lf.shape[1]
    pps = DEC_PAGES_PER_STEP
    assert n_pages % pps == 0 and n_heads % 2 == 0 and n_pool % SUBLANE == 0

    def page_id(bb, p, pt, i):
        return pt[bb, n_pages - 1 - (p * pps + i)]

    row_map = lambda bb, p, pt: (bb, 0, 0)
    row_spec = pl.BlockSpec((None, 1, w), row_map)
    kv_specs, lf_specs = [], []
    for i in range(pps):
        kv_map = lambda bb, p, pt, i=i: (page_id(bb, p, pt, i) + layer * n_pool, 0, 0, 0)
        kv_specs += [pl.BlockSpec((None, n_heads, page, LANE), kv_map)] * 2
        lf_specs.append(pl.BlockSpec((n_heads, SUBLANE, page),
                                     lambda bb, p, pt, i=i: (layer, page_id(bb, p, pt, i) // SUBLANE, 0)))
    kern = functools.partial(_dec_fox_kernel, scale=LANE ** -0.5 / Q_PRESCALE, n_pages=n_pages, pps=pps)
    grid_spec = pltpu.PrefetchScalarGridSpec(
        num_scalar_prefetch=1, grid=(b, n_pages // pps),
        in_specs=[row_spec, row_spec, row_spec, pl.BlockSpec((None, HEAD_ROWS, 1), row_map)]
        + kv_specs + lf_specs,
        out_specs=row_spec,
        scratch_shapes=[pltpu.VMEM((HEAD_ROWS, w), BF16), pltpu.VMEM((HEAD_ROWS, 1), F32),
                        pltpu.VMEM((HEAD_ROWS, 1), F32), pltpu.VMEM((HEAD_ROWS, w), F32),
                        pltpu.VMEM((HEAD_ROWS, 1), F32)])
    return pl.pallas_call(
        kern, grid_spec=grid_spec, out_shape=jax.ShapeDtypeStruct((b, 1, w), F32),
        compiler_params=_params(("arbitrary", "arbitrary"), VMEM_LIMIT_BIG), name="fox_decode_attn")(
            page_table, q, k_new, v_new, lf_new_col, *([cache_k, cache_v] * pps), *([cache_lf] * pps))


def _dec_xattn_kernel(q_ref, k_ref, v_ref, o_ref, *, scale, n_heads):
    for b in range(q_ref.shape[0]):
        s = lax.dot_general(q_ref[b].astype(BF16), k_ref[b].astype(BF16), NT_DIMS,
                            preferred_element_type=F32) * scale
        row = lax.broadcasted_iota(jnp.int32, s.shape, 0)
        col = lax.broadcasted_iota(jnp.int32, s.shape, 1)
        logits = jnp.where(col % n_heads == row, s, NEG_INF)
        p = jnp.exp(logits - jnp.max(logits, axis=-1, keepdims=True))
        p = p / jnp.sum(p, axis=-1, keepdims=True)
        o_ref[b] = jnp.dot(p.astype(BF16), v_ref[b].astype(BF16), preferred_element_type=F32)


DEC_XATTN_SEQS_PER_STEP = 4


def _dec_xattn(q, mem_k, mem_v, layer, n_heads):
    b = q.shape[0]
    n_rows = mem_k.shape[2]
    spb = DEC_XATTN_SEQS_PER_STEP
    assert b % spb == 0
    kern = functools.partial(_dec_xattn_kernel, scale=LANE ** -0.5, n_heads=n_heads)
    head_spec = pl.BlockSpec((spb, HEAD_ROWS, LANE), lambda bb: (bb, 0, 0))
    mem_spec = pl.BlockSpec((None, spb, n_rows, LANE), lambda bb: (layer, bb, 0, 0))
    return pl.pallas_call(
        kern, grid=(b // spb,), in_specs=[head_spec, mem_spec, mem_spec], out_specs=head_spec,
        out_shape=jax.ShapeDtypeStruct((b, HEAD_ROWS, LANE), F32),
        compiler_params=_params(("arbitrary",)), name="mem_xattn_decode")(q, mem_k, mem_v)


def _outproj_kernel(a_ref, xa_ref, w_ref, x_ref, o_ref):
    lhs = jnp.concatenate([a_ref[...].astype(BF16), xa_ref[...].astype(BF16)], axis=-1)
    o_ref[...] = x_ref[...] + jnp.dot(lhs, w_ref[...], preferred_element_type=F32)


def _outproj(a, xa, w, layer, x, *, tm, tn):
    m, d = x.shape
    ka, kx = a.shape[1], xa.shape[1]
    return pl.pallas_call(
        _outproj_kernel, grid=(m // tm, d // tn),
        in_specs=[pl.BlockSpec((tm, ka), lambda i, j: (i, 0)),
                  pl.BlockSpec((tm, kx), lambda i, j: (i, 0)),
                  pl.BlockSpec((None, ka + kx, tn), lambda i, j: (layer, 0, j)),
                  pl.BlockSpec((tm, tn), lambda i, j: (i, j))],
        out_specs=pl.BlockSpec((tm, tn), lambda i, j: (i, j)),
        out_shape=jax.ShapeDtypeStruct((m, d), F32),
        compiler_params=_params(("arbitrary", "arbitrary")), name="out_proj")(a, xa, w, x)


def _ffn_kernel(x_ref, g_ref, wg_ref, wu_ref, wd_ref, o_ref, h_scr):
    f = pl.program_id(1)

    @pl.when(f == 0)
    def _():
        x = x_ref[...]
        h = x * lax.rsqrt(jnp.mean(x * x, axis=-1, keepdims=True) + EPS) * g_ref[...]
        h_scr[...] = h.astype(BF16)
        o_ref[...] = x

    h = h_scr[...]
    gate = jnp.dot(h, wg_ref[...], preferred_element_type=F32)
    up = jnp.dot(h, wu_ref[...], preferred_element_type=F32)
    act = (gate * jax.nn.sigmoid(gate)) * up
    o_ref[...] += jnp.dot(act.astype(BF16), wd_ref[...], preferred_element_type=F32)


def _ffn(x, g, w_gate, w_up, w_down, layer, *, tm, tf):
    m, d = x.shape
    d_ff = w_gate.shape[-1]
    return pl.pallas_call(
        _ffn_kernel, grid=(m // tm, d_ff // tf),
        in_specs=[pl.BlockSpec((tm, d), lambda i, f: (i, 0)),
                  pl.BlockSpec((1, d), lambda i, f: (0, 0)),
                  pl.BlockSpec((None, d, tf), lambda i, f: (layer, 0, f)),
                  pl.BlockSpec((None, d, tf), lambda i, f: (layer, 0, f)),
                  pl.BlockSpec((None, tf, d), lambda i, f: (layer, f, 0))],
        out_specs=pl.BlockSpec((tm, d), lambda i, f: (i, 0)),
        out_shape=jax.ShapeDtypeStruct((m, d), F32),
        scratch_shapes=[pltpu.VMEM((tm, d), BF16)],
        compiler_params=_params(("arbitrary", "arbitrary"), VMEM_LIMIT_BIG), name="swiglu_ffn")(
            x, g.reshape(1, d), w_gate, w_up, w_down)


S5_CHUNK = 2 * LANE


def _gelu_tanh(y):
    return 0.5 * y * (1.0 + jnp.tanh(math.sqrt(2.0 / math.pi) * (y + 0.044715 * (y * y * y))))


def _s5_prompt_kernel(u_ref, bbre_ref, bbim_ref, cre_ref, cimn_ref, are_ref, aim_ref, d_ref,
                      wglu_ref, bglu_ref, mix_ref, hre_ref, him_ref,
                      sre, sim, hre_s, him_s, y_scr, *, lt, n_chunks):
    t = pl.program_id(1)
    tiles_per_chunk = S5_CHUNK // LANE
    chunks_per_row = are_ref.shape[0] // tiles_per_chunk
    chunks_per_tile = S5_CHUNK // (u_ref.shape[1] // n_chunks)

    @pl.when(t == 0)
    def _():
        hre_s[...] = jnp.zeros_like(hre_s)
        him_s[...] = jnp.zeros_like(him_s)

    u = u_ref[...]
    ub = u.astype(BF16)

    def put(dst, q, val):
        j, lc = divmod(q, chunks_per_row)
        for k in range(tiles_per_chunk):
            dst[lc * tiles_per_chunk + k, pl.ds(j, lt, stride=SUBLANE), :] = val[:, k * LANE:(k + 1) * LANE]

    def get(src, q):
        j, lc = divmod(q, chunks_per_row)
        return jnp.concatenate([src[lc * tiles_per_chunk + k, pl.ds(j, lt, stride=SUBLANE), :]
                                for k in range(tiles_per_chunk)], axis=-1)

    for q in range(n_chunks):
        ut = ub[:, (q // 2) * LANE:(q // 2 + 1) * LANE]
        put(sre, q, jnp.dot(ut, bbre_ref[q], preferred_element_type=F32))
        put(sim, q, jnp.dot(ut, bbim_ref[q], preferred_element_type=F32))

    a_re = are_ref[...]
    a_im = aim_ref[...]

    def body(tt, carry):
        h_re, h_im = carry
        r0 = pl.multiple_of(tt * SUBLANE, SUBLANE)
        n_re = a_re * h_re - a_im * h_im + sre[:, pl.ds(r0, SUBLANE), :]
        n_im = a_re * h_im + a_im * h_re + sim[:, pl.ds(r0, SUBLANE), :]
        sre[:, pl.ds(r0, SUBLANE), :] = n_re
        sim[:, pl.ds(r0, SUBLANE), :] = n_im
        return n_re, n_im

    h_re, h_im = lax.fori_loop(0, lt, body, (hre_s[...], him_s[...]), unroll=8)
    hre_s[...] = h_re
    him_s[...] = h_im
    hre_ref[...] = h_re
    him_ref[...] = h_im

    for c in range(n_chunks // chunks_per_tile):
        acc = jnp.zeros((lt, S5_CHUNK), F32)
        for q in range(c * chunks_per_tile, (c + 1) * chunks_per_tile):
            acc += jnp.dot(get(sre, q).astype(BF16), cre_ref[q], preferred_element_type=F32)
            acc += jnp.dot(get(sim, q).astype(BF16), cimn_ref[q], preferred_element_type=F32)
        sl = slice(c * S5_CHUNK, (c + 1) * S5_CHUNK)
        y_scr[:, sl] = _gelu_tanh(acc + d_ref[:, sl] * u[:, sl])

    y = y_scr[...]
    gate = jax.nn.sigmoid(jnp.dot(y.astype(BF16), wglu_ref[...], preferred_element_type=F32)
                          + bglu_ref[...])
    mix_ref[...] = (y * gate).astype(mix_ref.dtype)


def _s5_prompt(u, pk, w_glu, b_glu, *, lt):
    b, t, w = u.shape
    n_chunks = pk["bb_re"].shape[0]
    st_shape = pk["a_re8"].shape
    full = lambda a: pl.BlockSpec(a.shape, lambda bb, tt, nd=a.ndim: (0,) * nd)
    consts = [pk["bb_re"], pk["bb_im"], pk["c_re"], pk["c_imn"], pk["a_re8"], pk["a_im8"], pk["d_row"],
              w_glu, b_glu]
    kern = functools.partial(_s5_prompt_kernel, lt=lt, n_chunks=n_chunks)
    st_spec = pl.BlockSpec((None,) + st_shape, lambda bb, tt: (bb, 0, 0, 0))
    return pl.pallas_call(
        kern, grid=(b, t // lt),
        in_specs=[pl.BlockSpec((None, lt, w), lambda bb, tt: (bb, tt, 0))] + [full(a) for a in consts],
        out_specs=[pl.BlockSpec((None, lt, w), lambda bb, tt: (bb, tt, 0)), st_spec, st_spec],
        out_shape=[jax.ShapeDtypeStruct((b, t, w), BF16),
                   jax.ShapeDtypeStruct((b,) + st_shape, F32),
                   jax.ShapeDtypeStruct((b,) + st_shape, F32)],
        scratch_shapes=[pltpu.VMEM((st_shape[0], lt * SUBLANE, LANE), F32),
                        pltpu.VMEM((st_shape[0], lt * SUBLANE, LANE), F32),
                        pltpu.VMEM(st_shape, F32), pltpu.VMEM(st_shape, F32),
                        pltpu.VMEM((lt, w), F32)],
        compiler_params=_params(("arbitrary", "arbitrary"), VMEM_LIMIT_BIG), name="s5_prompt")(u, *consts)


def _s5_sample_kernel(u_ref, h0re_ref, h0im_ref, bbre_ref, bbim_ref, cre_ref, cimn_ref, are_ref, aim_ref,
                      d_ref, wglu_ref, bglu_ref, mix_ref, hre_ref, him_ref, *, n_chunks):
    u = u_ref[...]
    ub = u.astype(BF16)
    chunks_per_tile = S5_CHUNK // (u.shape[1] // n_chunks)
    ys = []
    for c in range(n_chunks // chunks_per_tile):
        acc = jnp.zeros((u.shape[0], S5_CHUNK), F32)
        for q in range(c * chunks_per_tile, (c + 1) * chunks_per_tile):
            ut = ub[:, (q // 2) * LANE:(q // 2 + 1) * LANE]
            sl = slice(q * S5_CHUNK, (q + 1) * S5_CHUNK)
            a_re, a_im = are_ref[:, sl], aim_ref[:, sl]
            h_re, h_im = h0re_ref[:, sl], h0im_ref[:, sl]
            n_re = a_re * h_re - a_im * h_im + jnp.dot(ut, bbre_ref[q], preferred_element_type=F32)
            n_im = a_re * h_im + a_im * h_re + jnp.dot(ut, bbim_ref[q], preferred_element_type=F32)
            hre_ref[:, sl] = n_re
            him_ref[:, sl] = n_im
            acc += jnp.dot(n_re.astype(BF16), cre_ref[q], preferred_element_type=F32)
            acc += jnp.dot(n_im.astype(BF16), cimn_ref[q], preferred_element_type=F32)
        cs = slice(c * S5_CHUNK, (c + 1) * S5_CHUNK)
        ys.append(_gelu_tanh(acc + d_ref[:, cs] * u[:, cs]))
    y = jnp.concatenate(ys, axis=-1)
    gate = jax.nn.sigmoid(jnp.dot(y.astype(BF16), wglu_ref[...], preferred_element_type=F32)
                          + bglu_ref[...])
    mix_ref[...] = (y * gate).astype(mix_ref.dtype)


def _s5_sample(u, h0_re, h0_im, pk, w_glu, b_glu):
    b, w = u.shape
    n_state = h0_re.shape[1]
    args = [u, h0_re, h0_im, pk["bb_re"], pk["bb_im"], pk["c_re"], pk["c_imn"], pk["a_re1"], pk["a_im1"],
            pk["d_row"], w_glu, b_glu]
    kern = functools.partial(_s5_sample_kernel, n_chunks=pk["bb_re"].shape[0])
    return pl.pallas_call(
        kern,
        out_shape=[jax.ShapeDtypeStruct((b, w), BF16), jax.ShapeDtypeStruct((b, n_state), F32),
                   jax.ShapeDtypeStruct((b, n_state), F32)],
        compiler_params=pltpu.CompilerParams(vmem_limit_bytes=VMEM_LIMIT), name="s5_sample")(*args)


def _state_tiles(a):
    return a.reshape(SUBLANE, -1, LANE).transpose(1, 0, 2)


def _state_untile(h, g, n):
    return h.transpose(0, 2, 1, 3).reshape(h.shape[0], g, n)


def _s5_disc_kernel(are_ref, aim_ref, ldt_ref, bre_ref, bim_ref, abre_ref, abim_ref, bbre_ref, bbim_ref):
    a_re, a_im = are_ref[...], aim_ref[...]
    dt = jnp.exp(ldt_ref[...])
    mag = jnp.exp(a_re * dt)
    ab_re = mag * jnp.cos(a_im * dt)
    ab_im = mag * jnp.sin(a_im * dt)
    num_re, num_im = ab_re - 1.0, ab_im
    den = a_re * a_re + a_im * a_im
    z_re = (num_re * a_re + num_im * a_im) / den
    z_im = (num_im * a_re - num_re * a_im) / den
    abre_ref[...] = ab_re
    abim_ref[...] = ab_im
    b_re, b_im = bre_ref[...], bim_ref[...]
    bbre_ref[...] = z_re * b_re - z_im * b_im
    bbim_ref[...] = z_re * b_im + z_im * b_re


def _s5_pack(a_re, a_im, log_dt, b_re, b_im, c_re, c_im, d):
    g, n = a_re.shape
    p = d.shape[1]
    ab_re, ab_im, bb_re, bb_im = pl.pallas_call(
        _s5_disc_kernel,
        out_shape=[jax.ShapeDtypeStruct((g, 1, n), F32)] * 2 + [jax.ShapeDtypeStruct((g, p, n), F32)] * 2,
        name="s5_discretise")(a_re.reshape(g, 1, n), a_im.reshape(g, 1, n), log_dt.reshape(g, 1, 1),
                              b_re.transpose(0, 2, 1), b_im.transpose(0, 2, 1))
    ab_re, ab_im = ab_re.reshape(g, n), ab_im.reshape(g, n)
    gpc = S5_CHUNK // n
    n_chunks = g // gpc
    eye = jnp.eye(gpc, dtype=F32)
    chunks_per_in_tile = LANE // (gpc * p)
    chunks_per_out_tile = S5_CHUNK // (gpc * p)

    def pack_in(bb):
        blk = jnp.einsum("qgpn,gh->qgphn", bb.reshape(n_chunks, gpc, p, n), eye)
        blk = blk.reshape(n_chunks, gpc * p, S5_CHUNK)
        sel = jax.nn.one_hot(jnp.arange(n_chunks) % chunks_per_in_tile, chunks_per_in_tile, dtype=F32)
        return jnp.einsum("qrc,qs->qsrc", blk, sel).reshape(n_chunks, LANE, S5_CHUNK).astype(BF16)

    def pack_out(cc):
        blk = jnp.einsum("qgnp,gh->qgnhp", cc.transpose(0, 2, 1).reshape(n_chunks, gpc, n, p), eye)
        blk = blk.reshape(n_chunks, S5_CHUNK, gpc * p)
        sel = jax.nn.one_hot(jnp.arange(n_chunks) % chunks_per_out_tile, chunks_per_out_tile, dtype=F32)
        return jnp.einsum("qrc,qs->qrsc", blk, sel).reshape(n_chunks, S5_CHUNK, S5_CHUNK).astype(BF16)

    return {
        "bb_re": pack_in(bb_re), "bb_im": pack_in(bb_im),
        "c_re": pack_out(c_re), "c_imn": pack_out(-c_im),
        "a_re8": _state_tiles(ab_re), "a_im8": _state_tiles(ab_im),
        "a_re1": ab_re.reshape(1, g * n), "a_im1": ab_im.reshape(1, g * n),
        "d_row": d.reshape(1, g * p),
    }


TM = 1024
TN = 512
TM_FFN = 1024
TF = 512
TM_OUT = 512
TQ = 512
S5_LT = 512


def _gain_row(*pieces):
    return jnp.concatenate([jnp.tile(g, reps) for g, reps in pieces]).reshape(1, -1).astype(F32)


def _head_rows(x, n_heads):
    x = x.reshape(x.shape[0], n_heads, LANE)
    return jnp.pad(x, ((0, 0), (0, HEAD_ROWS - n_heads), (0, 0)))


def kernel(x_prompt, x_sample, cache_fox_k, cache_fox_v, cache_fox_logf, state_s5_re, state_s5_im, cache_mem_k, cache_mem_v, page_table, mem_prompt, norm1_g, w_out, mem_norm_g, w_mem_kv, xq_norm_g, xk_norm_g, norm2_g, w_ffn_gate, w_ffn_up, w_ffn_down, fox_w_in, fox_b_f, fox_q_norm_g, fox_k_norm_g, s5_w_in, s5_a_re, s5_a_im, s5_log_dt, s5_b_re, s5_b_im, s5_c_re, s5_c_im, s5_d, s5_w_glu, s5_b_glu):
    bp, tp, d = x_prompt.shape
    bs, ts, _ = x_sample.shape
    depth = norm1_g.shape[0]
    n_fox, n_pool, page, fox_h, fox_hd = cache_fox_k.shape
    fox_w = fox_h * fox_hd
    n_mem, xa_h, xa_hd = cache_mem_k.shape[2:]
    xa_w = xa_h * xa_hd
    s5_g, s5_n = s5_a_re.shape[1:]
    mp, ms = bp * tp, bs * ts
    assert ts == 1 and fox_hd == LANE and xa_hd == LANE

    xp = x_prompt.reshape(mp, d)
    xs = x_sample.reshape(ms, d)
    mem = mem_prompt.reshape(bp * n_mem, d)
    ck = cache_fox_k.transpose(0, 1, 3, 2, 4).reshape(n_fox * n_pool, fox_h, page, fox_hd)
    cv = cache_fox_v.transpose(0, 1, 3, 2, 4).reshape(n_fox * n_pool, fox_h, page, fox_hd)
    clf = cache_fox_logf.transpose(0, 3, 1, 2).reshape(n_fox * fox_h, n_pool, page)
    cmk = cache_mem_k.reshape(depth, bs, n_mem * xa_h, xa_hd)
    cmv = cache_mem_v.reshape(depth, bs, n_mem * xa_h, xa_hd)

    outs = {k: [] for k in ("fk_p", "fv_p", "fl_p", "fk_s", "fv_s", "fl_s",
                            "sr_p", "si_p", "sr_s", "si_s", "mk_p", "mv_p")}
    fox_nt, xa_nt = fox_w // TN, xa_w // TN

    wo16 = w_out.astype(BF16)
    wkv16 = w_mem_kv.astype(BF16)
    wg16, wu16, wd16 = w_ffn_gate.astype(BF16), w_ffn_up.astype(BF16), w_ffn_down.astype(BF16)

    for i in range(depth):
        j = i // 2

        gain = _gain_row((xk_norm_g[i], xa_h), (jnp.ones((xa_hd,), F32), xa_h))
        segs = ((0, xa_nt, True, (0, 1)), (xa_nt, xa_nt, False, (2, 3)))
        defs = ((F32, 0, xa_nt, 0), (BF16, 0, xa_nt, 0), (F32, xa_nt, xa_nt, 0), (BF16, xa_nt, xa_nt, 0))
        mk32, mk16, mv32, mv16 = _proj(mem, mem_norm_g[i], wkv16, gain, segs, defs,
                                       tm=bp * n_mem, tn=TN, layer=i)
        outs["mk_p"].append(mk32.reshape(bp, n_mem, xa_h, xa_hd))
        outs["mv_p"].append(mv32.reshape(bp, n_mem, xa_h, xa_hd))

        if i % 2 == 0:
            w_in = fox_w_in[j]
            w_main = jnp.concatenate([w_in[:, :3 * fox_w], w_in[:, 3 * fox_w + fox_h:]], axis=1).astype(BF16)
            wf = jnp.pad(w_in[:, 3 * fox_w:3 * fox_w + fox_h], ((0, 0), (0, LANE - fox_h))).astype(BF16)
            bf = jnp.pad(fox_b_f[j], (0, LANE - fox_h)).reshape(1, LANE)
            gain = _gain_row((fox_q_norm_g[j] * Q_PRESCALE, fox_h), (fox_k_norm_g[j], fox_h),
                             (jnp.ones((fox_hd,), F32), fox_h), (xq_norm_g[i], xa_h))
            segs = ((0, fox_nt, True, (0,)), (fox_nt, fox_nt, True, (1, 2)),
                    (2 * fox_nt, fox_nt, False, (3, 4)), (3 * fox_nt, xa_nt, True, (5,)))
            defs = ((BF16, 0, fox_nt, 0), (F32, fox_nt, fox_nt, page), (BF16, fox_nt, fox_nt, 0),
                    (F32, 2 * fox_nt, fox_nt, page), (BF16, 2 * fox_nt, fox_nt, 0),
                    (BF16, 3 * fox_nt, xa_nt, 0))
            q16, k32, k16, v32, v16, xq16, lf_p, qa, ka = _proj(
                xp, norm1_g[i], w_main, gain, segs, defs, tm=TM, tn=TN, wf=wf, bf=bf, seq_len=tp,
                n_heads=fox_h)
            outs["fk_p"].append(k32.transpose(0, 2, 1, 3))
            outs["fv_p"].append(v32.transpose(0, 2, 1, 3))
            outs["fl_p"].append(lf_p[:, :fox_h].reshape(mp // page, page, fox_h))
            mix_p = _flash(q16.reshape(bp, tp, fox_w), qa.reshape(bp, tp, LANE), k16.reshape(bp, tp, fox_w),
                           ka.reshape(bp, tp, LANE), v16.reshape(bp, tp, fox_w), tq=TQ).reshape(mp, fox_w)

            defs_s = ((F32, 0, fox_nt, 0), (F32, fox_nt, fox_nt, 0),
                      (F32, 2 * fox_nt, fox_nt, 0), (F32, 3 * fox_nt, xa_nt, 0))
            segs_s = ((0, fox_nt, True, (0,)), (fox_nt, fox_nt, True, (1,)),
                      (2 * fox_nt, fox_nt, False, (2,)), (3 * fox_nt, xa_nt, True, (3,)))
            qs, ks, vs, xqs, lf_s = _proj(xs, norm1_g[i], w_main, gain, segs_s, defs_s,
                                          tm=ms, tn=TN, wf=wf, bf=bf)
            outs["fk_s"].append(ks.reshape(bs, ts, fox_h, fox_hd))
            outs["fv_s"].append(vs.reshape(bs, ts, fox_h, fox_hd))
            outs["fl_s"].append(lf_s[:, :fox_h].reshape(bs, ts, fox_h))
            lf_col = lf_s[:, :HEAD_ROWS].reshape(bs, HEAD_ROWS, 1)
            mix_s = _dec_fox(qs.reshape(bs, 1, fox_w), ks.reshape(bs, 1, fox_w), vs.reshape(bs, 1, fox_w),
                             lf_col, ck, cv, clf, page_table, j).reshape(ms, fox_w)
        else:
            w_main = s5_w_in[j].astype(BF16)
            s5_nt = (w_main.shape[1] - xa_w) // TN
            gain = _gain_row((jnp.ones((LANE,), F32), s5_nt * TN // LANE), (xq_norm_g[i], xa_h))
            segs = ((0, s5_nt, False, (0,)), (s5_nt, xa_nt, True, (1,)))
            pk = _s5_pack(s5_a_re[j], s5_a_im[j], s5_log_dt[j], s5_b_re[j], s5_b_im[j],
                          s5_c_re[j], s5_c_im[j], s5_d[j])
            w_glu = s5_w_glu[j].astype(BF16)
            b_glu = s5_b_glu[j].reshape(1, -1)
            u_p, xq16 = _proj(xp, norm1_g[i], w_main, gain, segs,
                              ((F32, 0, s5_nt, 0), (BF16, s5_nt, xa_nt, 0)), tm=TM, tn=TN)
            s5_w = u_p.shape[1]
            mix_p, hr, hi = _s5_prompt(u_p.reshape(bp, tp, s5_w), pk, w_glu, b_glu, lt=S5_LT)
            mix_p = mix_p.reshape(mp, s5_w)
            outs["sr_p"].append(_state_untile(hr, s5_g, s5_n))
            outs["si_p"].append(_state_untile(hi, s5_g, s5_n))
            u_s, xqs = _proj(xs, norm1_g[i], w_main, gain, segs,
                             ((F32, 0, s5_nt, 0), (F32, s5_nt, xa_nt, 0)), tm=ms, tn=TN)
            mix_s, hr, hi = _s5_sample(u_s, state_s5_re[j].reshape(bs, s5_g * s5_n),
                                       state_s5_im[j].reshape(bs, s5_g * s5_n), pk, w_glu, b_glu)
            outs["sr_s"].append(hr.reshape(bs, s5_g, s5_n))
            outs["si_s"].append(hi.reshape(bs, s5_g, s5_n))

        xa_p = _xattn(xq16.reshape(bp, tp, xa_w), mk16.reshape(bp, n_mem, xa_w),
                      mv16.reshape(bp, n_mem, xa_w), tq=TQ).reshape(mp, xa_w)
        xa_s = _dec_xattn(_head_rows(xqs, xa_h), cmk, cmv, i, xa_h)[:, :xa_h].reshape(ms, xa_w)

        xp = _outproj(mix_p, xa_p, wo16, i, xp, tm=TM_OUT, tn=d)
        xs = _outproj(mix_s, xa_s, wo16, i, xs, tm=ms, tn=d)
        xp = _ffn(xp, norm2_g[i], wg16, wu16, wd16, i, tm=TM_FFN, tf=TF)
        xs = _ffn(xs, norm2_g[i], wg16, wu16, wd16, i, tm=ms, tf=TF)

    st = lambda k: jnp.stack(outs[k])
    return (xp.reshape(bp, tp, d), xs.reshape(bs, ts, d),
            st("fk_p"), st("fv_p"), st("fl_p"), st("fk_s"), st("fv_s"), st("fl_s"),
            st("sr_p"), st("si_p"), st("sr_s"), st("si_s"), st("mk_p"), st("mv_p"))
```

```python
import functools
import math

import jax
import jax.numpy as jnp
import numpy as np
from jax import lax
from jax.experimental import pallas as pl
from jax.experimental.pallas import tpu as pltpu

F32 = jnp.float32
BF16 = jnp.bfloat16

EPS = 1e-6
NEG_INF = -1e30
LANE = 128
SUBLANE = 8
HEAD_ROWS = 16
VMEM_LIMIT = 52 * 1024 * 1024
VMEM_LIMIT_BIG = 58 * 1024 * 1024
NT_DIMS = (((1,), (1,)), ((), ()))


def _params(sem, vmem=VMEM_LIMIT):
    return pltpu.CompilerParams(dimension_semantics=sem, vmem_limit_bytes=vmem)


def _split3(x):
    hi = x.astype(BF16)
    r1 = x - hi.astype(F32)
    mid = r1.astype(BF16)
    lo = (r1 - mid.astype(F32)).astype(BF16)
    return hi, mid, lo


def _dot_exact01(x, w01):
    r = jnp.dot(jnp.concatenate(_split3(x), axis=0), w01, preferred_element_type=F32)
    return r[0:HEAD_ROWS] + r[HEAD_ROWS:2 * HEAD_ROWS] + r[2 * HEAD_ROWS:3 * HEAD_ROWS]


def _log_sigmoid(x):
    return jnp.minimum(x, 0.0) - jnp.log1p(jnp.exp(-jnp.abs(x)))


def _head_rmsnorm(y, gain_row):
    parts = []
    for c in range(y.shape[-1] // LANE):
        p = y[:, c * LANE:(c + 1) * LANE]
        parts.append(p * lax.rsqrt(jnp.mean(p * p, axis=-1, keepdims=True) + EPS))
    return jnp.concatenate(parts, axis=-1) * gain_row


AUG_LANES = 8
SQRT2 = math.sqrt(2.0)
Q_PRESCALE = 0.125
CUMSUM_ROWS = 512


def _lane_split3(x):
    return jnp.concatenate(_split3(x), axis=-1)


def _lane_sum3(r):
    return r[:, 0:LANE] + r[:, LANE:2 * LANE] + r[:, 2 * LANE:3 * LANE]


def _proj_kernel(*refs, segs, n_out, with_f, cumsum, seq_tiles):
    x_ref, g_ref, w_ref, gain_ref = refs[:4]
    pos = 4
    if with_f:
        wf_ref, bf_ref = refs[4:6]
        pos = 6
        if cumsum:
            wqa_ref, wka_ref, oneq_ref, onek_ref = refs[6:10]
            pos = 10
    out_refs = refs[pos:pos + n_out]
    pos += n_out
    if with_f:
        lf_ref = refs[pos]
        pos += 1
        if cumsum:
            qa_ref, ka_ref = refs[pos:pos + 2]
            pos += 2
    h_scr = refs[pos]
    pos += 1
    if with_f and cumsum:
        tri_scr, carry_scr = refs[pos:pos + 2]

    i = pl.program_id(0)
    j = pl.program_id(1)
    tm = x_ref.shape[0]

    if with_f and cumsum:
        ct = tri_scr.shape[0]

        @pl.when((i == 0) & (j == 0))
        def _():
            t = lax.broadcasted_iota(jnp.int32, (ct, ct), 0)
            u = lax.broadcasted_iota(jnp.int32, (ct, ct), 1)
            tri_scr[...] = jnp.where(u <= t, 1.0, 0.0).astype(BF16)

    @pl.when(j == 0)
    def _():
        x = x_ref[...]
        h = x * lax.rsqrt(jnp.mean(x * x, axis=-1, keepdims=True) + EPS) * g_ref[...]
        hb = h.astype(BF16)
        h_scr[...] = hb
        if with_f:
            lf = _log_sigmoid(jnp.dot(hb, wf_ref[...], preferred_element_type=F32) + bf_ref[...])
            lf_ref[...] = lf
            if cumsum:
                @pl.when(i % seq_tiles == 0)
                def _():
                    carry_scr[...] = jnp.zeros_like(carry_scr)
                carry = carry_scr[...]
                for s in range(tm // ct):
                    rs = slice(s * ct, (s + 1) * ct)
                    c = _lane_sum3(jnp.dot(tri_scr[...], _lane_split3(lf[rs]), preferred_element_type=F32))
                    c = c + carry
                    carry = c[ct - 1:ct, :]
                    pieces = _lane_split3(c * SQRT2)
                    qa_ref[rs, :] = (jnp.dot(pieces, wqa_ref[...], preferred_element_type=F32)
                                     + oneq_ref[...]).astype(qa_ref.dtype)
                    ka_ref[rs, :] = (jnp.dot(pieces, wka_ref[...], preferred_element_type=F32)
                                     + onek_ref[...]).astype(ka_ref.dtype)
                carry_scr[...] = carry

    n_sub = 2 if tm % (2 * HEAD_ROWS) == 0 and tm >= 2 * LANE else 1
    rows = tm // n_sub
    for start, n_tiles, norm, outs in segs:
        @pl.when((j >= start) & (j < start + n_tiles))
        def _(norm=norm, outs=outs):
            for r in range(n_sub):
                rs = slice(r * rows, (r + 1) * rows)
                y = jnp.dot(h_scr[rs, :], w_ref[...], preferred_element_type=F32)
                yy = _head_rmsnorm(y, gain_ref[...]) if norm else y
                for o in outs:
                    if len(out_refs[o].shape) == 4:
                        page = out_refs[o].shape[2]
                        for pg in range(rows // page):
                            for hh in range(yy.shape[1] // LANE):
                                out_refs[o][r * (rows // page) + pg, hh] = (
                                    yy[pg * page:(pg + 1) * page, hh * LANE:(hh + 1) * LANE])
                    else:
                        out_refs[o][rs, :] = yy.astype(out_refs[o].dtype)


def _aug_maps(n_heads):
    wqa = np.zeros((3 * LANE, LANE), np.float32)
    wka = np.zeros((3 * LANE, LANE), np.float32)
    oneq = np.zeros((1, LANE), np.float32)
    onek = np.zeros((1, LANE), np.float32)
    for h in range(n_heads):
        for piece in range(3):
            wqa[piece * LANE + h, AUG_LANES * h + piece] = 1.0
            wka[piece * LANE + h, AUG_LANES * h + 3 + piece] = -1.0
            oneq[0, AUG_LANES * h + 3 + piece] = 1.0
            onek[0, AUG_LANES * h + piece] = 1.0
    return jnp.asarray(wqa, BF16), jnp.asarray(wka, BF16), jnp.asarray(oneq), jnp.asarray(onek)


def _proj(x, g, w, gain_row, segs, out_defs, *, tm, tn, layer=None, wf=None, bf=None, seq_len=None,
          n_heads=None):
    m, k = x.shape
    n = w.shape[-1]
    with_f = wf is not None
    cumsum = seq_len is not None
    grid = (m // tm, n // tn)
    const = lambda shape: pl.BlockSpec(shape, lambda i, j: (0,) * len(shape))
    w_spec = (pl.BlockSpec((k, tn), lambda i, j: (0, j)) if layer is None
              else pl.BlockSpec((None, k, tn), lambda i, j: (layer, 0, j)))
    in_specs = [pl.BlockSpec((tm, k), lambda i, j: (i, 0)), const((1, k)), w_spec,
                pl.BlockSpec((1, tn), lambda i, j: (0, j))]
    args = [x, g.reshape(1, k), w, gain_row]
    if with_f:
        in_specs += [const((k, LANE)), const((1, LANE))]
        args += [wf, bf]
        if cumsum:
            maps = _aug_maps(n_heads)
            in_specs += [const(a.shape) for a in maps]
            args += list(maps)
    out_shapes, out_specs = [], []
    for dt, start, n_tiles, page in out_defs:
        if page:
            out_shapes.append(jax.ShapeDtypeStruct((m // page, n_tiles * tn // LANE, page, LANE), dt))
            out_specs.append(pl.BlockSpec(
                (tm // page, tn // LANE, page, LANE),
                lambda i, j, s=start, nt=n_tiles: (i, jnp.clip(j - s, 0, nt - 1), 0, 0)))
        else:
            out_shapes.append(jax.ShapeDtypeStruct((m, n_tiles * tn), dt))
            out_specs.append(pl.BlockSpec(
                (tm, tn), lambda i, j, s=start, nt=n_tiles: (i, jnp.clip(j - s, 0, nt - 1))))
    scratch = [pltpu.VMEM((tm, k), BF16)]
    seq_tiles = 1
    if with_f:
        row_spec = pl.BlockSpec((tm, LANE), lambda i, j: (i, 0))
        out_shapes.append(jax.ShapeDtypeStruct((m, LANE), F32))
        out_specs.append(row_spec)
        if cumsum:
            seq_tiles = seq_len // tm
            ct = math.gcd(tm, CUMSUM_ROWS)
            out_shapes += [jax.ShapeDtypeStruct((m, LANE), BF16)] * 2
            out_specs += [row_spec] * 2
            scratch += [pltpu.VMEM((ct, ct), BF16), pltpu.VMEM((1, LANE), F32)]
    kern = functools.partial(_proj_kernel, segs=segs, n_out=len(out_defs), with_f=with_f,
                             cumsum=cumsum, seq_tiles=seq_tiles)
    return pl.pallas_call(
        kern, grid=grid, in_specs=in_specs, out_specs=out_specs, out_shape=out_shapes,
        scratch_shapes=scratch,
        compiler_params=_params(("arbitrary", "arbitrary"), VMEM_LIMIT_BIG if tm >= 1024 else VMEM_LIMIT),
        name="norm_proj")(*args)


LOG2_SCALE = math.log2(math.e) / SQRT2


def _flash_kernel(qi_ref, kj_ref, q_ref, qa_ref, k_ref, ka_ref, v_ref, o_ref,
                  qaug_scr, m_scr, l_scr, acc_scr):
    hg = pl.program_id(1)
    i = qi_ref[pl.program_id(2)]
    j = kj_ref[pl.program_id(2)]
    tq, tk = q_ref.shape[0], k_ref.shape[0]
    n_hp = q_ref.shape[1] // LANE

    @pl.when(j == 0)
    def _():
        lane = lax.broadcasted_iota(jnp.int32, (tq, LANE), 1)
        for hp in range(n_hp):
            qaug_scr[hp, :, 0:LANE] = q_ref[:, hp * LANE:(hp + 1) * LANE]
            qaug_scr[hp, :, LANE:2 * LANE] = jnp.where(lane // AUG_LANES == hg * n_hp + hp, qa_ref[...],
                                                       jnp.zeros_like(qa_ref))
        m_scr[...] = jnp.full_like(m_scr, NEG_INF)
        l_scr[...] = jnp.zeros_like(l_scr)
        acc_scr[...] = jnp.zeros_like(acc_scr)

    def step(masked):
        for hp in range(n_hp):
            hs = slice(hp * LANE, (hp + 1) * LANE)
            k_aug = jnp.concatenate([k_ref[:, hs], ka_ref[...]], axis=-1)
            x = lax.dot_general(qaug_scr[hp], k_aug, NT_DIMS, preferred_element_type=F32) * LOG2_SCALE
            if masked:
                row = lax.broadcasted_iota(jnp.int32, x.shape, 0)
                col = lax.broadcasted_iota(jnp.int32, x.shape, 1)
                x = jnp.where(col <= row, x, NEG_INF)
            m_old = m_scr[hp]
            m_new = jnp.maximum(m_old, jnp.max(x, axis=-1, keepdims=True))
            alpha = jnp.exp2(m_old - m_new)
            l_part = alpha * l_scr[hp]
            ps = []
            for c in range(tk // LANE):
                p = jnp.exp2(x[:, c * LANE:(c + 1) * LANE] - m_new)
                l_part = l_part + p
                ps.append(p.astype(BF16))
            l_scr[hp] = l_part
            m_scr[hp] = m_new
            acc_scr[hp] = alpha * acc_scr[hp] + jnp.dot(jnp.concatenate(ps, axis=-1), v_ref[:, hs],
                                                        preferred_element_type=F32)

    @pl.when(j < i)
    def _():
        step(False)

    @pl.when(j == i)
    def _():
        step(True)
        for hp in range(n_hp):
            o_ref[:, hp * LANE:(hp + 1) * LANE] = (
                acc_scr[hp] / jnp.sum(l_scr[hp], axis=-1, keepdims=True)).astype(o_ref.dtype)


FLASH_HEADS_PER_STEP = 12


def _flash(q, qa, k, ka, v, *, tq):
    b, t, w = q.shape
    n_hp = FLASH_HEADS_PER_STEP
    hw = n_hp * LANE
    assert w % hw == 0
    nq = t // tq
    pairs = [(i, j) for i in range(nq) for j in range(i + 1)]
    qi = jnp.asarray([p[0] for p in pairs], jnp.int32)
    kj = jnp.asarray([p[1] for p in pairs], jnp.int32)
    q_map = lambda bb, hh, s, qi, kj: (bb, qi[s], hh)
    kv_map = lambda bb, hh, s, qi, kj: (bb, kj[s], hh)
    grid_spec = pltpu.PrefetchScalarGridSpec(
        num_scalar_prefetch=2, grid=(b, w // hw, len(pairs)),
        in_specs=[
            pl.BlockSpec((None, tq, hw), q_map),
            pl.BlockSpec((None, tq, LANE), lambda bb, hh, s, qi, kj: (bb, qi[s], 0)),
            pl.BlockSpec((None, tq, hw), kv_map),
            pl.BlockSpec((None, tq, LANE), lambda bb, hh, s, qi, kj: (bb, kj[s], 0)),
            pl.BlockSpec((None, tq, hw), kv_map),
        ],
        out_specs=pl.BlockSpec((None, tq, hw), q_map),
        scratch_shapes=[pltpu.VMEM((n_hp, tq, 2 * LANE), BF16), pltpu.VMEM((n_hp, tq, LANE), F32),
                        pltpu.VMEM((n_hp, tq, LANE), F32), pltpu.VMEM((n_hp, tq, LANE), F32)])
    return pl.pallas_call(
        _flash_kernel, grid_spec=grid_spec, out_shape=jax.ShapeDtypeStruct((b, t, w), BF16),
        compiler_params=_params(("arbitrary",) * 3), name="fox_prompt_attn")(qi, kj, q, qa, k, ka, v)


def _head_diag(width):
    rows = lax.broadcasted_iota(jnp.int32, (HEAD_ROWS, width), 0)
    lanes = lax.broadcasted_iota(jnp.int32, (HEAD_ROWS, width), 1)
    return (lanes // LANE) == rows


def _dec_fox_kernel(pt_ref, q_ref, kn_ref, vn_ref, lfn_ref, *refs, scale, n_pages, pps):
    kv_refs, lf_refs = refs[:2 * pps], refs[2 * pps:3 * pps]
    o_ref, qbd_scr, m_scr, l_scr, acc_scr, carry_scr = refs[3 * pps:]
    b = pl.program_id(0)
    step = pl.program_id(1)
    n_heads, keys, _ = kv_refs[0].shape
    width = n_heads * LANE
    pair = 2 * LANE
    diag = _head_diag(width)

    @pl.when(step == 0)
    def _():
        qbd = jnp.where(diag, jnp.broadcast_to(q_ref[...], (HEAD_ROWS, width)), 0.0).astype(BF16)
        qbd_scr[...] = qbd
        kn = kn_ref[...].astype(BF16).astype(F32)
        m_scr[...] = jnp.sum(qbd.astype(F32) * kn, axis=-1, keepdims=True) * scale
        l_scr[...] = jnp.ones_like(l_scr)
        acc_scr[...] = jnp.broadcast_to(vn_ref[...].astype(BF16).astype(F32), (HEAD_ROWS, width))
        carry_scr[...] = lfn_ref[...]

    u = lax.broadcasted_iota(jnp.int32, (keys, keys), 0)
    kk = lax.broadcasted_iota(jnp.int32, (keys, keys), 1)
    suffix = jnp.where(u >= kk, 1.0, 0.0).astype(BF16)
    sel_h = lax.broadcasted_iota(jnp.int32, (HEAD_ROWS, n_heads * SUBLANE), 0)
    sel_c = lax.broadcasted_iota(jnp.int32, (HEAD_ROWS, n_heads * SUBLANE), 1)

    scores, incls, lfts = [], [], []
    for i in range(pps):
        k_ref, lf_ref = kv_refs[2 * i], lf_refs[i]
        s = jnp.zeros((HEAD_ROWS, keys), F32)
        for c in range(n_heads // 2):
            kp = jnp.concatenate([k_ref[2 * c], k_ref[2 * c + 1]], axis=-1).astype(BF16)
            s += lax.dot_general(qbd_scr[:, c * pair:(c + 1) * pair], kp, NT_DIMS,
                                 preferred_element_type=F32)
        scores.append(s * scale)
        r = pt_ref[b, n_pages - 1 - (step * pps + i)] % SUBLANE
        sel = jnp.where((sel_c // SUBLANE == sel_h) & (sel_c % SUBLANE == r), 1.0, 0.0).astype(BF16)
        lft = sum(jnp.dot(sel, part, preferred_element_type=F32)
                  for part in _split3(lf_ref[...].reshape(n_heads * SUBLANE, keys)))
        lfts.append(lft)
        incls.append(_dot_exact01(lft, suffix))

    carry = carry_scr[...]
    logits = []
    for i in range(pps):
        logits.append(scores[i] + (carry + incls[i] - lfts[i]))
        carry = carry + incls[i][:, 0:1]
    carry_scr[...] = carry
    logits = jnp.concatenate(logits, axis=-1)

    m_old = m_scr[...]
    m_new = jnp.maximum(m_old, jnp.max(logits, axis=-1, keepdims=True))
    alpha = jnp.exp(m_old - m_new)
    pw = jnp.exp(logits - m_new)
    l_scr[...] = alpha * l_scr[...] + jnp.sum(pw, axis=-1, keepdims=True)
    m_scr[...] = m_new
    pb = pw.astype(BF16)
    for c in range(n_heads // 2):
        vp = jnp.concatenate(
            [jnp.concatenate([kv_refs[2 * i + 1][2 * c], kv_refs[2 * i + 1][2 * c + 1]], axis=-1)
             for i in range(pps)], axis=0).astype(BF16)
        sl = slice(c * pair, (c + 1) * pair)
        acc_scr[:, sl] = alpha * acc_scr[:, sl] + jnp.dot(pb, vp, preferred_element_type=F32)

    @pl.when(step == n_pages // pps - 1)
    def _():
        o_ref[...] = jnp.sum(jnp.where(diag, acc_scr[...] / l_scr[...], 0.0), axis=0, keepdims=True)


DEC_PAGES_PER_STEP = 16


def _dec_fox(q, k_new, v_new, lf_new_col, cache_k, cache_v, cache_lf, page_table, layer):
    b, _, w = q.shape
    n_pages = page_table.shape[1]
    n_heads, page = cache_k.shape[1:3]
    n_pool = cache_lf.shape[1]
    pps = DEC_PAGES_PER_STEP
    assert n_pages % pps == 0 and n_heads % 2 == 0 and n_pool % SUBLANE == 0

    def page_id(bb, p, pt, i):
        return pt[bb, n_pages - 1 - (p * pps + i)]

    row_map = lambda bb, p, pt: (bb, 0, 0)
    row_spec = pl.BlockSpec((None, 1, w), row_map)
    kv_specs, lf_specs = [], []
    for i in range(pps):
        kv_map = lambda bb, p, pt, i=i: (page_id(bb, p, pt, i) + layer * n_pool, 0, 0, 0)
        kv_specs += [pl.BlockSpec((None, n_heads, page, LANE), kv_map)] * 2
        lf_specs.append(pl.BlockSpec((n_heads, SUBLANE, page),
                                     lambda bb, p, pt, i=i: (layer, page_id(bb, p, pt, i) // SUBLANE, 0)))
    kern = functools.partial(_dec_fox_kernel, scale=LANE ** -0.5 / Q_PRESCALE, n_pages=n_pages, pps=pps)
    grid_spec = pltpu.PrefetchScalarGridSpec(
        num_scalar_prefetch=1, grid=(b, n_pages // pps),
        in_specs=[row_spec, row_spec, row_spec, pl.BlockSpec((None, HEAD_ROWS, 1), row_map)]
        + kv_specs + lf_specs,
        out_specs=row_spec,
        scratch_shapes=[pltpu.VMEM((HEAD_ROWS, w), BF16), pltpu.VMEM((HEAD_ROWS, 1), F32),
                        pltpu.VMEM((HEAD_ROWS, 1), F32), pltpu.VMEM((HEAD_ROWS, w), F32),
                        pltpu.VMEM((HEAD_ROWS, 1), F32)])
    return pl.pallas_call(
        kern, grid_spec=grid_spec, out_shape=jax.ShapeDtypeStruct((b, 1, w), F32),
        compiler_params=_params(("arbitrary", "arbitrary"), VMEM_LIMIT_BIG), name="fox_decode_attn")(
            page_table, q, k_new, v_new, lf_new_col, *([cache_k, cache_v] * pps), *([cache_lf] * pps))


def _dec_xattn_kernel(q_ref, k_ref, v_ref, o_ref, *, scale, n_heads):
    for b in range(q_ref.shape[0]):
        s = lax.dot_general(q_ref[b].astype(BF16), k_ref[b].astype(BF16), NT_DIMS,
                            preferred_element_type=F32) * scale
        row = lax.broadcasted_iota(jnp.int32, s.shape, 0)
        col = lax.broadcasted_iota(jnp.int32, s.shape, 1)
        logits = jnp.where(col % n_heads == row, s, NEG_INF)
        p = jnp.exp(logits - jnp.max(logits, axis=-1, keepdims=True))
        p = p / jnp.sum(p, axis=-1, keepdims=True)
        o_ref[b] = jnp.dot(p.astype(BF16), v_ref[b].astype(BF16), preferred_element_type=F32)


DEC_XATTN_SEQS_PER_STEP = 4


def _dec_xattn(q, mem_k, mem_v, layer, n_heads):
    b = q.shape[0]
    n_rows = mem_k.shape[2]
    spb = DEC_XATTN_SEQS_PER_STEP
    assert b % spb == 0
    kern = functools.partial(_dec_xattn_kernel, scale=LANE ** -0.5, n_heads=n_heads)
    head_spec = pl.BlockSpec((spb, HEAD_ROWS, LANE), lambda bb: (bb, 0, 0))
    mem_spec = pl.BlockSpec((None, spb, n_rows, LANE), lambda bb: (layer, bb, 0, 0))
    return pl.pallas_call(
        kern, grid=(b // spb,), in_specs=[head_spec, mem_spec, mem_spec], out_specs=head_spec,
        out_shape=jax.ShapeDtypeStruct((b, HEAD_ROWS, LANE), F32),
        compiler_params=_params(("arbitrary",)), name="mem_xattn_decode")(q, mem_k, mem_v)


def _outproj_kernel(a_ref, xa_ref, w_ref, x_ref, o_ref):
    lhs = jnp.concatenate([a_ref[...].astype(BF16), xa_ref[...].astype(BF16)], axis=-1)
    o_ref[...] = x_ref[...] + jnp.dot(lhs, w_ref[...], preferred_element_type=F32)


def _outproj(a, xa, w, layer, x, *, tm, tn):
    m, d = x.shape
    ka, kx = a.shape[1], xa.shape[1]
    return pl.pallas_call(
        _outproj_kernel, grid=(m // tm, d // tn),
        in_specs=[pl.BlockSpec((tm, ka), lambda i, j: (i, 0)),
                  pl.BlockSpec((tm, kx), lambda i, j: (i, 0)),
                  pl.BlockSpec((None, ka + kx, tn), lambda i, j: (layer, 0, j)),
                  pl.BlockSpec((tm, tn), lambda i, j: (i, j))],
        out_specs=pl.BlockSpec((tm, tn), lambda i, j: (i, j)),
        out_shape=jax.ShapeDtypeStruct((m, d), F32),
        compiler_params=_params(("arbitrary", "arbitrary")), name="out_proj")(a, xa, w, x)


def _outproj_xattn_kernel(a_ref, q_ref, k_ref, v_ref, w_ref, x_ref, o_ref, *, scale):
    parts = [a_ref[...].astype(BF16)]
    for h in range(q_ref.shape[-1] // LANE):
        sl = slice(h * LANE, (h + 1) * LANE)
        s = lax.dot_general(q_ref[:, sl], k_ref[:, sl], NT_DIMS, preferred_element_type=F32) * scale
        p = jnp.exp(s - jnp.max(s, axis=-1, keepdims=True))
        p = p / jnp.sum(p, axis=-1, keepdims=True)
        parts.append(jnp.dot(p.astype(BF16), v_ref[:, sl], preferred_element_type=F32).astype(BF16))
    o_ref[...] = x_ref[...] + jnp.dot(jnp.concatenate(parts, axis=-1), w_ref[...],
                                      preferred_element_type=F32)


def _outproj_xattn(a, q, k, v, w, layer, x, *, tm, seq_len):
    m, d = x.shape
    ka, kx = a.shape[1], q.shape[1]
    n_mem = k.shape[1]
    tiles_per_seq = seq_len // tm
    mem_spec = pl.BlockSpec((None, n_mem, kx), lambda i: (i // tiles_per_seq, 0, 0))
    kern = functools.partial(_outproj_xattn_kernel, scale=LANE ** -0.5)
    return pl.pallas_call(
        kern, grid=(m // tm,),
        in_specs=[pl.BlockSpec((tm, ka), lambda i: (i, 0)),
                  pl.BlockSpec((tm, kx), lambda i: (i, 0)),
                  mem_spec, mem_spec,
                  pl.BlockSpec((None, ka + kx, d), lambda i: (layer, 0, 0)),
                  pl.BlockSpec((tm, d), lambda i: (i, 0))],
        out_specs=pl.BlockSpec((tm, d), lambda i: (i, 0)),
        out_shape=jax.ShapeDtypeStruct((m, d), F32),
        compiler_params=_params(("arbitrary",)), name="xattn_out_proj")(a, q, k, v, w, x)


def _ffn_kernel(x_ref, g_ref, wg_ref, wu_ref, wd_ref, o_ref, h_scr):
    f = pl.program_id(1)

    @pl.when(f == 0)
    def _():
        x = x_ref[...]
        h = x * lax.rsqrt(jnp.mean(x * x, axis=-1, keepdims=True) + EPS) * g_ref[...]
        h_scr[...] = h.astype(BF16)
        o_ref[...] = x

    h = h_scr[...]
    gate = jnp.dot(h, wg_ref[...], preferred_element_type=F32)
    up = jnp.dot(h, wu_ref[...], preferred_element_type=F32)
    act = (gate * jax.nn.sigmoid(gate)) * up
    o_ref[...] += jnp.dot(act.astype(BF16), wd_ref[...], preferred_element_type=F32)


def _ffn(x, g, w_gate, w_up, w_down, layer, *, tm, tf):
    m, d = x.shape
    d_ff = w_gate.shape[-1]
    return pl.pallas_call(
        _ffn_kernel, grid=(m // tm, d_ff // tf),
        in_specs=[pl.BlockSpec((tm, d), lambda i, f: (i, 0)),
                  pl.BlockSpec((1, d), lambda i, f: (0, 0)),
                  pl.BlockSpec((None, d, tf), lambda i, f: (layer, 0, f)),
                  pl.BlockSpec((None, d, tf), lambda i, f: (layer, 0, f)),
                  pl.BlockSpec((None, tf, d), lambda i, f: (layer, f, 0))],
        out_specs=pl.BlockSpec((tm, d), lambda i, f: (i, 0)),
        out_shape=jax.ShapeDtypeStruct((m, d), F32),
        scratch_shapes=[pltpu.VMEM((tm, d), BF16)],
        compiler_params=_params(("arbitrary", "arbitrary"), VMEM_LIMIT_BIG), name="swiglu_ffn")(
            x, g.reshape(1, d), w_gate, w_up, w_down)


S5_CHUNK = 2 * LANE


def _gelu_tanh(y):
    return 0.5 * y * (1.0 + jnp.tanh(math.sqrt(2.0 / math.pi) * (y + 0.044715 * (y * y * y))))


def _s5_prompt_kernel(u_ref, bbre_ref, bbim_ref, cre_ref, cimn_ref, are_ref, aim_ref, d_ref,
                      wglu_ref, bglu_ref, mix_ref, hre_ref, him_ref,
                      sre, sim, hre_s, him_s, y_scr, *, lt, n_chunks):
    t = pl.program_id(1)
    tiles_per_chunk = S5_CHUNK // LANE
    chunks_per_row = are_ref.shape[0] // tiles_per_chunk
    chunks_per_tile = S5_CHUNK // (u_ref.shape[1] // n_chunks)

    @pl.when(t == 0)
    def _():
        hre_s[...] = jnp.zeros_like(hre_s)
        him_s[...] = jnp.zeros_like(him_s)

    u = u_ref[...]
    ub = u.astype(BF16)

    def put(dst, q, val):
        j, lc = divmod(q, chunks_per_row)
        for k in range(tiles_per_chunk):
            dst[lc * tiles_per_chunk + k, pl.ds(j, lt, stride=SUBLANE), :] = val[:, k * LANE:(k + 1) * LANE]

    def get(src, q):
        j, lc = divmod(q, chunks_per_row)
        return jnp.concatenate([src[lc * tiles_per_chunk + k, pl.ds(j, lt, stride=SUBLANE), :]
                                for k in range(tiles_per_chunk)], axis=-1)

    for q in range(n_chunks):
        ut = ub[:, (q // 2) * LANE:(q // 2 + 1) * LANE]
        put(sre, q, jnp.dot(ut, bbre_ref[q], preferred_element_type=F32))
        put(sim, q, jnp.dot(ut, bbim_ref[q], preferred_element_type=F32))

    a_re = are_ref[...]
    a_im = aim_ref[...]

    def body(tt, carry):
        h_re, h_im = carry
        r0 = pl.multiple_of(tt * SUBLANE, SUBLANE)
        n_re = a_re * h_re - a_im * h_im + sre[:, pl.ds(r0, SUBLANE), :]
        n_im = a_re * h_im + a_im * h_re + sim[:, pl.ds(r0, SUBLANE), :]
        sre[:, pl.ds(r0, SUBLANE), :] = n_re
        sim[:, pl.ds(r0, SUBLANE), :] = n_im
        return n_re, n_im

    h_re, h_im = lax.fori_loop(0, lt, body, (hre_s[...], him_s[...]), unroll=8)
    hre_s[...] = h_re
    him_s[...] = h_im
    hre_ref[...] = h_re
    him_ref[...] = h_im

    for c in range(n_chunks // chunks_per_tile):
        acc = jnp.zeros((lt, S5_CHUNK), F32)
        for q in range(c * chunks_per_tile, (c + 1) * chunks_per_tile):
            acc += jnp.dot(get(sre, q).astype(BF16), cre_ref[q], preferred_element_type=F32)
            acc += jnp.dot(get(sim, q).astype(BF16), cimn_ref[q], preferred_element_type=F32)
        sl = slice(c * S5_CHUNK, (c + 1) * S5_CHUNK)
        y_scr[:, sl] = _gelu_tanh(acc + d_ref[:, sl] * u[:, sl])

    y = y_scr[...]
    gate = jax.nn.sigmoid(jnp.dot(y.astype(BF16), wglu_ref[...], preferred_element_type=F32)
                          + bglu_ref[...])
    mix_ref[...] = (y * gate).astype(mix_ref.dtype)


def _s5_prompt(u, pk, w_glu, b_glu, *, lt):
    b, t, w = u.shape
    n_chunks = pk["bb_re"].shape[0]
    st_shape = pk["a_re8"].shape
    full = lambda a: pl.BlockSpec(a.shape, lambda bb, tt, nd=a.ndim: (0,) * nd)
    consts = [pk["bb_re"], pk["bb_im"], pk["c_re"], pk["c_imn"], pk["a_re8"], pk["a_im8"], pk["d_row"],
              w_glu, b_glu]
    kern = functools.partial(_s5_prompt_kernel, lt=lt, n_chunks=n_chunks)
    st_spec = pl.BlockSpec((None,) + st_shape, lambda bb, tt: (bb, 0, 0, 0))
    return pl.pallas_call(
        kern, grid=(b, t // lt),
        in_specs=[pl.BlockSpec((None, lt, w), lambda bb, tt: (bb, tt, 0))] + [full(a) for a in consts],
        out_specs=[pl.BlockSpec((None, lt, w), lambda bb, tt: (bb, tt, 0)), st_spec, st_spec],
        out_shape=[jax.ShapeDtypeStruct((b, t, w), BF16),
                   jax.ShapeDtypeStruct((b,) + st_shape, F32),
                   jax.ShapeDtypeStruct((b,) + st_shape, F32)],
        scratch_shapes=[pltpu.VMEM((st_shape[0], lt * SUBLANE, LANE), F32),
                        pltpu.VMEM((st_shape[0], lt * SUBLANE, LANE), F32),
                        pltpu.VMEM(st_shape, F32), pltpu.VMEM(st_shape, F32),
                        pltpu.VMEM((lt, w), F32)],
        compiler_params=_params(("arbitrary", "arbitrary"), VMEM_LIMIT_BIG), name="s5_prompt")(u, *consts)


def _s5_sample_kernel(u_ref, h0re_ref, h0im_ref, bbre_ref, bbim_ref, cre_ref, cimn_ref, are_ref, aim_ref,
                      d_ref, wglu_ref, bglu_ref, mix_ref, hre_ref, him_ref, *, n_chunks):
    u = u_ref[...]
    ub = u.astype(BF16)
    chunks_per_tile = S5_CHUNK // (u.shape[1] // n_chunks)
    ys = []
    for c in range(n_chunks // chunks_per_tile):
        acc = jnp.zeros((u.shape[0], S5_CHUNK), F32)
        for q in range(c * chunks_per_tile, (c + 1) * chunks_per_tile):
            ut = ub[:, (q // 2) * LANE:(q // 2 + 1) * LANE]
            sl = slice(q * S5_CHUNK, (q + 1) * S5_CHUNK)
            a_re, a_im = are_ref[:, sl], aim_ref[:, sl]
            h_re, h_im = h0re_ref[:, sl], h0im_ref[:, sl]
            n_re = a_re * h_re - a_im * h_im + jnp.dot(ut, bbre_ref[q], preferred_element_type=F32)
            n_im = a_re * h_im + a_im * h_re + jnp.dot(ut, bbim_ref[q], preferred_element_type=F32)
            hre_ref[:, sl] = n_re
            him_ref[:, sl] = n_im
            acc += jnp.dot(n_re.astype(BF16), cre_ref[q], preferred_element_type=F32)
            acc += jnp.dot(n_im.astype(BF16), cimn_ref[q], preferred_element_type=F32)
        cs = slice(c * S5_CHUNK, (c + 1) * S5_CHUNK)
        ys.append(_gelu_tanh(acc + d_ref[:, cs] * u[:, cs]))
    y = jnp.concatenate(ys, axis=-1)
    gate = jax.nn.sigmoid(jnp.dot(y.astype(BF16), wglu_ref[...], preferred_element_type=F32)
                          + bglu_ref[...])
    mix_ref[...] = (y * gate).astype(mix_ref.dtype)


def _s5_sample(u, h0_re, h0_im, pk, w_glu, b_glu):
    b, w = u.shape
    n_state = h0_re.shape[1]
    args = [u, h0_re, h0_im, pk["bb_re"], pk["bb_im"], pk["c_re"], pk["c_imn"], pk["a_re1"], pk["a_im1"],
            pk["d_row"], w_glu, b_glu]
    kern = functools.partial(_s5_sample_kernel, n_chunks=pk["bb_re"].shape[0])
    return pl.pallas_call(
        kern,
        out_shape=[jax.ShapeDtypeStruct((b, w), BF16), jax.ShapeDtypeStruct((b, n_state), F32),
                   jax.ShapeDtypeStruct((b, n_state), F32)],
        compiler_params=pltpu.CompilerParams(vmem_limit_bytes=VMEM_LIMIT), name="s5_sample")(*args)


def _state_tiles(a):
    return a.reshape(SUBLANE, -1, LANE).transpose(1, 0, 2)


def _state_untile(h, g, n):
    return h.transpose(0, 2, 1, 3).reshape(h.shape[0], g, n)


def _s5_disc_kernel(are_ref, aim_ref, ldt_ref, bre_ref, bim_ref, abre_ref, abim_ref, bbre_ref, bbim_ref):
    a_re, a_im = are_ref[...], aim_ref[...]
    dt = jnp.exp(ldt_ref[...])
    mag = jnp.exp(a_re * dt)
    ab_re = mag * jnp.cos(a_im * dt)
    ab_im = mag * jnp.sin(a_im * dt)
    num_re, num_im = ab_re - 1.0, ab_im
    den = a_re * a_re + a_im * a_im
    z_re = (num_re * a_re + num_im * a_im) / den
    z_im = (num_im * a_re - num_re * a_im) / den
    abre_ref[...] = ab_re
    abim_ref[...] = ab_im
    b_re, b_im = bre_ref[...], bim_ref[...]
    bbre_ref[...] = z_re * b_re - z_im * b_im
    bbim_ref[...] = z_re * b_im + z_im * b_re


def _s5_pack(a_re, a_im, log_dt, b_re, b_im, c_re, c_im, d):
    g, n = a_re.shape
    p = d.shape[1]
    ab_re, ab_im, bb_re, bb_im = pl.pallas_call(
        _s5_disc_kernel,
        out_shape=[jax.ShapeDtypeStruct((g, 1, n), F32)] * 2 + [jax.ShapeDtypeStruct((g, p, n), F32)] * 2,
        name="s5_discretise")(a_re.reshape(g, 1, n), a_im.reshape(g, 1, n), log_dt.reshape(g, 1, 1),
                              b_re.transpose(0, 2, 1), b_im.transpose(0, 2, 1))
    ab_re, ab_im = ab_re.reshape(g, n), ab_im.reshape(g, n)
    gpc = S5_CHUNK // n
    n_chunks = g // gpc
    eye = jnp.eye(gpc, dtype=F32)
    chunks_per_in_tile = LANE // (gpc * p)
    chunks_per_out_tile = S5_CHUNK // (gpc * p)

    def pack_in(bb):
        blk = jnp.einsum("qgpn,gh->qgphn", bb.reshape(n_chunks, gpc, p, n), eye)
        blk = blk.reshape(n_chunks, gpc * p, S5_CHUNK)
        sel = jax.nn.one_hot(jnp.arange(n_chunks) % chunks_per_in_tile, chunks_per_in_tile, dtype=F32)
        return jnp.einsum("qrc,qs->qsrc", blk, sel).reshape(n_chunks, LANE, S5_CHUNK).astype(BF16)

    def pack_out(cc):
        blk = jnp.einsum("qgnp,gh->qgnhp", cc.transpose(0, 2, 1).reshape(n_chunks, gpc, n, p), eye)
        blk = blk.reshape(n_chunks, S5_CHUNK, gpc * p)
        sel = jax.nn.one_hot(jnp.arange(n_chunks) % chunks_per_out_tile, chunks_per_out_tile, dtype=F32)
        return jnp.einsum("qrc,qs->qrsc", blk, sel).reshape(n_chunks, S5_CHUNK, S5_CHUNK).astype(BF16)

    return {
        "bb_re": pack_in(bb_re), "bb_im": pack_in(bb_im),
        "c_re": pack_out(c_re), "c_imn": pack_out(-c_im),
        "a_re8": _state_tiles(ab_re), "a_im8": _state_tiles(ab_im),
        "a_re1": ab_re.reshape(1, g * n), "a_im1": ab_im.reshape(1, g * n),
        "d_row": d.reshape(1, g * p),
    }


TM = 1024
TN = 512
TM_FFN = 1024
TF = 512
TM_OUT = 512
TQ = 512
S5_LT = 512


def _gain_row(*pieces):
    return jnp.concatenate([jnp.tile(g, reps) for g, reps in pieces]).reshape(1, -1).astype(F32)


def _head_rows(x, n_heads):
    x = x.reshape(x.shape[0], n_heads, LANE)
    return jnp.pad(x, ((0, 0), (0, HEAD_ROWS - n_heads), (0, 0)))


def kernel(x_prompt, x_sample, cache_fox_k, cache_fox_v, cache_fox_logf, state_s5_re, state_s5_im, cache_mem_k, cache_mem_v, page_table, mem_prompt, norm1_g, w_out, mem_norm_g, w_mem_kv, xq_norm_g, xk_norm_g, norm2_g, w_ffn_gate, w_ffn_up, w_ffn_down, fox_w_in, fox_b_f, fox_q_norm_g, fox_k_norm_g, s5_w_in, s5_a_re, s5_a_im, s5_log_dt, s5_b_re, s5_b_im, s5_c_re, s5_c_im, s5_d, s5_w_glu, s5_b_glu):
    bp, tp, d = x_prompt.shape
    bs, ts, _ = x_sample.shape
    depth = norm1_g.shape[0]
    n_fox, n_pool, page, fox_h, fox_hd = cache_fox_k.shape
    fox_w = fox_h * fox_hd
    n_mem, xa_h, xa_hd = cache_mem_k.shape[2:]
    xa_w = xa_h * xa_hd
    s5_g, s5_n = s5_a_re.shape[1:]
    mp, ms = bp * tp, bs * ts
    assert ts == 1 and fox_hd == LANE and xa_hd == LANE

    xp = x_prompt.reshape(mp, d)
    xs = x_sample.reshape(ms, d)
    mem = mem_prompt.reshape(bp * n_mem, d)
    ck = cache_fox_k.transpose(0, 1, 3, 2, 4).reshape(n_fox * n_pool, fox_h, page, fox_hd)
    cv = cache_fox_v.transpose(0, 1, 3, 2, 4).reshape(n_fox * n_pool, fox_h, page, fox_hd)
    clf = cache_fox_logf.transpose(0, 3, 1, 2).reshape(n_fox * fox_h, n_pool, page)
    cmk = cache_mem_k.reshape(depth, bs, n_mem * xa_h, xa_hd)
    cmv = cache_mem_v.reshape(depth, bs, n_mem * xa_h, xa_hd)

    outs = {k: [] for k in ("fk_p", "fv_p", "fl_p", "fk_s", "fv_s", "fl_s",
                            "sr_p", "si_p", "sr_s", "si_s", "mk_p", "mv_p")}
    fox_nt, xa_nt = fox_w // TN, xa_w // TN

    wo16 = w_out.astype(BF16)
    wkv16 = w_mem_kv.astype(BF16)
    wg16, wu16, wd16 = w_ffn_gate.astype(BF16), w_ffn_up.astype(BF16), w_ffn_down.astype(BF16)

    for i in range(depth):
        j = i // 2

        gain = _gain_row((xk_norm_g[i], xa_h), (jnp.ones((xa_hd,), F32), xa_h))
        segs = ((0, xa_nt, True, (0, 1)), (xa_nt, xa_nt, False, (2, 3)))
        defs = ((F32, 0, xa_nt, 0), (BF16, 0, xa_nt, 0), (F32, xa_nt, xa_nt, 0), (BF16, xa_nt, xa_nt, 0))
        mk32, mk16, mv32, mv16 = _proj(mem, mem_norm_g[i], wkv16, gain, segs, defs,
                                       tm=bp * n_mem, tn=TN, layer=i)
        outs["mk_p"].append(mk32.reshape(bp, n_mem, xa_h, xa_hd))
        outs["mv_p"].append(mv32.reshape(bp, n_mem, xa_h, xa_hd))

        if i % 2 == 0:
            w_in = fox_w_in[j]
            w_main = jnp.concatenate([w_in[:, :3 * fox_w], w_in[:, 3 * fox_w + fox_h:]], axis=1).astype(BF16)
            wf = jnp.pad(w_in[:, 3 * fox_w:3 * fox_w + fox_h], ((0, 0), (0, LANE - fox_h))).astype(BF16)
            bf = jnp.pad(fox_b_f[j], (0, LANE - fox_h)).reshape(1, LANE)
            gain = _gain_row((fox_q_norm_g[j] * Q_PRESCALE, fox_h), (fox_k_norm_g[j], fox_h),
                             (jnp.ones((fox_hd,), F32), fox_h), (xq_norm_g[i], xa_h))
            segs = ((0, fox_nt, True, (0,)), (fox_nt, fox_nt, True, (1, 2)),
                    (2 * fox_nt, fox_nt, False, (3, 4)), (3 * fox_nt, xa_nt, True, (5,)))
            defs = ((BF16, 0, fox_nt, 0), (F32, fox_nt, fox_nt, page), (BF16, fox_nt, fox_nt, 0),
                    (F32, 2 * fox_nt, fox_nt, page), (BF16, 2 * fox_nt, fox_nt, 0),
                    (BF16, 3 * fox_nt, xa_nt, 0))
            q16, k32, k16, v32, v16, xq16, lf_p, qa, ka = _proj(
                xp, norm1_g[i], w_main, gain, segs, defs, tm=TM, tn=TN, wf=wf, bf=bf, seq_len=tp,
                n_heads=fox_h)
            outs["fk_p"].append(k32.transpose(0, 2, 1, 3))
            outs["fv_p"].append(v32.transpose(0, 2, 1, 3))
            outs["fl_p"].append(lf_p[:, :fox_h].reshape(mp // page, page, fox_h))
            mix_p = _flash(q16.reshape(bp, tp, fox_w), qa.reshape(bp, tp, LANE), k16.reshape(bp, tp, fox_w),
                           ka.reshape(bp, tp, LANE), v16.reshape(bp, tp, fox_w), tq=TQ).reshape(mp, fox_w)

            defs_s = ((F32, 0, fox_nt, 0), (F32, fox_nt, fox_nt, 0),
                      (F32, 2 * fox_nt, fox_nt, 0), (F32, 3 * fox_nt, xa_nt, 0))
            segs_s = ((0, fox_nt, True, (0,)), (fox_nt, fox_nt, True, (1,)),
                      (2 * fox_nt, fox_nt, False, (2,)), (3 * fox_nt, xa_nt, True, (3,)))
            qs, ks, vs, xqs, lf_s = _proj(xs, norm1_g[i], w_main, gain, segs_s, defs_s,
                                          tm=ms, tn=TN, wf=wf, bf=bf)
            outs["fk_s"].append(ks.reshape(bs, ts, fox_h, fox_hd))
            outs["fv_s"].append(vs.reshape(bs, ts, fox_h, fox_hd))
            outs["fl_s"].append(lf_s[:, :fox_h].reshape(bs, ts, fox_h))
            lf_col = lf_s[:, :HEAD_ROWS].reshape(bs, HEAD_ROWS, 1)
            mix_s = _dec_fox(qs.reshape(bs, 1, fox_w), ks.reshape(bs, 1, fox_w), vs.reshape(bs, 1, fox_w),
                             lf_col, ck, cv, clf, page_table, j).reshape(ms, fox_w)
        else:
            w_main = s5_w_in[j].astype(BF16)
            s5_nt = (w_main.shape[1] - xa_w) // TN
            gain = _gain_row((jnp.ones((LANE,), F32), s5_nt * TN // LANE), (xq_norm_g[i], xa_h))
            segs = ((0, s5_nt, False, (0,)), (s5_nt, xa_nt, True, (1,)))
            pk = _s5_pack(s5_a_re[j], s5_a_im[j], s5_log_dt[j], s5_b_re[j], s5_b_im[j],
                          s5_c_re[j], s5_c_im[j], s5_d[j])
            w_glu = s5_w_glu[j].astype(BF16)
            b_glu = s5_b_glu[j].reshape(1, -1)
            u_p, xq16 = _proj(xp, norm1_g[i], w_main, gain, segs,
                              ((F32, 0, s5_nt, 0), (BF16, s5_nt, xa_nt, 0)), tm=TM, tn=TN)
            s5_w = u_p.shape[1]
            mix_p, hr, hi = _s5_prompt(u_p.reshape(bp, tp, s5_w), pk, w_glu, b_glu, lt=S5_LT)
            mix_p = mix_p.reshape(mp, s5_w)
            outs["sr_p"].append(_state_untile(hr, s5_g, s5_n))
            outs["si_p"].append(_state_untile(hi, s5_g, s5_n))
            u_s, xqs = _proj(xs, norm1_g[i], w_main, gain, segs,
                             ((F32, 0, s5_nt, 0), (F32, s5_nt, xa_nt, 0)), tm=ms, tn=TN)
            mix_s, hr, hi = _s5_sample(u_s, state_s5_re[j].reshape(bs, s5_g * s5_n),
                                       state_s5_im[j].reshape(bs, s5_g * s5_n), pk, w_glu, b_glu)
            outs["sr_s"].append(hr.reshape(bs, s5_g, s5_n))
            outs["si_s"].append(hi.reshape(bs, s5_g, s5_n))

        xa_s = _dec_xattn(_head_rows(xqs, xa_h), cmk, cmv, i, xa_h)[:, :xa_h].reshape(ms, xa_w)

        xp = _outproj_xattn(mix_p, xq16, mk16.reshape(bp, n_mem, xa_w), mv16.reshape(bp, n_mem, xa_w),
                            wo16, i, xp, tm=TM_OUT, seq_len=tp)
        xs = _outproj(mix_s, xa_s, wo16, i, xs, tm=ms, tn=d)
        xp = _ffn(xp, norm2_g[i], wg16, wu16, wd16, i, tm=TM_FFN, tf=TF)
        xs = _ffn(xs, norm2_g[i], wg16, wu16, wd16, i, tm=ms, tf=TF)

    st = lambda k: jnp.stack(outs[k])
    return (xp.reshape(bp, tp, d), xs.reshape(bs, ts, d),
            st("fk_p"), st("fv_p"), st("fl_p"), st("fk_s"), st("fv_s"), st("fl_s"),
            st("sr_p"), st("si_p"), st("sr_s"), st("si_s"), st("mk_p"), st("mv_p"))
```
